```python
import math
import jax, jax.numpy as jnp
from jax import lax
import numpy as np

D_MODEL = 2048
BATCH = 2
SEQ = 4096
DEPTH = 2

GRID_W = 64
ROPE_THETA = 10000.0
HEAD_DIM = 128
N_Q_HEADS = 8
N_KV_HEADS = 2
ATTN_W = N_Q_HEADS * HEAD_DIM
KV_W = N_KV_HEADS * HEAD_DIM
Q_BLOCK = 128
SSM_GROUP = 16
SSM_GROUPS = 64
SSM_W = SSM_GROUP * SSM_GROUPS
SSM_STATE = 64
DT_MIN = 1e-3
DT_MAX = 1e-1
MLP_CHUNK = 128
MLP_GROUPS = 8
MLP_GROUP_W = 128
MLP_W = MLP_GROUPS * MLP_GROUP_W
N_BRANCH = 3
BRANCH_W = 1024
N_IN = ATTN_W + 2 * KV_W + SSM_W + 2 * MLP_W + N_BRANCH * D_MODEL
D_FF = 5632
N_EXPERTS = 8
TOP_K = 2
D_FF_EXPERT = 7168
MOE_BLOCK = 256
N_DENSE = (DEPTH + 1) // 2
N_MOE = DEPTH // 2
PLE_DIM = 256
EPS = 1e-6

kernel_name = 'hybrid_gqa_s5_gmlp_moe_encoder'


def rms_norm(x, g):
    xf = x.astype(jnp.float32)
    y = xf * lax.rsqrt(jnp.mean(xf * xf, axis=-1, keepdims=True) + EPS)
    return (y * g.astype(jnp.float32)).astype(x.dtype)


def swiglu(h, w1, w3, w2):
    return (jax.nn.silu(h @ w1) * (h @ w3)) @ w2


def axial_rope_tables(L):
    rows = L // GRID_W
    t = jnp.arange(L)
    pos = jnp.stack([t // GRID_W - rows // 2, t % GRID_W - GRID_W // 2], axis=-1).astype(jnp.float32)
    n_freq = HEAD_DIM // 4
    inv_freq = ROPE_THETA ** (-jnp.arange(n_freq, dtype=jnp.float32) / n_freq)
    ang = pos[:, :, None] * inv_freq
    return jnp.cos(ang), jnp.sin(ang)


def apply_axial_rope(x, cos, sin):
    B, L, H, _ = x.shape
    xr = x.astype(jnp.float32).reshape(B, L, H, 2, 2, HEAD_DIM // 4)
    x1, x2 = xr[..., 0, :], xr[..., 1, :]
    c = cos[None, :, None]
    s = sin[None, :, None]
    out = jnp.stack([x1 * c - x2 * s, x2 * c + x1 * s], axis=-2)
    return out.reshape(B, L, H, HEAD_DIM).astype(x.dtype)


def axial_gqa_attention(q, k, v, q_gain, k_gain):
    B, L, _ = q.shape
    q = rms_norm(q.reshape(B, L, N_Q_HEADS, HEAD_DIM), q_gain)
    k = rms_norm(k.reshape(B, L, N_KV_HEADS, HEAD_DIM), k_gain)
    v = v.reshape(B, L, N_KV_HEADS, HEAD_DIM)
    cos, sin = axial_rope_tables(L)
    q = apply_axial_rope(q, cos, sin)
    k = apply_axial_rope(k, cos, sin)
    rep = N_Q_HEADS // N_KV_HEADS
    nb = L // Q_BLOCK
    qb = q.reshape(B, nb, Q_BLOCK, N_KV_HEADS, rep, HEAD_DIM).transpose(1, 0, 2, 3, 4, 5)
    scale = HEAD_DIM ** -0.5

    def block(qblk):
        s = jnp.einsum('bqgrd,bkgd->bgrqk', qblk, k, preferred_element_type=jnp.float32) * scale
        pr = jax.nn.softmax(s, axis=-1)
        return jnp.einsum('bgrqk,bkgd->bqgrd', pr.astype(v.dtype), v)

    o = lax.map(block, qb)
    return o.transpose(1, 0, 2, 3, 4, 5).reshape(B, L, ATTN_W)


def _ssm_combine(left, right):
    a_l, b_l = left
    a_r, b_r = right
    return a_r * a_l, a_r * b_l + b_r


def bidirectional_s5(u, lam_re, lam_im, log_dt, b_re, b_im, c_re, c_im, d_skip, w_glu):
    B, L, _ = u.shape
    uf = u.astype(jnp.float32).reshape(B, L, SSM_GROUPS, SSM_GROUP)
    lam = lax.complex(lam_re.astype(jnp.float32), lam_im.astype(jnp.float32))
    dt = jnp.exp(log_dt.astype(jnp.float32))[..., None]
    lam_bar = jnp.exp(lam * dt)
    b_cplx = lax.complex(b_re.astype(jnp.float32), b_im.astype(jnp.float32))
    b_bar = ((lam_bar - 1.0) / lam)[..., None] * b_cplx
    y = jnp.zeros((B, L, SSM_GROUPS, SSM_GROUP), jnp.float32)
    for dn, rev in ((0, False), (1, True)):
        bu = lax.complex(jnp.einsum('blgc,gpc->blgp', uf, jnp.real(b_bar[dn])),
                         jnp.einsum('blgc,gpc->blgp', uf, jnp.imag(b_bar[dn])))
        a = jnp.broadcast_to(lam_bar[dn], bu.shape)
        _, states = lax.associative_scan(_ssm_combine, (a, bu), axis=1, reverse=rev)
        y = y + jnp.einsum('blgp,gcp->blgc', jnp.real(states), c_re[dn].astype(jnp.float32)) \
              - jnp.einsum('blgp,gcp->blgc', jnp.imag(states), c_im[dn].astype(jnp.float32))
    y = y.reshape(B, L, SSM_W) + d_skip.astype(jnp.float32) * uf.reshape(B, L, SSM_W)
    y = jax.nn.gelu(y)
    y = y * jax.nn.sigmoid(y @ w_glu.astype(jnp.float32))
    return y.astype(u.dtype)


def chunked_spatial_gating(z_u, z_v, v_gain, w_s, b_s):
    B, L, _ = z_u.shape
    u = jax.nn.gelu(z_u)
    v = rms_norm(jax.nn.gelu(z_v), v_gain)
    nc = L // MLP_CHUNK
    vc = v.reshape(B, nc, MLP_CHUNK, MLP_GROUPS, MLP_GROUP_W)
    s = jnp.einsum('bnpgc,gqp->bnqgc', vc, w_s) + b_s.T[None, None, :, :, None]
    return u * s.reshape(B, L, MLP_W)


def mixer_sublayer(h, w_in, q_gain, k_gain, lam_re, lam_im, log_dt, b_re, b_im, c_re, c_im,
                   d_skip, w_glu, v_gain, w_s, b_s, w_branch, w_out):
    B, L, D = h.shape
    proj = h @ w_in
    cuts = [int(c) for c in np.cumsum([ATTN_W, KV_W, KV_W, SSM_W, MLP_W, MLP_W])]
    q, k, v, u_ssm, z_u, z_v, gate_logits = jnp.split(proj, cuts, axis=-1)
    attn = axial_gqa_attention(q, k, v, q_gain, k_gain)
    ssm = bidirectional_s5(u_ssm, lam_re, lam_im, log_dt, b_re, b_im, c_re, c_im, d_skip, w_glu)
    mlp = chunked_spatial_gating(z_u, z_v, v_gain, w_s, b_s)
    gates = jax.nn.sigmoid(gate_logits.astype(jnp.float32)).reshape(B, L, N_BRANCH, D)
    branches = jnp.stack([attn, ssm, mlp], axis=2)
    proj_b = jnp.einsum('blnc,ncd->blnd', branches, w_branch)
    merged = jnp.sum(gates * proj_b.astype(jnp.float32), axis=2).astype(h.dtype)
    return merged @ w_out


def moe_swiglu(h, w_router, e_w1, e_w3, e_w2):
    B, L, D = h.shape
    t = h.reshape(-1, D)
    N = t.shape[0]
    logits = (t @ w_router).astype(jnp.float32)
    top_val, top_idx = lax.top_k(logits, TOP_K)
    gates = jax.nn.softmax(top_val, axis=-1)
    NK = N * TOP_K
    e_flat = top_idx.reshape(-1)
    tok_flat = jnp.arange(NK) // TOP_K
    g_flat = gates.reshape(-1)
    order = jnp.argsort(e_flat)
    e_s, tok_s, g_s = e_flat[order], tok_flat[order], g_flat[order]
    counts = jnp.bincount(e_flat, length=N_EXPERTS)
    starts = jnp.cumsum(counts) - counts
    padded = (counts + MOE_BLOCK - 1) // MOE_BLOCK * MOE_BLOCK
    pends = jnp.cumsum(padded)
    pstarts = pends - padded
    dest = pstarts[e_s] + (jnp.arange(NK) - starts[e_s])
    n_blocks = -(-NK // MOE_BLOCK) + N_EXPERTS
    cap = n_blocks * MOE_BLOCK
    xbuf = jnp.zeros((cap, D), t.dtype).at[dest].set(t[tok_s])
    block_e = jnp.minimum(jnp.searchsorted(pends, jnp.arange(n_blocks) * MOE_BLOCK, side='right'), N_EXPERTS - 1)

    def expert_block(args):
        xb, e = args
        return swiglu(xb, e_w1[e], e_w3[e], e_w2[e])

    ybuf = lax.map(expert_block, (xbuf.reshape(n_blocks, MOE_BLOCK, D), block_e)).reshape(cap, D)
    y = ybuf[dest] * g_s[:, None].astype(ybuf.dtype)
    out = jnp.zeros((N, D), t.dtype).at[tok_s].add(y.astype(t.dtype))
    return out.reshape(B, L, D)


def setup_inputs(seed: int = 0) -> dict:
    key = jax.random.key(seed)
    ks = iter(jax.random.split(key, 40))
    f32 = jnp.float32

    def nrm(shape, scale):
        return jax.random.normal(next(ks), shape, f32) * scale

    def gain(shape):
        return 1.0 + nrm(shape, 0.01)

    ssm_shape = (DEPTH, 2, SSM_GROUPS, SSM_STATE)
    return {
        'x': nrm((BATCH, SEQ, D_MODEL), 1.0),
        'p': nrm((DEPTH, BATCH, SEQ, PLE_DIM), 1.0),
        'mix_norm': gain((DEPTH, D_MODEL)),
        'w_in': nrm((DEPTH, D_MODEL, N_IN), D_MODEL ** -0.5),
        'q_norm': gain((DEPTH, HEAD_DIM)),
        'k_norm': gain((DEPTH, HEAD_DIM)),
        'ssm_lambda_re': -0.5 + nrm(ssm_shape, 0.01),
        'ssm_lambda_im': jnp.pi * jnp.arange(SSM_STATE, dtype=f32) + nrm(ssm_shape, 0.01),
        'ssm_log_dt': jax.random.uniform(next(ks), (DEPTH, 2, SSM_GROUPS), f32, math.log(DT_MIN), math.log(DT_MAX)),
        'ssm_b_re': nrm((DEPTH, 2, SSM_GROUPS, SSM_STATE, SSM_GROUP), (2 * SSM_GROUP) ** -0.5),
        'ssm_b_im': nrm((DEPTH, 2, SSM_GROUPS, SSM_STATE, SSM_GROUP), (2 * SSM_GROUP) ** -0.5),
        'ssm_c_re': nrm((DEPTH, 2, SSM_GROUPS, SSM_GROUP, SSM_STATE), (2 * SSM_STATE) ** -0.5),
        'ssm_c_im': nrm((DEPTH, 2, SSM_GROUPS, SSM_GROUP, SSM_STATE), (2 * SSM_STATE) ** -0.5),
        'ssm_d': nrm((DEPTH, SSM_W), 1.0),
        'ssm_glu_w': nrm((DEPTH, SSM_W, SSM_W), SSM_W ** -0.5),
        'gmlp_v_norm': gain((DEPTH, MLP_W)),
        'gmlp_ws': nrm((DEPTH, MLP_GROUPS, MLP_CHUNK, MLP_CHUNK), MLP_CHUNK ** -0.5),
        'gmlp_b': 1.0 + nrm((DEPTH, MLP_GROUPS, MLP_CHUNK), 0.1),
        'w_branch': nrm((DEPTH, N_BRANCH, BRANCH_W, D_MODEL), BRANCH_W ** -0.5),
        'w_out': nrm((DEPTH, D_MODEL, D_MODEL), D_MODEL ** -0.5),
        'ffn_norm': gain((DEPTH, D_MODEL)),
        'dense_w1': nrm((N_DENSE, D_MODEL, D_FF), D_MODEL ** -0.5),
        'dense_w3': nrm((N_DENSE, D_MODEL, D_FF), D_MODEL ** -0.5),
        'dense_w2': nrm((N_DENSE, D_FF, D_MODEL), D_FF ** -0.5),
        'router_w': nrm((N_MOE, D_MODEL, N_EXPERTS), D_MODEL ** -0.5),
        'expert_w1': nrm((N_MOE, N_EXPERTS, D_MODEL, D_FF_EXPERT), D_MODEL ** -0.5),
        'expert_w3': nrm((N_MOE, N_EXPERTS, D_MODEL, D_FF_EXPERT), D_MODEL ** -0.5),
        'expert_w2': nrm((N_MOE, N_EXPERTS, D_FF_EXPERT, D_MODEL), D_FF_EXPERT ** -0.5),
        'ple_norm': gain((DEPTH, D_MODEL)),
        'ple_gate_w': nrm((DEPTH, D_MODEL, D_MODEL), D_MODEL ** -0.5),
        'ple_proj_w': nrm((DEPTH, PLE_DIM, D_MODEL), PLE_DIM ** -0.5),
    }


def reference(x, p, mix_norm, w_in, q_norm, k_norm, ssm_lambda_re, ssm_lambda_im, ssm_log_dt,
              ssm_b_re, ssm_b_im, ssm_c_re, ssm_c_im, ssm_d, ssm_glu_w, gmlp_v_norm, gmlp_ws, gmlp_b,
              w_branch, w_out, ffn_norm, dense_w1, dense_w3, dense_w2, router_w, expert_w1, expert_w3,
              expert_w2, ple_norm, ple_gate_w, ple_proj_w):
    for i in range(DEPTH):
        h = rms_norm(x, mix_norm[i])
        x = x + mixer_sublayer(h, w_in[i], q_norm[i], k_norm[i], ssm_lambda_re[i], ssm_lambda_im[i],
                               ssm_log_dt[i], ssm_b_re[i], ssm_b_im[i], ssm_c_re[i], ssm_c_im[i], ssm_d[i],
                               ssm_glu_w[i], gmlp_v_norm[i], gmlp_ws[i], gmlp_b[i], w_branch[i], w_out[i])
        h = rms_norm(x, ffn_norm[i])
        j = i // 2
        if i % 2 == 0:
            x = x + swiglu(h, dense_w1[j], dense_w3[j], dense_w2[j])
        else:
            x = x + moe_swiglu(h, router_w[j], expert_w1[j], expert_w3[j], expert_w2[j])
        g = jax.nn.sigmoid((rms_norm(x, ple_norm[i]) @ ple_gate_w[i]).astype(jnp.float32))
        x = x + (g * (p[i] @ ple_proj_w[i]).astype(jnp.float32)).astype(x.dtype)
    return x
```

```python
import functools
import math

import jax
import jax.numpy as jnp
import numpy as np
from jax import lax
from jax.experimental import pallas as pl
from jax.experimental.pallas import tpu as pltpu

GRID_W = 64
ROPE_THETA = 10000.0
HEAD_DIM = 128
N_Q_HEADS = 8
N_KV_HEADS = 2
ATTN_W = N_Q_HEADS * HEAD_DIM
KV_W = N_KV_HEADS * HEAD_DIM
SSM_GROUP = 16
SSM_GROUPS = 64
SSM_W = SSM_GROUP * SSM_GROUPS
SSM_STATE = 64
MLP_CHUNK = 128
MLP_GROUPS = 8
MLP_GROUP_W = 128
MLP_W = MLP_GROUPS * MLP_GROUP_W
N_BRANCH = 3
N_EXPERTS = 8
TOP_K = 2
EPS = 1e-6

Q_OFF = 0
K_OFF = ATTN_W
V_OFF = K_OFF + KV_W
U_OFF = V_OFF + KV_W
ZU_OFF = U_OFF + SSM_W
ZV_OFF = ZU_OFF + MLP_W
GATE_OFF = ZV_OFF + MLP_W

V7X_VMEM_LIMIT_BYTES = 60 * 1024 * 1024
LANES = 128

BF16 = jnp.bfloat16
F32 = jnp.float32


def _params(*sem):
    return pltpu.CompilerParams(dimension_semantics=sem, vmem_limit_bytes=V7X_VMEM_LIMIT_BYTES)


def _gelu(x):
    c = math.sqrt(2.0 / math.pi)
    return 0.5 * x * (1.0 + jnp.tanh(c * (x + 0.044715 * (x * x * x))))


def _sigmoid(x):
    return 1.0 / (1.0 + jnp.exp(-x))


def _silu(x):
    return x * _sigmoid(x)


def _add_norm_kernel(*refs, has_delta):
    if has_delta:
        x_ref, d_ref, g_ref, xo_ref, h_ref = refs
        x = x_ref[...] + d_ref[...]
        xo_ref[...] = x
    else:
        x_ref, g_ref, h_ref = refs
        x = x_ref[...]
    y = x * lax.rsqrt(jnp.mean(x * x, axis=-1, keepdims=True) + EPS)
    h_ref[...] = (y * g_ref[...]).astype(h_ref.dtype)


def add_norm(x, delta, gains, layer, tm=512):
    m, d = x.shape
    row = pl.BlockSpec((tm, d), lambda i: (i, 0))
    gspec = pl.BlockSpec((None, 1, d), lambda i: (layer, 0, 0))
    if delta is None:
        h = pl.pallas_call(
            functools.partial(_add_norm_kernel, has_delta=False),
            grid=(m // tm,), in_specs=[row, gspec], out_specs=row,
            out_shape=jax.ShapeDtypeStruct((m, d), BF16),
            compiler_params=_params("parallel"), name="norm")(x, gains)
        return x, h
    xo, h = pl.pallas_call(
        functools.partial(_add_norm_kernel, has_delta=True),
        grid=(m // tm,), in_specs=[row, row, gspec], out_specs=[row, row],
        out_shape=[jax.ShapeDtypeStruct((m, d), F32), jax.ShapeDtypeStruct((m, d), BF16)],
        compiler_params=_params("parallel"), name="add_norm")(x, delta, gains)
    return xo, h


def _fused_mm_kernel(*refs, n_dots, n_extras, epilogue):
    a_refs = refs[:n_dots]
    w_refs = refs[n_dots:2 * n_dots]
    e_refs = refs[2 * n_dots:2 * n_dots + n_extras]
    o_ref = refs[2 * n_dots + n_extras]
    wb_refs = refs[2 * n_dots + n_extras + 1:]

    @pl.when(pl.program_id(1) == 0)
    def _():
        for w_ref, wb_ref in zip(w_refs, wb_refs):
            wb_ref[...] = w_ref[...].astype(BF16)

    dots = []
    for a_ref, wb_ref in zip(a_refs, wb_refs):
        a = a_ref[...]
        if a.dtype != BF16:
            a = a.astype(BF16)
        dots.append(jnp.dot(a, wb_ref[...], preferred_element_type=F32))
    extras = [e_ref[...] for e_ref in e_refs]
    o_ref[...] = epilogue(dots, extras).astype(o_ref.dtype)


def fused_mm(m, n, dots, extras, epilogue, out_dtype, tm, tn, name):
    in_specs, args, scratch = [], [], []
    for a, acb, k, _, _, _ in dots:
        in_specs.append(pl.BlockSpec((tm, k), lambda j, i, acb=acb: (i, acb)))
        args.append(a)
    for _, _, k, w, lead, off in dots:
        in_specs.append(pl.BlockSpec((None,) * len(lead) + (k, tn),
                                     lambda j, i, lead=tuple(lead), off=off: lead + (0, off + j)))
        args.append(w)
        scratch.append(pltpu.VMEM((k, tn), BF16))
    for e, off in extras:
        in_specs.append(pl.BlockSpec((tm, tn), lambda j, i, off=off: (i, off + j)))
        args.append(e)
    return pl.pallas_call(
        functools.partial(_fused_mm_kernel, n_dots=len(dots), n_extras=len(extras), epilogue=epilogue),
        grid=(n // tn, m // tm), in_specs=in_specs,
        out_specs=pl.BlockSpec((tm, tn), lambda j, i: (i, j)),
        out_shape=jax.ShapeDtypeStruct((m, n), out_dtype),
        scratch_shapes=scratch,
        compiler_params=_params("arbitrary", "arbitrary"), name=name)(*args)


def _rope(x, c, s):
    lane = lax.broadcasted_iota(jnp.int32, x.shape, x.ndim - 1)
    quarter = HEAD_DIM // 4
    partner = jnp.where((lane % (2 * quarter)) < quarter,
                        pltpu.roll(x, HEAD_DIM - quarter, x.ndim - 1),
                        pltpu.roll(x, quarter, x.ndim - 1))
    return x * c + partner * s


def _head_norm(x, g):
    return x * lax.rsqrt(jnp.mean(x * x, axis=-1, keepdims=True) + EPS) * g


def _attn_kernel(q_ref, k_ref, v_ref, cq_ref, sq_ref, ck_ref, sk_ref, qg_ref, kg_ref, o_ref, ks_ref, *, rep):
    @pl.when(pl.program_id(2) == 0)
    def _():
        k = _head_norm(k_ref[...].astype(F32), kg_ref[...])
        ks_ref[...] = _rope(k, ck_ref[...], sk_ref[...]).astype(BF16)

    scale = HEAD_DIM ** -0.5
    cq = cq_ref[...]
    sq = sq_ref[...]
    for hh in range(rep):
        sl = slice(hh * HEAD_DIM, (hh + 1) * HEAD_DIM)
        q = _head_norm(q_ref[:, sl].astype(F32), qg_ref[...])
        q = (_rope(q, cq, sq) * scale).astype(BF16)
        s = lax.dot_general(q, ks_ref[...], (((1,), (1,)), ((), ())), preferred_element_type=F32)
        m = jnp.max(s, axis=-1, keepdims=True)
        p = jnp.exp(s - m)
        l = jnp.sum(p, axis=-1, keepdims=True)
        o = jnp.dot(p.astype(BF16), v_ref[...], preferred_element_type=F32)
        o_ref[:, sl] = (o / l).astype(o_ref.dtype)


def attention(proj, q_gain, k_gain, layer, rope_c, rope_s, batch, seq, tq=256):
    rep = N_Q_HEADS // N_KV_HEADS
    qw = rep * HEAD_DIM
    nq = seq // tq
    gspec = pl.BlockSpec((None, 1, HEAD_DIM), lambda b, g, i: (layer, 0, 0))
    return pl.pallas_call(
        functools.partial(_attn_kernel, rep=rep),
        grid=(batch, N_KV_HEADS, nq),
        in_specs=[
            pl.BlockSpec((tq, qw), lambda b, g, i: (b * nq + i, Q_OFF // qw + g)),
            pl.BlockSpec((seq, HEAD_DIM), lambda b, g, i: (b, K_OFF // HEAD_DIM + g)),
            pl.BlockSpec((seq, HEAD_DIM), lambda b, g, i: (b, V_OFF // HEAD_DIM + g)),
            pl.BlockSpec((tq, HEAD_DIM), lambda b, g, i: (i, 0)),
            pl.BlockSpec((tq, HEAD_DIM), lambda b, g, i: (i, 0)),
            pl.BlockSpec((seq, HEAD_DIM), lambda b, g, i: (0, 0)),
            pl.BlockSpec((seq, HEAD_DIM), lambda b, g, i: (0, 0)),
            gspec, gspec,
        ],
        out_specs=pl.BlockSpec((tq, qw), lambda b, g, i: (b * nq + i, g)),
        out_shape=jax.ShapeDtypeStruct((batch * seq, ATTN_W), BF16),
        scratch_shapes=[pltpu.VMEM((seq, HEAD_DIM), BF16)],
        compiler_params=_params("arbitrary", "arbitrary", "arbitrary"), name="attention",
    )(proj, proj, proj, rope_c, rope_s, rope_c, rope_s, q_gain, k_gain)


def rope_tables(seq):
    rows = seq // GRID_W
    t = jnp.arange(seq)
    pos = jnp.stack([t // GRID_W - rows // 2, t % GRID_W - GRID_W // 2], axis=-1).astype(F32)
    n_freq = HEAD_DIM // 4
    inv_freq = ROPE_THETA ** (-jnp.arange(n_freq, dtype=F32) / n_freq)
    ang = pos[:, :, None] * inv_freq
    cos, sin = jnp.cos(ang), jnp.sin(ang)
    c = jnp.concatenate([cos[:, 0], cos[:, 0], cos[:, 1], cos[:, 1]], axis=-1)
    s = jnp.concatenate([-sin[:, 0], sin[:, 0], -sin[:, 1], sin[:, 1]], axis=-1)
    return c, s


SSM_SEGS = 8
SSM_GB = 8
SSM_CB = SSM_GB * SSM_STATE
SSM_UB = SSM_GB * SSM_GROUP
SSM_TB = 64


def _ssm_kernel(*refs, nseg_rows, reverse, final_pass, n_seq):
    if final_pass:
        u_ref, bm_ref, ar_ref, ai_ref, init_ref, cm_ref, y_ref, s_ref, st_ref = refs
    else:
        u_ref, bm_ref, ar_ref, ai_ref, apr_ref, api_ref, init_out_ref, s_ref, st_ref = refs
    tb = pl.program_id(1)
    ntb = pl.num_programs(1)
    ns = nseg_rows
    cb = SSM_CB

    @pl.when(tb == 0)
    def _():
        if final_pass:
            st_ref[...] = init_ref[...]
        else:
            st_ref[...] = jnp.zeros_like(st_ref)

    s_ref[...] = jnp.dot(u_ref[...], bm_ref[...], preferred_element_type=F32)
    arb = jnp.broadcast_to(ar_ref[...], (ns, cb))
    aib = jnp.broadcast_to(ai_ref[...], (ns, cb))

    def step(i, carry):
        sr, si = carry
        t = (SSM_TB - 1 - i) if reverse else i
        r0 = pl.multiple_of(t * ns, ns)
        nr = arb * sr - aib * si + s_ref[pl.ds(r0, ns), 0:cb]
        ni = arb * si + aib * sr + s_ref[pl.ds(r0, ns), cb:2 * cb]
        if final_pass:
            s_ref[pl.ds(r0, ns), 0:cb] = nr
            s_ref[pl.ds(r0, ns), cb:2 * cb] = ni
        return nr, ni

    sr, si = lax.fori_loop(0, SSM_TB, step, (st_ref[:, 0:cb], st_ref[:, cb:2 * cb]), unroll=2)
    st_ref[:, 0:cb] = sr
    st_ref[:, cb:2 * cb] = si

    if final_pass:
        y_ref[...] = jnp.dot(s_ref[...].astype(BF16), cm_ref[...], preferred_element_type=F32)
    else:
        @pl.when(tb == ntb - 1)
        def _():
            apr = apr_ref[...]
            api = api_ref[...]
            zero = jnp.zeros((1, cb), F32)
            for b in range(n_seq):
                order = range(SSM_SEGS - 1, -1, -1) if reverse else range(SSM_SEGS)
                cr, ci = zero, zero
                for q in order:
                    row = b * SSM_SEGS + q
                    init_out_ref[pl.ds(row, 1), 0:cb] = cr
                    init_out_ref[pl.ds(row, 1), cb:2 * cb] = ci
                    er = st_ref[pl.ds(row, 1), 0:cb]
                    ei = st_ref[pl.ds(row, 1), cb:2 * cb]
                    cr, ci = apr * cr - api * ci + er, apr * ci + api * cr + ei


def _ssm_pass(u_perm, bmat, a_re, a_im, d, *, n_seq, seg_len, reverse, final_pass, extra):
    ns = n_seq * SSM_SEGS
    ncb = SSM_GROUPS // SSM_GB
    ntb = seg_len // SSM_TB
    rows = SSM_TB * ns

    def tmap(tb):
        return (ntb - 1 - tb) if reverse else tb

    vec = pl.BlockSpec((None, None, 1, SSM_CB), lambda j, tb: (d, j, 0, 0))
    in_specs = [
        pl.BlockSpec((rows, SSM_UB), lambda j, tb: (tmap(tb), j)),
        pl.BlockSpec((None, None, SSM_UB, 2 * SSM_CB), lambda j, tb: (d, j, 0, 0)),
        vec, vec,
    ]
    st_spec = pl.BlockSpec((None, ns, 2 * SSM_CB), lambda j, tb: (j, 0, 0))
    scratch = [pltpu.VMEM((rows, 2 * SSM_CB), F32), pltpu.VMEM((ns, 2 * SSM_CB), F32)]
    kern = functools.partial(_ssm_kernel, nseg_rows=ns, reverse=reverse, final_pass=final_pass, n_seq=n_seq)
    if final_pass:
        init, cmat = extra
        in_specs += [st_spec, pl.BlockSpec((None, None, 2 * SSM_CB, SSM_UB), lambda j, tb: (d, j, 0, 0))]
        return pl.pallas_call(
            kern, grid=(ncb, ntb), in_specs=in_specs,
            out_specs=pl.BlockSpec((rows, SSM_UB), lambda j, tb: (tmap(tb), j)),
            out_shape=jax.ShapeDtypeStruct((seg_len * ns, SSM_W), F32),
            scratch_shapes=scratch,
            compiler_params=_params("arbitrary", "arbitrary"), name="ssm_scan",
        )(u_perm, bmat, a_re, a_im, init, cmat)
    ap_re, ap_im = extra
    in_specs += [vec, vec]
    return pl.pallas_call(
        kern, grid=(ncb, ntb), in_specs=in_specs, out_specs=st_spec,
        out_shape=jax.ShapeDtypeStruct((ncb, ns, 2 * SSM_CB), F32),
        scratch_shapes=scratch,
        compiler_params=_params("arbitrary", "arbitrary"), name="ssm_carry",
    )(u_perm, bmat, a_re, a_im, ap_re, ap_im)


def _ssm_tables(lam_re, lam_im, log_dt, b_re, b_im, c_re, c_im, seg_len):
    ncb = SSM_GROUPS // SSM_GB
    lam = lax.complex(lam_re.astype(F32), lam_im.astype(F32))
    dt = jnp.exp(log_dt.astype(F32))[..., None]
    lam_bar = jnp.exp(lam * dt)
    b_bar = ((lam_bar - 1.0) / lam)[..., None] * lax.complex(b_re.astype(F32), b_im.astype(F32))
    lam_pow = jnp.exp(lam * dt * seg_len)
    eye = jnp.eye(SSM_GB, dtype=F32)

    def bm(part):
        x = part.reshape(2, ncb, SSM_GB, SSM_STATE, SSM_GROUP)
        x = jnp.einsum('djgpc,gh->djgchp', x, eye)
        return x.reshape(2, ncb, SSM_UB, SSM_CB)

    def cm(part):
        x = part.reshape(2, ncb, SSM_GB, SSM_GROUP, SSM_STATE)
        x = jnp.einsum('djgcp,gh->djgphc', x, eye)
        return x.reshape(2, ncb, SSM_CB, SSM_UB)

    bmat = jnp.concatenate([bm(jnp.real(b_bar)), bm(jnp.imag(b_bar))], axis=-1).astype(BF16)
    cmat = jnp.concatenate([cm(c_re.astype(F32)), -cm(c_im.astype(F32))], axis=-2).astype(BF16)

    def vec(x):
        return x.reshape(2, ncb, 1, SSM_CB)

    return (bmat, cmat, vec(jnp.real(lam_bar)), vec(jnp.imag(lam_bar)),
            vec(jnp.real(lam_pow)), vec(jnp.imag(lam_pow)))


def _ssm_act_kernel(yf_ref, yb_ref, u_ref, d_ref, o32_ref, o16_ref):
    y = yf_ref[...] + yb_ref[...] + d_ref[...] * u_ref[...].astype(F32)
    y = _gelu(y)
    o32_ref[...] = y
    o16_ref[...] = y.astype(BF16)


def ssm_branch(proj, layer, tabs, d_skip, w_glu, batch, seq):
    bmat, cmat, a_re, a_im, ap_re, ap_im = tabs
    seg_len = seq // SSM_SEGS
    ns = batch * SSM_SEGS
    m = batch * seq
    u = proj[:, U_OFF:U_OFF + SSM_W].reshape(ns, seg_len, SSM_W)
    u_perm = jnp.transpose(u, (1, 0, 2)).reshape(m, SSM_W)
    ys = []
    for d, reverse in ((0, False), (1, True)):
        kw = dict(n_seq=batch, seg_len=seg_len, reverse=reverse)
        init = _ssm_pass(u_perm, bmat, a_re, a_im, d, final_pass=False, extra=(ap_re, ap_im), **kw)
        ys.append(_ssm_pass(u_perm, bmat, a_re, a_im, d, final_pass=True, extra=(init, cmat), **kw))
    tm = 512
    row = pl.BlockSpec((tm, SSM_W), lambda i: (i, 0))
    y32, y16 = pl.pallas_call(
        _ssm_act_kernel, grid=(m // tm,),
        in_specs=[row, row, row, pl.BlockSpec((None, 1, SSM_W), lambda i: (layer, 0, 0))],
        out_specs=[row, row],
        out_shape=[jax.ShapeDtypeStruct((m, SSM_W), F32), jax.ShapeDtypeStruct((m, SSM_W), BF16)],
        compiler_params=_params("parallel"), name="ssm_act")(ys[0], ys[1], u_perm, d_skip)
    out = fused_mm(m, SSM_W, [(y16, 0, SSM_W, w_glu, (layer,), 0)], [(y32, 0)],
                   lambda dts, ex: ex[0] * _sigmoid(dts[0]), BF16, tm=1024, tn=512, name="ssm_glu")
    out = out.reshape(seg_len, ns, SSM_W)
    return jnp.transpose(out, (1, 0, 2)).reshape(m, SSM_W)


GMLP_NC = 4


def _gmlp_kernel(zu0_ref, zu1_ref, zv0_ref, zv1_ref, g_ref, ws_ref, bb_ref, o_ref):
    half = MLP_W // 2
    zv = jnp.concatenate([zv0_ref[...], zv1_ref[...]], axis=-1).astype(F32)
    v = _gelu(zv)
    v = (v * lax.rsqrt(jnp.mean(v * v, axis=-1, keepdims=True) + EPS) * g_ref[...]).astype(BF16)
    for g in range(MLP_GROUPS):
        cs = slice(g * MLP_GROUP_W, (g + 1) * MLP_GROUP_W)
        vg = jnp.concatenate([v[n * MLP_CHUNK:(n + 1) * MLP_CHUNK, cs] for n in range(GMLP_NC)], axis=-1)
        s = jnp.dot(ws_ref[g].astype(BF16), vg, preferred_element_type=F32)
        zu_ref = zu0_ref if g * MLP_GROUP_W < half else zu1_ref
        us = slice((g * MLP_GROUP_W) % half, (g * MLP_GROUP_W) % half + MLP_GROUP_W)
        for n in range(GMLP_NC):
            rs = slice(n * MLP_CHUNK, (n + 1) * MLP_CHUNK)
            sn = s[:, n * MLP_GROUP_W:(n + 1) * MLP_GROUP_W] + bb_ref[g]
            o_ref[rs, cs] = (_gelu(zu_ref[rs, us].astype(F32)) * sn).astype(o_ref.dtype)


def gmlp_branch(proj, layer, v_gain, w_s, b_bcast, m):
    rows = GMLP_NC * MLP_CHUNK
    half = MLP_W // 2

    def zspec(off):
        return pl.BlockSpec((rows, half), lambda i, off=off: (i, off // half))

    return pl.pallas_call(
        _gmlp_kernel, grid=(m // rows,),
        in_specs=[zspec(ZU_OFF), zspec(ZU_OFF + half), zspec(ZV_OFF), zspec(ZV_OFF + half),
                  pl.BlockSpec((None, 1, MLP_W), lambda i: (layer, 0, 0)),
                  pl.BlockSpec((None, MLP_GROUPS, MLP_CHUNK, MLP_CHUNK), lambda i: (layer, 0, 0, 0)),
                  pl.BlockSpec((None, MLP_GROUPS, MLP_CHUNK, MLP_GROUP_W), lambda i: (layer, 0, 0, 0))],
        out_specs=pl.BlockSpec((rows, MLP_W), lambda i: (i, 0)),
        out_shape=jax.ShapeDtypeStruct((m, MLP_W), BF16),
        compiler_params=_params("parallel"), name="gmlp",
    )(proj, proj, proj, proj, v_gain, w_s, b_bcast)


FFN_TM = 1024
FFN_SUB = 256
FFN_TF = 256


def _ffn_kernel(ue_ref, un_ref, x_ref, w1_ref, w3_ref, w2_ref, o_ref):
    u = pl.program_id(0)
    j = pl.program_id(1)
    nsub = un_ref[u]

    @pl.when(j == 0)
    def _():
        o_ref[...] = jnp.zeros_like(o_ref)

    @pl.when(nsub > 0)
    def _():
        w1 = w1_ref[...].astype(BF16)
        w3 = w3_ref[...].astype(BF16)
        w2 = w2_ref[...].astype(BF16)

        def sub(r, c):
            r0 = pl.multiple_of(r * FFN_SUB, FFN_SUB)
            x = x_ref[pl.ds(r0, FFN_SUB), :]
            if x.dtype != BF16:
                x = x.astype(BF16)
            h1 = jnp.dot(x, w1, preferred_element_type=F32)
            h3 = jnp.dot(x, w3, preferred_element_type=F32)
            act = (_silu(h1) * h3).astype(BF16)
            o_ref[pl.ds(r0, FFN_SUB), :] += jnp.dot(act, w2, preferred_element_type=F32)
            return c

        lax.fori_loop(0, nsub, sub, 0)


def swiglu_ffn(x, unit_expert, unit_nsub, w1, w3, w2, lead):
    rows, d = x.shape
    f = w1.shape[-1]
    n_units = rows // FFN_TM
    nf = f // FFN_TF
    nl = len(lead)

    def wmap_up(u, j, ue, un):
        live = un[u] > 0
        return tuple(lead) + (ue[u], 0, jnp.where(live, j, nf - 1))

    def wmap_down(u, j, ue, un):
        live = un[u] > 0
        return tuple(lead) + (ue[u], jnp.where(live, j, nf - 1), 0)

    grid_spec = pltpu.PrefetchScalarGridSpec(
        num_scalar_prefetch=2, grid=(n_units, nf),
        in_specs=[pl.BlockSpec((FFN_TM, d), lambda u, j, ue, un: (u, 0)),
                  pl.BlockSpec((None,) * (nl + 1) + (d, FFN_TF), wmap_up),
                  pl.BlockSpec((None,) * (nl + 1) + (d, FFN_TF), wmap_up),
                  pl.BlockSpec((None,) * (nl + 1) + (FFN_TF, d), wmap_down)],
        out_specs=pl.BlockSpec((FFN_TM, d), lambda u, j, ue, un: (u, 0)))
    return pl.pallas_call(
        _ffn_kernel, grid_spec=grid_spec,
        out_shape=jax.ShapeDtypeStruct((rows, d), F32),
        compiler_params=_params("arbitrary", "arbitrary"), name="swiglu_ffn",
    )(unit_expert, unit_nsub, x, w1, w3, w2)


def _router_kernel(x_ref, g_ref, wr_ref, h_ref, r_ref):
    x = x_ref[...]
    h = x * lax.rsqrt(jnp.mean(x * x, axis=-1, keepdims=True) + EPS) * g_ref[...]
    h_ref[...] = h
    logits = jnp.dot(h, wr_ref[...], preferred_element_type=F32, precision=lax.Precision.HIGHEST)
    lane = lax.broadcasted_iota(jnp.int32, logits.shape, 1)
    neg = jnp.float32(-jnp.inf)
    logits = jnp.where(lane < N_EXPERTS, logits, neg)
    m1 = jnp.max(logits, axis=-1, keepdims=True)
    i1 = jnp.min(jnp.where(logits == m1, lane, LANES), axis=-1, keepdims=True)
    rest = jnp.where(lane == i1, neg, logits)
    m2 = jnp.max(rest, axis=-1, keepdims=True)
    i2 = jnp.min(jnp.where(rest == m2, lane, LANES), axis=-1, keepdims=True)
    e = jnp.exp(m2 - m1)
    g1 = 1.0 / (1.0 + e)
    g2 = e / (1.0 + e)
    r_ref[...] = jnp.where(lane == 0, i1.astype(F32),
                           jnp.where(lane == 1, i2.astype(F32),
                                     jnp.where(lane == 2, g1, jnp.where(lane == 3, g2, 0.0))))


def norm_router(x, gains, layer, w_router_pad, tm=256):
    m, d = x.shape
    row = pl.BlockSpec((tm, d), lambda i: (i, 0))
    return pl.pallas_call(
        _router_kernel, grid=(m // tm,),
        in_specs=[row, pl.BlockSpec((None, 1, d), lambda i: (layer, 0, 0)),
                  pl.BlockSpec((d, LANES), lambda i: (0, 0))],
        out_specs=[row, pl.BlockSpec((tm, LANES), lambda i: (i, 0))],
        out_shape=[jax.ShapeDtypeStruct((m, d), F32), jax.ShapeDtypeStruct((m, LANES), F32)],
        compiler_params=_params("parallel"), name="norm_router")(x, gains, w_router_pad)


GATHER_TB = 256


def _gather_kernel(dest_ref, h_ref, xin_ref, xout_ref, sem):
    del xin_ref
    base = pl.program_id(0) * GATHER_TB

    def copy(tok, k):
        return pltpu.make_async_copy(h_ref.at[pl.ds(tok, 1)], xout_ref.at[pl.ds(dest_ref[TOP_K * tok + k], 1)], sem)

    def issue(t, c):
        for k in range(TOP_K):
            copy(base + t, k).start()
        return c

    def drain(t, c):
        for k in range(TOP_K):
            copy(base + t, k).wait()
        return c

    lax.fori_loop(0, GATHER_TB, issue, 0)
    lax.fori_loop(0, GATHER_TB, drain, 0)


def moe_gather(h, dest, cap):
    m, d = h.shape
    xbuf = jnp.zeros((cap, d), h.dtype)
    grid_spec = pltpu.PrefetchScalarGridSpec(
        num_scalar_prefetch=1, grid=(m // GATHER_TB,),
        in_specs=[pl.BlockSpec(memory_space=pl.ANY), pl.BlockSpec(memory_space=pl.ANY)],
        out_specs=pl.BlockSpec(memory_space=pl.ANY),
        scratch_shapes=[pltpu.SemaphoreType.DMA(())])
    return pl.pallas_call(
        _gather_kernel, grid_spec=grid_spec,
        out_shape=jax.ShapeDtypeStruct((cap, d), h.dtype),
        input_output_aliases={2: 0},
        compiler_params=_params("arbitrary"), name="moe_gather")(dest, h, xbuf)


COMBINE_TB = 128


def _combine_kernel(dest_ref, x_ref, r_ref, g_ref, y_ref, xo_ref, h_ref, buf_ref, sem):
    base = pl.program_id(0) * COMBINE_TB

    def copy(t, k):
        return pltpu.make_async_copy(y_ref.at[pl.ds(dest_ref[TOP_K * (base + t) + k], 1)],
                                     buf_ref.at[k, pl.ds(t, 1)], sem)

    def issue(t, c):
        for k in range(TOP_K):
            copy(t, k).start()
        return c

    def drain(t, c):
        for k in range(TOP_K):
            copy(t, k).wait()
        return c

    lax.fori_loop(0, COMBINE_TB, issue, 0)
    lax.fori_loop(0, COMBINE_TB, drain, 0)
    r = r_ref[...]
    g1 = r[:, 2:3]
    g2 = r[:, 3:4]
    x = x_ref[...] + (buf_ref[0] * g1 + buf_ref[1] * g2)
    xo_ref[...] = x
    y = x * lax.rsqrt(jnp.mean(x * x, axis=-1, keepdims=True) + EPS)
    h_ref[...] = (y * g_ref[...]).astype(h_ref.dtype)


def moe_combine(x, route, dest, ybuf, gains, layer):
    m, d = x.shape
    row = pl.BlockSpec((COMBINE_TB, d), lambda i, dr: (i, 0))
    grid_spec = pltpu.PrefetchScalarGridSpec(
        num_scalar_prefetch=1, grid=(m // COMBINE_TB,),
        in_specs=[row, pl.BlockSpec((COMBINE_TB, LANES), lambda i, dr: (i, 0)),
                  pl.BlockSpec((None, 1, d), lambda i, dr: (layer, 0, 0)),
                  pl.BlockSpec(memory_space=pl.ANY)],
        out_specs=[row, row],
        scratch_shapes=[pltpu.VMEM((TOP_K, COMBINE_TB, d), F32), pltpu.SemaphoreType.DMA(())])
    return pl.pallas_call(
        _combine_kernel, grid_spec=grid_spec,
        out_shape=[jax.ShapeDtypeStruct((m, d), F32), jax.ShapeDtypeStruct((m, d), BF16)],
        compiler_params=_params("arbitrary"), name="moe_combine")(dest, x, route, gains, ybuf)


def moe_layer(x1, ffn_norm, ple_norm, layer, w_router, e_w1, e_w3, e_w2, j):
    m, d = x1.shape
    wr_pad = jnp.zeros((d, LANES), F32).at[:, :N_EXPERTS].set(w_router)
    h, route = norm_router(x1, ffn_norm, layer, wr_pad)
    e_flat = route[:, :TOP_K].astype(jnp.int32).reshape(-1)
    onehot = (e_flat[:, None] == jnp.arange(N_EXPERTS)[None, :]).astype(jnp.int32)
    csum = jnp.cumsum(onehot, axis=0)
    rank = jnp.sum((csum - onehot) * onehot, axis=1)
    counts = csum[-1]
    n_units_e = (counts + FFN_TM - 1) // FFN_TM
    unit_end = jnp.cumsum(n_units_e)
    unit_start = unit_end - n_units_e
    dest = (unit_start[e_flat] * FFN_TM + rank).astype(jnp.int32)
    n_units = (m * TOP_K) // FFN_TM + N_EXPERTS
    uidx = jnp.arange(n_units)
    ue = jnp.minimum(jnp.searchsorted(unit_end, uidx, side='right'), N_EXPERTS - 1).astype(jnp.int32)
    live_rows = jnp.clip(counts[ue] - (uidx - unit_start[ue]) * FFN_TM, 0, FFN_TM)
    live_rows = jnp.where(uidx < unit_end[-1], live_rows, 0)
    un = ((live_rows + FFN_SUB - 1) // FFN_SUB).astype(jnp.int32)
    last_live = jnp.maximum(unit_end[-1] - 1, 0)
    ue = jnp.where(uidx < unit_end[-1], ue, ue[last_live]).astype(jnp.int32)

    xbuf = moe_gather(h, dest, n_units * FFN_TM)
    ybuf = swiglu_ffn(xbuf, ue, un, e_w1, e_w3, e_w2, (j,))
    return moe_combine(x1, route, dest, ybuf, ple_norm, layer)


def kernel(x, p, mix_norm, w_in, q_norm, k_norm, ssm_lambda_re, ssm_lambda_im, ssm_log_dt, ssm_b_re, ssm_b_im,
           ssm_c_re, ssm_c_im, ssm_d, ssm_glu_w, gmlp_v_norm, gmlp_ws, gmlp_b, w_branch, w_out, ffn_norm,
           dense_w1, dense_w3, dense_w2, router_w, expert_w1, expert_w3, expert_w2, ple_norm, ple_gate_w,
           ple_proj_w):
    batch, seq, d = x.shape
    depth = w_in.shape[0]
    n_in = w_in.shape[-1]
    m = batch * seq
    xs = x.reshape(m, d)
    rope_c, rope_s = rope_tables(seq)

    def g3(a):
        return a.reshape(a.shape[0], 1, a.shape[1])

    mix_norm, q_norm, k_norm, ssm_d, gmlp_v_norm, ffn_norm, ple_norm = map(
        g3, (mix_norm, q_norm, k_norm, ssm_d, gmlp_v_norm, ffn_norm, ple_norm))
    b_bcast = jnp.broadcast_to(gmlp_b[..., None], gmlp_b.shape + (MLP_GROUP_W,))
    p2 = p.reshape(depth, m, p.shape[-1])
    dense_units = m // FFN_TM
    dense_un = jnp.full((dense_units,), FFN_TM // FFN_SUB, jnp.int32)

    for i in range(depth):
        _, h = add_norm(xs, None, mix_norm, i)
        proj = fused_mm(m, n_in, [(h, 0, d, w_in, (i,), 0)], [], lambda dts, ex: dts[0], BF16,
                        tm=1024, tn=512, name="in_proj")
        attn = attention(proj, q_norm, k_norm, i, rope_c, rope_s, batch, seq)
        tabs = _ssm_tables(ssm_lambda_re[i], ssm_lambda_im[i], ssm_log_dt[i], ssm_b_re[i], ssm_b_im[i],
                           ssm_c_re[i], ssm_c_im[i], seq // SSM_SEGS)
        ssm = ssm_branch(proj, i, tabs, ssm_d, ssm_glu_w, batch, seq)
        mlp = gmlp_branch(proj, i, gmlp_v_norm, gmlp_ws, b_bcast, m)

        tn = 512
        merged = fused_mm(
            m, d,
            [(br, 0, br.shape[1], w_branch, (i, n), 0) for n, br in enumerate((attn, ssm, mlp))],
            [(proj, (GATE_OFF + n * d) // tn) for n in range(N_BRANCH)],
            lambda dts, ex: sum(_sigmoid(e.astype(F32)) * dt for e, dt in zip(ex, dts)),
            BF16, tm=1024, tn=tn, name="branch_merge")
        x1 = fused_mm(m, d, [(merged, 0, d, w_out, (i,), 0)], [(xs, 0)],
                      lambda dts, ex: ex[0] + dts[0], F32, tm=1024, tn=512, name="out_proj")

        j = i // 2
        if i % 2 == 0:
            _, h2 = add_norm(x1, None, ffn_norm, i)
            dense_ue = jnp.full((dense_units,), j, jnp.int32)
            y = swiglu_ffn(h2, dense_ue, dense_un, dense_w1, dense_w3, dense_w2, ())
            x2, hn = add_norm(x1, y, ple_norm, i)
        else:
            x2, hn = moe_layer(x1, ffn_norm, ple_norm, i, router_w[j], expert_w1, expert_w3, expert_w2, j)

        xs = fused_mm(
            m, d,
            [(hn, 0, d, ple_gate_w, (i,), 0), (p2[i], 0, p.shape[-1], ple_proj_w, (i,), 0)],
            [(x2, 0)],
            lambda dts, ex: ex[0] + _sigmoid(dts[0]) * dts[1], F32, tm=1024, tn=512, name="ple")
    return xs.reshape(batch, seq, d)
```

```python
import functools
import math

import jax
import jax.numpy as jnp
import numpy as np
from jax import lax
from jax.experimental import pallas as pl
from jax.experimental.pallas import tpu as pltpu

GRID_W = 64
ROPE_THETA = 10000.0
HEAD_DIM = 128
N_Q_HEADS = 8
N_KV_HEADS = 2
ATTN_W = N_Q_HEADS * HEAD_DIM
KV_W = N_KV_HEADS * HEAD_DIM
SSM_GROUP = 16
SSM_GROUPS = 64
SSM_W = SSM_GROUP * SSM_GROUPS
SSM_STATE = 64
MLP_CHUNK = 128
MLP_GROUPS = 8
MLP_GROUP_W = 128
MLP_W = MLP_GROUPS * MLP_GROUP_W
N_BRANCH = 3
N_EXPERTS = 8
TOP_K = 2
EPS = 1e-6

Q_OFF = 0
K_OFF = ATTN_W
V_OFF = K_OFF + KV_W
U_OFF = V_OFF + KV_W
ZU_OFF = U_OFF + SSM_W
ZV_OFF = ZU_OFF + MLP_W
GATE_OFF = ZV_OFF + MLP_W

V7X_VMEM_LIMIT_BYTES = 60 * 1024 * 1024
LANES = 128

BF16 = jnp.bfloat16
F32 = jnp.float32


def _params(*sem):
    return pltpu.CompilerParams(dimension_semantics=sem, vmem_limit_bytes=V7X_VMEM_LIMIT_BYTES)


def _gelu(x):
    c = math.sqrt(2.0 / math.pi)
    return 0.5 * x * (1.0 + jnp.tanh(c * (x + 0.044715 * (x * x * x))))


def _sigmoid(x):
    return 1.0 / (1.0 + jnp.exp(-x))


def _silu(x):
    return x * _sigmoid(x)


def _add_norm_kernel(*refs, has_delta):
    if has_delta:
        x_ref, d_ref, g_ref, xo_ref, h_ref = refs
        x = x_ref[...] + d_ref[...]
        xo_ref[...] = x
    else:
        x_ref, g_ref, h_ref = refs
        x = x_ref[...]
    y = x * lax.rsqrt(jnp.mean(x * x, axis=-1, keepdims=True) + EPS)
    h_ref[...] = (y * g_ref[...]).astype(h_ref.dtype)


def add_norm(x, delta, gains, layer, tm=512):
    m, d = x.shape
    row = pl.BlockSpec((tm, d), lambda i: (i, 0))
    gspec = pl.BlockSpec((None, 1, d), lambda i: (layer, 0, 0))
    if delta is None:
        h = pl.pallas_call(
            functools.partial(_add_norm_kernel, has_delta=False),
            grid=(m // tm,), in_specs=[row, gspec], out_specs=row,
            out_shape=jax.ShapeDtypeStruct((m, d), BF16),
            compiler_params=_params("parallel"), name="norm")(x, gains)
        return x, h
    xo, h = pl.pallas_call(
        functools.partial(_add_norm_kernel, has_delta=True),
        grid=(m // tm,), in_specs=[row, row, gspec], out_specs=[row, row],
        out_shape=[jax.ShapeDtypeStruct((m, d), F32), jax.ShapeDtypeStruct((m, d), BF16)],
        compiler_params=_params("parallel"), name="add_norm")(x, delta, gains)
    return xo, h


def _fused_mm_kernel(*refs, n_dots, n_extras, epilogue):
    a_refs = refs[:n_dots]
    w_refs = refs[n_dots:2 * n_dots]
    e_refs = refs[2 * n_dots:2 * n_dots + n_extras]
    o_ref = refs[2 * n_dots + n_extras]
    wb_refs = refs[2 * n_dots + n_extras + 1:]

    @pl.when(pl.program_id(1) == 0)
    def _():
        for w_ref, wb_ref in zip(w_refs, wb_refs):
            wb_ref[...] = w_ref[...].astype(BF16)

    dots = []
    for a_ref, wb_ref in zip(a_refs, wb_refs):
        a = a_ref[...]
        if a.dtype != BF16:
            a = a.astype(BF16)
        dots.append(jnp.dot(a, wb_ref[...], preferred_element_type=F32))
    extras = [e_ref[...] for e_ref in e_refs]
    o_ref[...] = epilogue(dots, extras).astype(o_ref.dtype)


def fused_mm(m, n, dots, extras, epilogue, out_dtype, tm, tn, name):
    in_specs, args, scratch = [], [], []
    for a, acb, k, _, _, _ in dots:
        in_specs.append(pl.BlockSpec((tm, k), lambda j, i, acb=acb: (i, acb)))
        args.append(a)
    for _, _, k, w, lead, off in dots:
        in_specs.append(pl.BlockSpec((None,) * len(lead) + (k, tn),
                                     lambda j, i, lead=tuple(lead), off=off: lead + (0, off + j)))
        args.append(w)
        scratch.append(pltpu.VMEM((k, tn), BF16))
    for e, off in extras:
        in_specs.append(pl.BlockSpec((tm, tn), lambda j, i, off=off: (i, off + j)))
        args.append(e)
    return pl.pallas_call(
        functools.partial(_fused_mm_kernel, n_dots=len(dots), n_extras=len(extras), epilogue=epilogue),
        grid=(n // tn, m // tm), in_specs=in_specs,
        out_specs=pl.BlockSpec((tm, tn), lambda j, i: (i, j)),
        out_shape=jax.ShapeDtypeStruct((m, n), out_dtype),
        scratch_shapes=scratch,
        compiler_params=_params("arbitrary", "arbitrary"), name=name)(*args)


def _rope(x, c, s):
    lane = lax.broadcasted_iota(jnp.int32, x.shape, x.ndim - 1)
    quarter = HEAD_DIM // 4
    partner = jnp.where((lane % (2 * quarter)) < quarter,
                        pltpu.roll(x, HEAD_DIM - quarter, x.ndim - 1),
                        pltpu.roll(x, quarter, x.ndim - 1))
    return x * c + partner * s


def _head_norm(x, g):
    return x * lax.rsqrt(jnp.mean(x * x, axis=-1, keepdims=True) + EPS) * g


def _attn_kernel(q_ref, k_ref, v_ref, cq_ref, sq_ref, ck_ref, sk_ref, qg_ref, kg_ref, o_ref, ks_ref, vs_ref, *,
                 rep):
    @pl.when(pl.program_id(2) == 0)
    def _():
        k = _head_norm(k_ref[...].astype(F32), kg_ref[...])
        ks_ref[...] = _rope(k, ck_ref[...], sk_ref[...]).astype(BF16)
        vs_ref[:, 0:HEAD_DIM] = v_ref[...]
        vs_ref[:, HEAD_DIM:2 * HEAD_DIM] = jnp.ones((v_ref.shape[0], HEAD_DIM), BF16)

    scale = HEAD_DIM ** -0.5 * math.log2(math.e)
    cq = cq_ref[...]
    sq = sq_ref[...]
    for hh in range(rep):
        sl = slice(hh * HEAD_DIM, (hh + 1) * HEAD_DIM)
        q = _head_norm(q_ref[:, sl].astype(F32), qg_ref[...])
        q = (_rope(q, cq, sq) * scale).astype(BF16)
        s = lax.dot_general(q, ks_ref[...], (((1,), (1,)), ((), ())), preferred_element_type=F32)
        m = jnp.max(s, axis=-1, keepdims=True)
        p = jnp.exp2(s - m).astype(BF16)
        o = jnp.dot(p, vs_ref[...], preferred_element_type=F32)
        o_ref[:, sl] = (o[:, 0:HEAD_DIM] / o[:, HEAD_DIM:HEAD_DIM + 1]).astype(o_ref.dtype)


def attention(proj, q_gain, k_gain, layer, rope_c, rope_s, batch, seq, tq=256):
    rep = N_Q_HEADS // N_KV_HEADS
    qw = rep * HEAD_DIM
    nq = seq // tq
    gspec = pl.BlockSpec((None, 1, HEAD_DIM), lambda b, g, i: (layer, 0, 0))
    return pl.pallas_call(
        functools.partial(_attn_kernel, rep=rep),
        grid=(batch, N_KV_HEADS, nq),
        in_specs=[
            pl.BlockSpec((tq, qw), lambda b, g, i: (b * nq + i, Q_OFF // qw + g)),
            pl.BlockSpec((seq, HEAD_DIM), lambda b, g, i: (b, K_OFF // HEAD_DIM + g)),
            pl.BlockSpec((seq, HEAD_DIM), lambda b, g, i: (b, V_OFF // HEAD_DIM + g)),
            pl.BlockSpec((tq, HEAD_DIM), lambda b, g, i: (i, 0)),
            pl.BlockSpec((tq, HEAD_DIM), lambda b, g, i: (i, 0)),
            pl.BlockSpec((seq, HEAD_DIM), lambda b, g, i: (0, 0)),
            pl.BlockSpec((seq, HEAD_DIM), lambda b, g, i: (0, 0)),
            gspec, gspec,
        ],
        out_specs=pl.BlockSpec((tq, qw), lambda b, g, i: (b * nq + i, g)),
        out_shape=jax.ShapeDtypeStruct((batch * seq, ATTN_W), BF16),
        scratch_shapes=[pltpu.VMEM((seq, HEAD_DIM), BF16), pltpu.VMEM((seq, 2 * HEAD_DIM), BF16)],
        compiler_params=_params("arbitrary", "arbitrary", "arbitrary"), name="attention",
    )(proj, proj, proj, rope_c, rope_s, rope_c, rope_s, q_gain, k_gain)


def rope_tables(seq):
    rows = seq // GRID_W
    t = jnp.arange(seq)
    pos = jnp.stack([t // GRID_W - rows // 2, t % GRID_W - GRID_W // 2], axis=-1).astype(F32)
    n_freq = HEAD_DIM // 4
    inv_freq = ROPE_THETA ** (-jnp.arange(n_freq, dtype=F32) / n_freq)
    ang = pos[:, :, None] * inv_freq
    cos, sin = jnp.cos(ang), jnp.sin(ang)
    c = jnp.concatenate([cos[:, 0], cos[:, 0], cos[:, 1], cos[:, 1]], axis=-1)
    s = jnp.concatenate([-sin[:, 0], sin[:, 0], -sin[:, 1], sin[:, 1]], axis=-1)
    return c, s


SSM_SEGS = 8
SSM_GB = 8
SSM_CB = SSM_GB * SSM_STATE
SSM_UB = SSM_GB * SSM_GROUP
SSM_TB = 64


def _ssm_kernel(*refs, nseg_rows, reverse, final_pass, n_seq):
    if final_pass:
        u_ref, bm_ref, ar_ref, ai_ref, init_ref, cm_ref, y_ref, s_ref, st_ref = refs
    else:
        u_ref, bm_ref, ar_ref, ai_ref, apr_ref, api_ref, init_out_ref, s_ref, st_ref = refs
    tb = pl.program_id(1)
    ntb = pl.num_programs(1)
    ns = nseg_rows
    cb = SSM_CB

    @pl.when(tb == 0)
    def _():
        if final_pass:
            st_ref[...] = init_ref[...]
        else:
            st_ref[...] = jnp.zeros_like(st_ref)

    s_ref[...] = jnp.dot(u_ref[...], bm_ref[...], preferred_element_type=F32)
    arb = jnp.broadcast_to(ar_ref[...], (ns, cb))
    aib = jnp.broadcast_to(ai_ref[...], (ns, cb))

    def step(i, carry):
        sr, si = carry
        t = (SSM_TB - 1 - i) if reverse else i
        r0 = pl.multiple_of(t * ns, ns)
        nr = arb * sr - aib * si + s_ref[pl.ds(r0, ns), 0:cb]
        ni = arb * si + aib * sr + s_ref[pl.ds(r0, ns), cb:2 * cb]
        if final_pass:
            s_ref[pl.ds(r0, ns), 0:cb] = nr
            s_ref[pl.ds(r0, ns), cb:2 * cb] = ni
        return nr, ni

    sr, si = lax.fori_loop(0, SSM_TB, step, (st_ref[:, 0:cb], st_ref[:, cb:2 * cb]), unroll=2)
    st_ref[:, 0:cb] = sr
    st_ref[:, cb:2 * cb] = si

    if final_pass:
        y_ref[...] = jnp.dot(s_ref[...].astype(BF16), cm_ref[...], preferred_element_type=F32)
    else:
        @pl.when(tb == ntb - 1)
        def _():
            apr = apr_ref[...]
            api = api_ref[...]
            zero = jnp.zeros((1, cb), F32)
            for b in range(n_seq):
                order = range(SSM_SEGS - 1, -1, -1) if reverse else range(SSM_SEGS)
                cr, ci = zero, zero
                for q in order:
                    row = b * SSM_SEGS + q
                    init_out_ref[pl.ds(row, 1), 0:cb] = cr
                    init_out_ref[pl.ds(row, 1), cb:2 * cb] = ci
                    er = st_ref[pl.ds(row, 1), 0:cb]
                    ei = st_ref[pl.ds(row, 1), cb:2 * cb]
                    cr, ci = apr * cr - api * ci + er, apr * ci + api * cr + ei


def _ssm_pass(u_perm, bmat, a_re, a_im, d, *, n_seq, seg_len, reverse, final_pass, extra):
    ns = n_seq * SSM_SEGS
    ncb = SSM_GROUPS // SSM_GB
    ntb = seg_len // SSM_TB
    rows = SSM_TB * ns

    def tmap(tb):
        return (ntb - 1 - tb) if reverse else tb

    vec = pl.BlockSpec((None, None, 1, SSM_CB), lambda j, tb: (d, j, 0, 0))
    in_specs = [
        pl.BlockSpec((rows, SSM_UB), lambda j, tb: (tmap(tb), j)),
        pl.BlockSpec((None, None, SSM_UB, 2 * SSM_CB), lambda j, tb: (d, j, 0, 0)),
        vec, vec,
    ]
    st_spec = pl.BlockSpec((None, ns, 2 * SSM_CB), lambda j, tb: (j, 0, 0))
    scratch = [pltpu.VMEM((rows, 2 * SSM_CB), F32), pltpu.VMEM((ns, 2 * SSM_CB), F32)]
    kern = functools.partial(_ssm_kernel, nseg_rows=ns, reverse=reverse, final_pass=final_pass, n_seq=n_seq)
    if final_pass:
        init, cmat = extra
        in_specs += [st_spec, pl.BlockSpec((None, None, 2 * SSM_CB, SSM_UB), lambda j, tb: (d, j, 0, 0))]
        return pl.pallas_call(
            kern, grid=(ncb, ntb), in_specs=in_specs,
            out_specs=pl.BlockSpec((rows, SSM_UB), lambda j, tb: (tmap(tb), j)),
            out_shape=jax.ShapeDtypeStruct((seg_len * ns, SSM_W), F32),
            scratch_shapes=scratch,
            compiler_params=_params("arbitrary", "arbitrary"), name="ssm_scan",
        )(u_perm, bmat, a_re, a_im, init, cmat)
    ap_re, ap_im = extra
    in_specs += [vec, vec]
    return pl.pallas_call(
        kern, grid=(ncb, ntb), in_specs=in_specs, out_specs=st_spec,
        out_shape=jax.ShapeDtypeStruct((ncb, ns, 2 * SSM_CB), F32),
        scratch_shapes=scratch,
        compiler_params=_params("arbitrary", "arbitrary"), name="ssm_carry",
    )(u_perm, bmat, a_re, a_im, ap_re, ap_im)


def _ssm_tables(lam_re, lam_im, log_dt, b_re, b_im, c_re, c_im, seg_len):
    ncb = SSM_GROUPS // SSM_GB
    lam = lax.complex(lam_re.astype(F32), lam_im.astype(F32))
    dt = jnp.exp(log_dt.astype(F32))[..., None]
    lam_bar = jnp.exp(lam * dt)
    b_bar = ((lam_bar - 1.0) / lam)[..., None] * lax.complex(b_re.astype(F32), b_im.astype(F32))
    lam_pow = jnp.exp(lam * dt * seg_len)
    eye = jnp.eye(SSM_GB, dtype=F32)

    def bm(part):
        x = part.reshape(2, ncb, SSM_GB, SSM_STATE, SSM_GROUP)
        x = jnp.einsum('djgpc,gh->djgchp', x, eye)
        return x.reshape(2, ncb, SSM_UB, SSM_CB)

    def cm(part):
        x = part.reshape(2, ncb, SSM_GB, SSM_GROUP, SSM_STATE)
        x = jnp.einsum('djgcp,gh->djgphc', x, eye)
        return x.reshape(2, ncb, SSM_CB, SSM_UB)

    bmat = jnp.concatenate([bm(jnp.real(b_bar)), bm(jnp.imag(b_bar))], axis=-1).astype(BF16)
    cmat = jnp.concatenate([cm(c_re.astype(F32)), -cm(c_im.astype(F32))], axis=-2).astype(BF16)

    def vec(x):
        return x.reshape(2, ncb, 1, SSM_CB)

    return (bmat, cmat, vec(jnp.real(lam_bar)), vec(jnp.imag(lam_bar)),
            vec(jnp.real(lam_pow)), vec(jnp.imag(lam_pow)))


def _ssm_act_kernel(yf_ref, yb_ref, u_ref, d_ref, o32_ref, o16_ref):
    y = yf_ref[...] + yb_ref[...] + d_ref[...] * u_ref[...].astype(F32)
    y = _gelu(y)
    o32_ref[...] = y
    o16_ref[...] = y.astype(BF16)


def ssm_branch(proj, layer, tabs, d_skip, w_glu, batch, seq):
    bmat, cmat, a_re, a_im, ap_re, ap_im = tabs
    seg_len = seq // SSM_SEGS
    ns = batch * SSM_SEGS
    m = batch * seq
    u = proj[:, U_OFF:U_OFF + SSM_W].reshape(ns, seg_len, SSM_W)
    u_perm = jnp.transpose(u, (1, 0, 2)).reshape(m, SSM_W)
    ys = []
    for d, reverse in ((0, False), (1, True)):
        kw = dict(n_seq=batch, seg_len=seg_len, reverse=reverse)
        init = _ssm_pass(u_perm, bmat, a_re, a_im, d, final_pass=False, extra=(ap_re, ap_im), **kw)
        ys.append(_ssm_pass(u_perm, bmat, a_re, a_im, d, final_pass=True, extra=(init, cmat), **kw))
    tm = 512
    row = pl.BlockSpec((tm, SSM_W), lambda i: (i, 0))
    y32, y16 = pl.pallas_call(
        _ssm_act_kernel, grid=(m // tm,),
        in_specs=[row, row, row, pl.BlockSpec((None, 1, SSM_W), lambda i: (layer, 0, 0))],
        out_specs=[row, row],
        out_shape=[jax.ShapeDtypeStruct((m, SSM_W), F32), jax.ShapeDtypeStruct((m, SSM_W), BF16)],
        compiler_params=_params("parallel"), name="ssm_act")(ys[0], ys[1], u_perm, d_skip)
    out = fused_mm(m, SSM_W, [(y16, 0, SSM_W, w_glu, (layer,), 0)], [(y32, 0)],
                   lambda dts, ex: ex[0] * _sigmoid(dts[0]), BF16, tm=1024, tn=512, name="ssm_glu")
    out = out.reshape(seg_len, ns, SSM_W)
    return jnp.transpose(out, (1, 0, 2)).reshape(m, SSM_W)


GMLP_NC = 4


def _gmlp_kernel(zu0_ref, zu1_ref, zv0_ref, zv1_ref, g_ref, ws_ref, bb_ref, o_ref):
    half = MLP_W // 2
    zv = jnp.concatenate([zv0_ref[...], zv1_ref[...]], axis=-1).astype(F32)
    v = _gelu(zv)
    v = (v * lax.rsqrt(jnp.mean(v * v, axis=-1, keepdims=True) + EPS) * g_ref[...]).astype(BF16)
    for g in range(MLP_GROUPS):
        cs = slice(g * MLP_GROUP_W, (g + 1) * MLP_GROUP_W)
        vg = jnp.concatenate([v[n * MLP_CHUNK:(n + 1) * MLP_CHUNK, cs] for n in range(GMLP_NC)], axis=-1)
        s = jnp.dot(ws_ref[g].astype(BF16), vg, preferred_element_type=F32)
        zu_ref = zu0_ref if g * MLP_GROUP_W < half else zu1_ref
        us = slice((g * MLP_GROUP_W) % half, (g * MLP_GROUP_W) % half + MLP_GROUP_W)
        for n in range(GMLP_NC):
            rs = slice(n * MLP_CHUNK, (n + 1) * MLP_CHUNK)
            sn = s[:, n * MLP_GROUP_W:(n + 1) * MLP_GROUP_W] + bb_ref[g]
            o_ref[rs, cs] = (_gelu(zu_ref[rs, us].astype(F32)) * sn).astype(o_ref.dtype)


def gmlp_branch(proj, layer, v_gain, w_s, b_bcast, m):
    rows = GMLP_NC * MLP_CHUNK
    half = MLP_W // 2

    def zspec(off):
        return pl.BlockSpec((rows, half), lambda i, off=off: (i, off // half))

    return pl.pallas_call(
        _gmlp_kernel, grid=(m // rows,),
        in_specs=[zspec(ZU_OFF), zspec(ZU_OFF + half), zspec(ZV_OFF), zspec(ZV_OFF + half),
                  pl.BlockSpec((None, 1, MLP_W), lambda i: (layer, 0, 0)),
                  pl.BlockSpec((None, MLP_GROUPS, MLP_CHUNK, MLP_CHUNK), lambda i: (layer, 0, 0, 0)),
                  pl.BlockSpec((None, MLP_GROUPS, MLP_CHUNK, MLP_GROUP_W), lambda i: (layer, 0, 0, 0))],
        out_specs=pl.BlockSpec((rows, MLP_W), lambda i: (i, 0)),
        out_shape=jax.ShapeDtypeStruct((m, MLP_W), BF16),
        compiler_params=_params("parallel"), name="gmlp",
    )(proj, proj, proj, proj, v_gain, w_s, b_bcast)


FFN_TM = 1024
FFN_SUB = 256
FFN_TF = 256


def _ffn_kernel(ue_ref, un_ref, x_ref, w1_ref, w3_ref, w2_ref, o_ref, *, live_counts):
    u = pl.program_id(0)
    j = pl.program_id(1)
    nsub = un_ref[u]

    @pl.when(j == 0)
    def _():
        o_ref[...] = jnp.zeros_like(o_ref)

    for k in live_counts:
        @pl.when(nsub == k)
        def _(k=k):
            rows = k * FFN_SUB
            x = x_ref[0:rows, :]
            h1 = jnp.dot(x, w1_ref[...].astype(BF16), preferred_element_type=F32)
            h3 = jnp.dot(x, w3_ref[...].astype(BF16), preferred_element_type=F32)
            act = (_silu(h1) * h3).astype(BF16)
            o_ref[0:rows, :] += jnp.dot(act, w2_ref[...].astype(BF16), preferred_element_type=F32)


def swiglu_ffn(x, unit_expert, unit_nsub, w1, w3, w2, lead, live_counts):
    rows, d = x.shape
    f = w1.shape[-1]
    n_units = rows // FFN_TM
    nf = f // FFN_TF
    nl = len(lead)

    def wmap_up(u, j, ue, un):
        live = un[u] > 0
        return tuple(lead) + (ue[u], 0, jnp.where(live, j, nf - 1))

    def wmap_down(u, j, ue, un):
        live = un[u] > 0
        return tuple(lead) + (ue[u], jnp.where(live, j, nf - 1), 0)

    grid_spec = pltpu.PrefetchScalarGridSpec(
        num_scalar_prefetch=2, grid=(n_units, nf),
        in_specs=[pl.BlockSpec((FFN_TM, d), lambda u, j, ue, un: (u, 0)),
                  pl.BlockSpec((None,) * (nl + 1) + (d, FFN_TF), wmap_up),
                  pl.BlockSpec((None,) * (nl + 1) + (d, FFN_TF), wmap_up),
                  pl.BlockSpec((None,) * (nl + 1) + (FFN_TF, d), wmap_down)],
        out_specs=pl.BlockSpec((FFN_TM, d), lambda u, j, ue, un: (u, 0)))
    return pl.pallas_call(
        functools.partial(_ffn_kernel, live_counts=tuple(live_counts)), grid_spec=grid_spec,
        out_shape=jax.ShapeDtypeStruct((rows, d), F32),
        compiler_params=_params("arbitrary", "arbitrary"), name="swiglu_ffn",
    )(unit_expert, unit_nsub, x, w1, w3, w2)


def _router_kernel(x_ref, g_ref, wr_ref, h_ref, r_ref):
    x = x_ref[...]
    h = x * lax.rsqrt(jnp.mean(x * x, axis=-1, keepdims=True) + EPS) * g_ref[...]
    h_ref[...] = h
    logits = jnp.dot(h, wr_ref[...], preferred_element_type=F32, precision=lax.Precision.HIGHEST)
    lane = lax.broadcasted_iota(jnp.int32, logits.shape, 1)
    neg = jnp.float32(-jnp.inf)
    logits = jnp.where(lane < N_EXPERTS, logits, neg)
    m1 = jnp.max(logits, axis=-1, keepdims=True)
    i1 = jnp.min(jnp.where(logits == m1, lane, LANES), axis=-1, keepdims=True)
    rest = jnp.where(lane == i1, neg, logits)
    m2 = jnp.max(rest, axis=-1, keepdims=True)
    i2 = jnp.min(jnp.where(rest == m2, lane, LANES), axis=-1, keepdims=True)
    e = jnp.exp(m2 - m1)
    g1 = 1.0 / (1.0 + e)
    g2 = e / (1.0 + e)
    r_ref[...] = jnp.where(lane == 0, i1.astype(F32),
                           jnp.where(lane == 1, i2.astype(F32),
                                     jnp.where(lane == 2, g1, jnp.where(lane == 3, g2, 0.0))))


def norm_router(x, gains, layer, w_router_pad, tm=256):
    m, d = x.shape
    row = pl.BlockSpec((tm, d), lambda i: (i, 0))
    return pl.pallas_call(
        _router_kernel, grid=(m // tm,),
        in_specs=[row, pl.BlockSpec((None, 1, d), lambda i: (layer, 0, 0)),
                  pl.BlockSpec((d, LANES), lambda i: (0, 0))],
        out_specs=[row, pl.BlockSpec((tm, LANES), lambda i: (i, 0))],
        out_shape=[jax.ShapeDtypeStruct((m, d), F32), jax.ShapeDtypeStruct((m, LANES), F32)],
        compiler_params=_params("parallel"), name="norm_router")(x, gains, w_router_pad)


GATHER_TB = FFN_SUB


def _gather_kernel(src_ref, nv_ref, h_ref, o_ref, buf_ref, sem):
    i = pl.program_id(0)
    base = i * GATHER_TB
    nv = nv_ref[i]

    @pl.when(i == 0)
    def _():
        buf_ref[...] = jnp.zeros_like(buf_ref)

    def copy(t):
        return pltpu.make_async_copy(h_ref.at[pl.ds(src_ref[base + t], 1)], buf_ref.at[pl.ds(t, 1)], sem)

    def issue(t, c):
        copy(t).start()
        return c

    def drain(t, c):
        copy(t).wait()
        return c

    lax.fori_loop(0, nv, issue, 0)
    lax.fori_loop(0, nv, drain, 0)
    row = lax.broadcasted_iota(jnp.int32, buf_ref.shape, 0)
    o_ref[...] = jnp.where(row < nv, buf_ref[...], 0.0).astype(o_ref.dtype)


def moe_gather(h, src, n_valid, cap):
    _, d = h.shape
    grid_spec = pltpu.PrefetchScalarGridSpec(
        num_scalar_prefetch=2, grid=(cap // GATHER_TB,),
        in_specs=[pl.BlockSpec(memory_space=pl.ANY)],
        out_specs=pl.BlockSpec((GATHER_TB, d), lambda i, s, n: (i, 0)),
        scratch_shapes=[pltpu.VMEM((GATHER_TB, d), h.dtype), pltpu.SemaphoreType.DMA(())])
    return pl.pallas_call(
        _gather_kernel, grid_spec=grid_spec,
        out_shape=jax.ShapeDtypeStruct((cap, d), BF16),
        compiler_params=_params("arbitrary"), name="moe_gather")(src, n_valid, h)


COMBINE_TB = 128


def _combine_kernel(dest_ref, x_ref, r_ref, g_ref, y_ref, xo_ref, h_ref, buf_ref, sem):
    base = pl.program_id(0) * COMBINE_TB

    def copy(t, k):
        return pltpu.make_async_copy(y_ref.at[pl.ds(dest_ref[TOP_K * (base + t) + k], 1)],
                                     buf_ref.at[k, pl.ds(t, 1)], sem)

    def issue(t, c):
        for k in range(TOP_K):
            copy(t, k).start()
        return c

    def drain(t, c):
        for k in range(TOP_K):
            copy(t, k).wait()
        return c

    lax.fori_loop(0, COMBINE_TB, issue, 0)
    lax.fori_loop(0, COMBINE_TB, drain, 0)
    r = r_ref[...]
    g1 = r[:, 2:3]
    g2 = r[:, 3:4]
    x = x_ref[...] + (buf_ref[0] * g1 + buf_ref[1] * g2)
    xo_ref[...] = x
    y = x * lax.rsqrt(jnp.mean(x * x, axis=-1, keepdims=True) + EPS)
    h_ref[...] = (y * g_ref[...]).astype(h_ref.dtype)


def moe_combine(x, route, dest, ybuf, gains, layer):
    m, d = x.shape
    row = pl.BlockSpec((COMBINE_TB, d), lambda i, dr: (i, 0))
    grid_spec = pltpu.PrefetchScalarGridSpec(
        num_scalar_prefetch=1, grid=(m // COMBINE_TB,),
        in_specs=[row, pl.BlockSpec((COMBINE_TB, LANES), lambda i, dr: (i, 0)),
                  pl.BlockSpec((None, 1, d), lambda i, dr: (layer, 0, 0)),
                  pl.BlockSpec(memory_space=pl.ANY)],
        out_specs=[row, row],
        scratch_shapes=[pltpu.VMEM((TOP_K, COMBINE_TB, d), F32), pltpu.SemaphoreType.DMA(())])
    return pl.pallas_call(
        _combine_kernel, grid_spec=grid_spec,
        out_shape=[jax.ShapeDtypeStruct((m, d), F32), jax.ShapeDtypeStruct((m, d), BF16)],
        compiler_params=_params("arbitrary"), name="moe_combine")(dest, x, route, gains, ybuf)


def moe_layer(x1, ffn_norm, ple_norm, layer, w_router, e_w1, e_w3, e_w2, j):
    m, d = x1.shape
    wr_pad = jnp.zeros((d, LANES), F32).at[:, :N_EXPERTS].set(w_router)
    h, route = norm_router(x1, ffn_norm, layer, wr_pad)
    e_flat = route[:, :TOP_K].astype(jnp.int32).reshape(-1)
    onehot = (e_flat[:, None] == jnp.arange(N_EXPERTS)[None, :]).astype(jnp.int32)
    csum = jnp.cumsum(onehot, axis=0)
    rank = jnp.sum((csum - onehot) * onehot, axis=1)
    counts = csum[-1]
    n_units_e = (counts + FFN_TM - 1) // FFN_TM
    unit_end = jnp.cumsum(n_units_e)
    unit_start = unit_end - n_units_e
    dest = (unit_start[e_flat] * FFN_TM + rank).astype(jnp.int32)
    n_units = (m * TOP_K) // FFN_TM + N_EXPERTS
    uidx = jnp.arange(n_units)
    ue = jnp.minimum(jnp.searchsorted(unit_end, uidx, side='right'), N_EXPERTS - 1).astype(jnp.int32)
    live_rows = jnp.clip(counts[ue] - (uidx - unit_start[ue]) * FFN_TM, 0, FFN_TM)
    live_rows = jnp.where(uidx < unit_end[-1], live_rows, 0)
    un = ((live_rows + FFN_SUB - 1) // FFN_SUB).astype(jnp.int32)
    last_live = jnp.maximum(unit_end[-1] - 1, 0)
    ue = jnp.where(uidx < unit_end[-1], ue, ue[last_live]).astype(jnp.int32)

    cap = n_units * FFN_TM
    src = jnp.zeros((cap,), jnp.int32).at[dest].set(jnp.arange(m * TOP_K, dtype=jnp.int32) // TOP_K)
    blk = jnp.arange(cap // GATHER_TB)
    per_unit = FFN_TM // GATHER_TB
    n_valid = jnp.clip(live_rows[blk // per_unit] - (blk % per_unit) * GATHER_TB, 0, GATHER_TB).astype(jnp.int32)

    xbuf = moe_gather(h, src, n_valid, cap)
    ybuf = swiglu_ffn(xbuf, ue, un, e_w1, e_w3, e_w2, (j,), range(1, FFN_TM // FFN_SUB + 1))
    return moe_combine(x1, route, dest, ybuf, ple_norm, layer)


def kernel(x, p, mix_norm, w_in, q_norm, k_norm, ssm_lambda_re, ssm_lambda_im, ssm_log_dt, ssm_b_re, ssm_b_im,
           ssm_c_re, ssm_c_im, ssm_d, ssm_glu_w, gmlp_v_norm, gmlp_ws, gmlp_b, w_branch, w_out, ffn_norm,
           dense_w1, dense_w3, dense_w2, router_w, expert_w1, expert_w3, expert_w2, ple_norm, ple_gate_w,
           ple_proj_w):
    batch, seq, d = x.shape
    depth = w_in.shape[0]
    n_in = w_in.shape[-1]
    m = batch * seq
    xs = x.reshape(m, d)
    rope_c, rope_s = rope_tables(seq)

    def g3(a):
        return a.reshape(a.shape[0], 1, a.shape[1])

    mix_norm, q_norm, k_norm, ssm_d, gmlp_v_norm, ffn_norm, ple_norm = map(
        g3, (mix_norm, q_norm, k_norm, ssm_d, gmlp_v_norm, ffn_norm, ple_norm))
    b_bcast = jnp.broadcast_to(gmlp_b[..., None], gmlp_b.shape + (MLP_GROUP_W,))
    p2 = p.reshape(depth, m, p.shape[-1])
    dense_units = m // FFN_TM
    dense_un = jnp.full((dense_units,), FFN_TM // FFN_SUB, jnp.int32)

    for i in range(depth):
        _, h = add_norm(xs, None, mix_norm, i)
        proj = fused_mm(m, n_in, [(h, 0, d, w_in, (i,), 0)], [], lambda dts, ex: dts[0], BF16,
                        tm=1024, tn=512, name="in_proj")
        attn = attention(proj, q_norm, k_norm, i, rope_c, rope_s, batch, seq)
        tabs = _ssm_tables(ssm_lambda_re[i], ssm_lambda_im[i], ssm_log_dt[i], ssm_b_re[i], ssm_b_im[i],
                           ssm_c_re[i], ssm_c_im[i], seq // SSM_SEGS)
        ssm = ssm_branch(proj, i, tabs, ssm_d, ssm_glu_w, batch, seq)
        mlp = gmlp_branch(proj, i, gmlp_v_norm, gmlp_ws, b_bcast, m)

        tn = 512
        merged = fused_mm(
            m, d,
            [(br, 0, br.shape[1], w_branch, (i, n), 0) for n, br in enumerate((attn, ssm, mlp))],
            [(proj, (GATE_OFF + n * d) // tn) for n in range(N_BRANCH)],
            lambda dts, ex: sum(_sigmoid(e.astype(F32)) * dt for e, dt in zip(ex, dts)),
            BF16, tm=1024, tn=tn, name="branch_merge")
        x1 = fused_mm(m, d, [(merged, 0, d, w_out, (i,), 0)], [(xs, 0)],
                      lambda dts, ex: ex[0] + dts[0], F32, tm=1024, tn=512, name="out_proj")

        j = i // 2
        if i % 2 == 0:
            _, h2 = add_norm(x1, None, ffn_norm, i)
            dense_ue = jnp.full((dense_units,), j, jnp.int32)
            y = swiglu_ffn(h2, dense_ue, dense_un, dense_w1, dense_w3, dense_w2, (), (FFN_TM // FFN_SUB,))
            x2, hn = add_norm(x1, y, ple_norm, i)
        else:
            x2, hn = moe_layer(x1, ffn_norm, ple_norm, i, router_w[j], expert_w1, expert_w3, expert_w2, j)

        xs = fused_mm(
            m, d,
            [(hn, 0, d, ple_gate_w, (i,), 0), (p2[i], 0, p.shape[-1], ple_proj_w, (i,), 0)],
            [(x2, 0)],
            lambda dts, ex: ex[0] + _sigmoid(dts[0]) * dts[1], F32, tm=1024, tn=512, name="ple")
    return xs.reshape(batch, seq, d)
```

```python
import functools
import math

import jax
import jax.numpy as jnp
import numpy as np
from jax import lax
from jax.experimental import pallas as pl
from jax.experimental.pallas import tpu as pltpu

GRID_W = 64
ROPE_THETA = 10000.0
HEAD_DIM = 128
N_Q_HEADS = 8
N_KV_HEADS = 2
ATTN_W = N_Q_HEADS * HEAD_DIM
KV_W = N_KV_HEADS * HEAD_DIM
SSM_GROUP = 16
SSM_GROUPS = 64
SSM_W = SSM_GROUP * SSM_GROUPS
SSM_STATE = 64
MLP_CHUNK = 128
MLP_GROUPS = 8
MLP_GROUP_W = 128
MLP_W = MLP_GROUPS * MLP_GROUP_W
N_BRANCH = 3
N_EXPERTS = 8
TOP_K = 2
EPS = 1e-6

Q_OFF = 0
K_OFF = ATTN_W
V_OFF = K_OFF + KV_W
U_OFF = V_OFF + KV_W
ZU_OFF = U_OFF + SSM_W
ZV_OFF = ZU_OFF + MLP_W
GATE_OFF = ZV_OFF + MLP_W

V7X_VMEM_LIMIT_BYTES = 60 * 1024 * 1024
LANES = 128

BF16 = jnp.bfloat16
F32 = jnp.float32


def _params(*sem):
    return pltpu.CompilerParams(dimension_semantics=sem, vmem_limit_bytes=V7X_VMEM_LIMIT_BYTES)


def _gelu(x):
    c = math.sqrt(2.0 / math.pi)
    return 0.5 * x * (1.0 + jnp.tanh(c * (x + 0.044715 * (x * x * x))))


def _sigmoid(x):
    return 1.0 / (1.0 + jnp.exp(-x))


def _silu(x):
    return x * _sigmoid(x)


def _add_norm_kernel(*refs, has_delta):
    if has_delta:
        x_ref, d_ref, g_ref, xo_ref, h_ref = refs
        x = x_ref[...] + d_ref[...]
        xo_ref[...] = x
    else:
        x_ref, g_ref, h_ref = refs
        x = x_ref[...]
    y = x * lax.rsqrt(jnp.mean(x * x, axis=-1, keepdims=True) + EPS)
    h_ref[...] = (y * g_ref[...]).astype(h_ref.dtype)


def add_norm(x, delta, gains, layer, tm=512):
    m, d = x.shape
    row = pl.BlockSpec((tm, d), lambda i: (i, 0))
    gspec = pl.BlockSpec((None, 1, d), lambda i: (layer, 0, 0))
    if delta is None:
        h = pl.pallas_call(
            functools.partial(_add_norm_kernel, has_delta=False),
            grid=(m // tm,), in_specs=[row, gspec], out_specs=row,
            out_shape=jax.ShapeDtypeStruct((m, d), BF16),
            compiler_params=_params("parallel"), name="norm")(x, gains)
        return x, h
    xo, h = pl.pallas_call(
        functools.partial(_add_norm_kernel, has_delta=True),
        grid=(m // tm,), in_specs=[row, row, gspec], out_specs=[row, row],
        out_shape=[jax.ShapeDtypeStruct((m, d), F32), jax.ShapeDtypeStruct((m, d), BF16)],
        compiler_params=_params("parallel"), name="add_norm")(x, delta, gains)
    return xo, h


def _fused_mm_kernel(*refs, n_dots, n_extras, epilogue):
    a_refs = refs[:n_dots]
    w_refs = refs[n_dots:2 * n_dots]
    e_refs = refs[2 * n_dots:2 * n_dots + n_extras]
    o_ref = refs[2 * n_dots + n_extras]
    wb_refs = refs[2 * n_dots + n_extras + 1:]

    @pl.when(pl.program_id(1) == 0)
    def _():
        for w_ref, wb_ref in zip(w_refs, wb_refs):
            wb_ref[...] = w_ref[...].astype(BF16)

    dots = []
    for a_ref, wb_ref in zip(a_refs, wb_refs):
        a = a_ref[...]
        if a.dtype != BF16:
            a = a.astype(BF16)
        dots.append(jnp.dot(a, wb_ref[...], preferred_element_type=F32))
    extras = [e_ref[...] for e_ref in e_refs]
    o_ref[...] = epilogue(dots, extras).astype(o_ref.dtype)


def fused_mm(m, n, dots, extras, epilogue, out_dtype, tm, tn, name):
    in_specs, args, scratch = [], [], []
    for a, acb, k, _, _, _ in dots:
        in_specs.append(pl.BlockSpec((tm, k), lambda j, i, acb=acb: (i, acb)))
        args.append(a)
    for _, _, k, w, lead, off in dots:
        in_specs.append(pl.BlockSpec((None,) * len(lead) + (k, tn),
                                     lambda j, i, lead=tuple(lead), off=off: lead + (0, off + j)))
        args.append(w)
        scratch.append(pltpu.VMEM((k, tn), BF16))
    for e, off in extras:
        in_specs.append(pl.BlockSpec((tm, tn), lambda j, i, off=off: (i, off + j)))
        args.append(e)
    return pl.pallas_call(
        functools.partial(_fused_mm_kernel, n_dots=len(dots), n_extras=len(extras), epilogue=epilogue),
        grid=(n // tn, m // tm), in_specs=in_specs,
        out_specs=pl.BlockSpec((tm, tn), lambda j, i: (i, j)),
        out_shape=jax.ShapeDtypeStruct((m, n), out_dtype),
        scratch_shapes=scratch,
        compiler_params=_params("arbitrary", "arbitrary"), name=name)(*args)


def _rope(x, c, s):
    lane = lax.broadcasted_iota(jnp.int32, x.shape, x.ndim - 1)
    quarter = HEAD_DIM // 4
    partner = jnp.where((lane % (2 * quarter)) < quarter,
                        pltpu.roll(x, HEAD_DIM - quarter, x.ndim - 1),
                        pltpu.roll(x, quarter, x.ndim - 1))
    return x * c + partner * s


def _head_norm(x, g):
    return x * lax.rsqrt(jnp.mean(x * x, axis=-1, keepdims=True) + EPS) * g


def _attn_kernel(q_ref, k_ref, v_ref, cq_ref, sq_ref, ck_ref, sk_ref, qg_ref, kg_ref, o_ref, ks_ref, vs_ref, *,
                 rep):
    @pl.when(pl.program_id(2) == 0)
    def _():
        k = _head_norm(k_ref[...].astype(F32), kg_ref[...])
        ks_ref[...] = _rope(k, ck_ref[...], sk_ref[...]).astype(BF16)
        vs_ref[:, 0:HEAD_DIM] = v_ref[...]
        vs_ref[:, HEAD_DIM:2 * HEAD_DIM] = jnp.ones((v_ref.shape[0], HEAD_DIM), BF16)

    scale = HEAD_DIM ** -0.5 * math.log2(math.e)
    cq = cq_ref[...]
    sq = sq_ref[...]
    for hh in range(rep):
        sl = slice(hh * HEAD_DIM, (hh + 1) * HEAD_DIM)
        q = _head_norm(q_ref[:, sl].astype(F32), qg_ref[...])
        q = (_rope(q, cq, sq) * scale).astype(BF16)
        s = lax.dot_general(q, ks_ref[...], (((1,), (1,)), ((), ())), preferred_element_type=F32)
        m = jnp.max(s, axis=-1, keepdims=True)
        p = jnp.exp2(s - m).astype(BF16)
        o = jnp.dot(p, vs_ref[...], preferred_element_type=F32)
        o_ref[:, sl] = (o[:, 0:HEAD_DIM] / o[:, HEAD_DIM:HEAD_DIM + 1]).astype(o_ref.dtype)


def attention(proj, q_gain, k_gain, layer, rope_c, rope_s, batch, seq, tq=256):
    rep = N_Q_HEADS // N_KV_HEADS
    qw = rep * HEAD_DIM
    nq = seq // tq
    gspec = pl.BlockSpec((None, 1, HEAD_DIM), lambda b, g, i: (layer, 0, 0))
    return pl.pallas_call(
        functools.partial(_attn_kernel, rep=rep),
        grid=(batch, N_KV_HEADS, nq),
        in_specs=[
            pl.BlockSpec((tq, qw), lambda b, g, i: (b * nq + i, Q_OFF // qw + g)),
            pl.BlockSpec((seq, HEAD_DIM), lambda b, g, i: (b, K_OFF // HEAD_DIM + g)),
            pl.BlockSpec((seq, HEAD_DIM), lambda b, g, i: (b, V_OFF // HEAD_DIM + g)),
            pl.BlockSpec((tq, HEAD_DIM), lambda b, g, i: (i, 0)),
            pl.BlockSpec((tq, HEAD_DIM), lambda b, g, i: (i, 0)),
            pl.BlockSpec((seq, HEAD_DIM), lambda b, g, i: (0, 0)),
            pl.BlockSpec((seq, HEAD_DIM), lambda b, g, i: (0, 0)),
            gspec, gspec,
        ],
        out_specs=pl.BlockSpec((tq, qw), lambda b, g, i: (b * nq + i, g)),
        out_shape=jax.ShapeDtypeStruct((batch * seq, ATTN_W), BF16),
        scratch_shapes=[pltpu.VMEM((seq, HEAD_DIM), BF16), pltpu.VMEM((seq, 2 * HEAD_DIM), BF16)],
        compiler_params=_params("arbitrary", "arbitrary", "arbitrary"), name="attention",
    )(proj, proj, proj, rope_c, rope_s, rope_c, rope_s, q_gain, k_gain)


def rope_tables(seq):
    rows = seq // GRID_W
    t = jnp.arange(seq)
    pos = jnp.stack([t // GRID_W - rows // 2, t % GRID_W - GRID_W // 2], axis=-1).astype(F32)
    n_freq = HEAD_DIM // 4
    inv_freq = ROPE_THETA ** (-jnp.arange(n_freq, dtype=F32) / n_freq)
    ang = pos[:, :, None] * inv_freq
    cos, sin = jnp.cos(ang), jnp.sin(ang)
    c = jnp.concatenate([cos[:, 0], cos[:, 0], cos[:, 1], cos[:, 1]], axis=-1)
    s = jnp.concatenate([-sin[:, 0], sin[:, 0], -sin[:, 1], sin[:, 1]], axis=-1)
    return c, s


SSM_SEGS = 8
SSM_GB = 8
SSM_CB = SSM_GB * SSM_STATE
SSM_UB = SSM_GB * SSM_GROUP
SSM_TB = 256


def _ssm_kernel(*refs, nseg_rows, reverse, final_pass, n_seq, n_steps):
    if final_pass:
        u_ref, bm_ref, ar_ref, ai_ref, init_ref, cm_ref, y_ref, s_ref, st_ref, sb_ref = refs
    else:
        u_ref, bm_ref, ar_ref, ai_ref, apr_ref, api_ref, init_out_ref, s_ref, st_ref = refs
    tb = pl.program_id(1)
    ntb = pl.num_programs(1)
    ns = nseg_rows
    cb = SSM_CB

    @pl.when(tb == 0)
    def _():
        if final_pass:
            st_ref[...] = init_ref[...]
        else:
            st_ref[...] = jnp.zeros_like(st_ref)

    s_ref[...] = jnp.dot(u_ref[...], bm_ref[...], preferred_element_type=F32)
    arb = jnp.broadcast_to(ar_ref[...], (ns, cb))
    aib = jnp.broadcast_to(ai_ref[...], (ns, cb))

    def step(i, carry):
        sr, si = carry
        t = (n_steps - 1 - i) if reverse else i
        r0 = pl.multiple_of(t * ns, ns)
        nr = arb * sr - aib * si + s_ref[pl.ds(r0, ns), 0:cb]
        ni = arb * si + aib * sr + s_ref[pl.ds(r0, ns), cb:2 * cb]
        if final_pass:
            sb_ref[pl.ds(r0, ns), 0:cb] = nr.astype(BF16)
            sb_ref[pl.ds(r0, ns), cb:2 * cb] = ni.astype(BF16)
        return nr, ni

    sr, si = lax.fori_loop(0, n_steps, step, (st_ref[:, 0:cb], st_ref[:, cb:2 * cb]), unroll=2)
    st_ref[:, 0:cb] = sr
    st_ref[:, cb:2 * cb] = si

    if final_pass:
        y_ref[...] = jnp.dot(sb_ref[...], cm_ref[...], preferred_element_type=F32)
    else:
        @pl.when(tb == ntb - 1)
        def _():
            apr = apr_ref[...]
            api = api_ref[...]
            zero = jnp.zeros((1, cb), F32)
            for b in range(n_seq):
                order = range(SSM_SEGS - 1, -1, -1) if reverse else range(SSM_SEGS)
                cr, ci = zero, zero
                for q in order:
                    row = b * SSM_SEGS + q
                    init_out_ref[pl.ds(row, 1), 0:cb] = cr
                    init_out_ref[pl.ds(row, 1), cb:2 * cb] = ci
                    er = st_ref[pl.ds(row, 1), 0:cb]
                    ei = st_ref[pl.ds(row, 1), cb:2 * cb]
                    cr, ci = apr * cr - api * ci + er, apr * ci + api * cr + ei


def _ssm_pass(u_perm, bmat, a_re, a_im, d, *, n_seq, seg_len, reverse, final_pass, extra):
    ns = n_seq * SSM_SEGS
    ncb = SSM_GROUPS // SSM_GB
    n_steps = min(SSM_TB, seg_len)
    ntb = seg_len // n_steps
    rows = n_steps * ns

    def tmap(tb):
        return (ntb - 1 - tb) if reverse else tb

    vec = pl.BlockSpec((None, None, 1, SSM_CB), lambda j, tb: (d, j, 0, 0))
    in_specs = [
        pl.BlockSpec((rows, SSM_UB), lambda j, tb: (tmap(tb), j)),
        pl.BlockSpec((None, None, SSM_UB, 2 * SSM_CB), lambda j, tb: (d, j, 0, 0)),
        vec, vec,
    ]
    st_spec = pl.BlockSpec((None, ns, 2 * SSM_CB), lambda j, tb: (j, 0, 0))
    scratch = [pltpu.VMEM((rows, 2 * SSM_CB), F32), pltpu.VMEM((ns, 2 * SSM_CB), F32)]
    kern = functools.partial(_ssm_kernel, nseg_rows=ns, reverse=reverse, final_pass=final_pass, n_seq=n_seq,
                             n_steps=n_steps)
    if final_pass:
        init, cmat = extra
        scratch = scratch + [pltpu.VMEM((rows, 2 * SSM_CB), BF16)]
        in_specs += [st_spec, pl.BlockSpec((None, None, 2 * SSM_CB, SSM_UB), lambda j, tb: (d, j, 0, 0))]
        return pl.pallas_call(
            kern, grid=(ncb, ntb), in_specs=in_specs,
            out_specs=pl.BlockSpec((rows, SSM_UB), lambda j, tb: (tmap(tb), j)),
            out_shape=jax.ShapeDtypeStruct((seg_len * ns, SSM_W), F32),
            scratch_shapes=scratch,
            compiler_params=_params("arbitrary", "arbitrary"), name="ssm_scan",
        )(u_perm, bmat, a_re, a_im, init, cmat)
    ap_re, ap_im = extra
    in_specs += [vec, vec]
    return pl.pallas_call(
        kern, grid=(ncb, ntb), in_specs=in_specs, out_specs=st_spec,
        out_shape=jax.ShapeDtypeStruct((ncb, ns, 2 * SSM_CB), F32),
        scratch_shapes=scratch,
        compiler_params=_params("arbitrary", "arbitrary"), name="ssm_carry",
    )(u_perm, bmat, a_re, a_im, ap_re, ap_im)


def _ssm_tables(lam_re, lam_im, log_dt, b_re, b_im, c_re, c_im, seg_len):
    ncb = SSM_GROUPS // SSM_GB
    lam = lax.complex(lam_re.astype(F32), lam_im.astype(F32))
    dt = jnp.exp(log_dt.astype(F32))[..., None]
    lam_bar = jnp.exp(lam * dt)
    b_bar = ((lam_bar - 1.0) / lam)[..., None] * lax.complex(b_re.astype(F32), b_im.astype(F32))
    lam_pow = jnp.exp(lam * dt * seg_len)
    eye = jnp.eye(SSM_GB, dtype=F32)

    def bm(part):
        x = part.reshape(2, ncb, SSM_GB, SSM_STATE, SSM_GROUP)
        x = jnp.einsum('djgpc,gh->djgchp', x, eye)
        return x.reshape(2, ncb, SSM_UB, SSM_CB)

    def cm(part):
        x = part.reshape(2, ncb, SSM_GB, SSM_GROUP, SSM_STATE)
        x = jnp.einsum('djgcp,gh->djgphc', x, eye)
        return x.reshape(2, ncb, SSM_CB, SSM_UB)

    bmat = jnp.concatenate([bm(jnp.real(b_bar)), bm(jnp.imag(b_bar))], axis=-1).astype(BF16)
    cmat = jnp.concatenate([cm(c_re.astype(F32)), -cm(c_im.astype(F32))], axis=-2).astype(BF16)

    def vec(x):
        return x.reshape(2, ncb, 1, SSM_CB)

    return (bmat, cmat, vec(jnp.real(lam_bar)), vec(jnp.imag(lam_bar)),
            vec(jnp.real(lam_pow)), vec(jnp.imag(lam_pow)))


def _ssm_act_kernel(yf_ref, yb_ref, u_ref, d_ref, o32_ref, o16_ref):
    y = yf_ref[...] + yb_ref[...] + d_ref[...] * u_ref[...].astype(F32)
    y = _gelu(y)
    o32_ref[...] = y
    o16_ref[...] = y.astype(BF16)


def ssm_branch(proj, layer, tabs, d_skip, w_glu, batch, seq):
    bmat, cmat, a_re, a_im, ap_re, ap_im = tabs
    seg_len = seq // SSM_SEGS
    ns = batch * SSM_SEGS
    m = batch * seq
    u = proj[:, U_OFF:U_OFF + SSM_W].reshape(ns, seg_len, SSM_W)
    u_perm = jnp.transpose(u, (1, 0, 2)).reshape(m, SSM_W)
    ys = []
    for d, reverse in ((0, False), (1, True)):
        kw = dict(n_seq=batch, seg_len=seg_len, reverse=reverse)
        init = _ssm_pass(u_perm, bmat, a_re, a_im, d, final_pass=False, extra=(ap_re, ap_im), **kw)
        ys.append(_ssm_pass(u_perm, bmat, a_re, a_im, d, final_pass=True, extra=(init, cmat), **kw))
    tm = 512
    row = pl.BlockSpec((tm, SSM_W), lambda i: (i, 0))
    y32, y16 = pl.pallas_call(
        _ssm_act_kernel, grid=(m // tm,),
        in_specs=[row, row, row, pl.BlockSpec((None, 1, SSM_W), lambda i: (layer, 0, 0))],
        out_specs=[row, row],
        out_shape=[jax.ShapeDtypeStruct((m, SSM_W), F32), jax.ShapeDtypeStruct((m, SSM_W), BF16)],
        compiler_params=_params("parallel"), name="ssm_act")(ys[0], ys[1], u_perm, d_skip)
    out = fused_mm(m, SSM_W, [(y16, 0, SSM_W, w_glu, (layer,), 0)], [(y32, 0)],
                   lambda dts, ex: ex[0] * _sigmoid(dts[0]), BF16, tm=1024, tn=512, name="ssm_glu")
    out = out.reshape(seg_len, ns, SSM_W)
    return jnp.transpose(out, (1, 0, 2)).reshape(m, SSM_W)


GMLP_NC = 4


def _gmlp_kernel(zu0_ref, zu1_ref, zv0_ref, zv1_ref, g_ref, ws_ref, bb_ref, o_ref):
    half = MLP_W // 2
    zv = jnp.concatenate([zv0_ref[...], zv1_ref[...]], axis=-1).astype(F32)
    v = _gelu(zv)
    v = (v * lax.rsqrt(jnp.mean(v * v, axis=-1, keepdims=True) + EPS) * g_ref[...]).astype(BF16)
    for g in range(MLP_GROUPS):
        cs = slice(g * MLP_GROUP_W, (g + 1) * MLP_GROUP_W)
        vg = jnp.concatenate([v[n * MLP_CHUNK:(n + 1) * MLP_CHUNK, cs] for n in range(GMLP_NC)], axis=-1)
        s = jnp.dot(ws_ref[g].astype(BF16), vg, preferred_element_type=F32)
        zu_ref = zu0_ref if g * MLP_GROUP_W < half else zu1_ref
        us = slice((g * MLP_GROUP_W) % half, (g * MLP_GROUP_W) % half + MLP_GROUP_W)
        for n in range(GMLP_NC):
            rs = slice(n * MLP_CHUNK, (n + 1) * MLP_CHUNK)
            sn = s[:, n * MLP_GROUP_W:(n + 1) * MLP_GROUP_W] + bb_ref[g]
            o_ref[rs, cs] = (_gelu(zu_ref[rs, us].astype(F32)) * sn).astype(o_ref.dtype)


def gmlp_branch(proj, layer, v_gain, w_s, b_bcast, m):
    rows = GMLP_NC * MLP_CHUNK
    half = MLP_W // 2

    def zspec(off):
        return pl.BlockSpec((rows, half), lambda i, off=off: (i, off // half))

    return pl.pallas_call(
        _gmlp_kernel, grid=(m // rows,),
        in_specs=[zspec(ZU_OFF), zspec(ZU_OFF + half), zspec(ZV_OFF), zspec(ZV_OFF + half),
                  pl.BlockSpec((None, 1, MLP_W), lambda i: (layer, 0, 0)),
                  pl.BlockSpec((None, MLP_GROUPS, MLP_CHUNK, MLP_CHUNK), lambda i: (layer, 0, 0, 0)),
                  pl.BlockSpec((None, MLP_GROUPS, MLP_CHUNK, MLP_GROUP_W), lambda i: (layer, 0, 0, 0))],
        out_specs=pl.BlockSpec((rows, MLP_W), lambda i: (i, 0)),
        out_shape=jax.ShapeDtypeStruct((m, MLP_W), BF16),
        compiler_params=_params("parallel"), name="gmlp",
    )(proj, proj, proj, proj, v_gain, w_s, b_bcast)


DENSE_TM = 1024
DENSE_TF = 512
MOE_TM = 1280
MOE_TF = 256
FFN_SUB = 256


def _ffn_kernel(ue_ref, un_ref, ub_ref, x_ref, w1_ref, w3_ref, w2_ref, o_ref, *, live_counts):
    del ue_ref, ub_ref
    u = pl.program_id(0)
    j = pl.program_id(1)
    nsub = un_ref[u]

    @pl.when((j == 0) & (nsub > 0))
    def _():
        o_ref[...] = jnp.zeros_like(o_ref)

    for k in live_counts:
        @pl.when(nsub == k)
        def _(k=k):
            rows = k * FFN_SUB
            x = x_ref[0:rows, :]
            h1 = jnp.dot(x, w1_ref[...].astype(BF16), preferred_element_type=F32)
            h3 = jnp.dot(x, w3_ref[...].astype(BF16), preferred_element_type=F32)
            act = (_silu(h1) * h3).astype(BF16)
            o_ref[0:rows, :] += jnp.dot(act, w2_ref[...].astype(BF16), preferred_element_type=F32)


def swiglu_ffn(x, unit_expert, unit_nsub, unit_block, w1, w3, w2, lead, live_counts, tm, tf):
    rows, d = x.shape
    f = w1.shape[-1]
    n_units = rows // tm
    nf = f // tf
    nl = len(lead)

    def wmap_up(u, j, ue, un, ub):
        return tuple(lead) + (ue[u], 0, jnp.where(un[u] > 0, j, nf - 1))

    def wmap_down(u, j, ue, un, ub):
        return tuple(lead) + (ue[u], jnp.where(un[u] > 0, j, nf - 1), 0)

    def rmap(u, j, ue, un, ub):
        return (ub[u], 0)

    grid_spec = pltpu.PrefetchScalarGridSpec(
        num_scalar_prefetch=3, grid=(n_units, nf),
        in_specs=[pl.BlockSpec((tm, d), rmap),
                  pl.BlockSpec((None,) * (nl + 1) + (d, tf), wmap_up),
                  pl.BlockSpec((None,) * (nl + 1) + (d, tf), wmap_up),
                  pl.BlockSpec((None,) * (nl + 1) + (tf, d), wmap_down)],
        out_specs=pl.BlockSpec((tm, d), rmap))
    return pl.pallas_call(
        functools.partial(_ffn_kernel, live_counts=tuple(live_counts)), grid_spec=grid_spec,
        out_shape=jax.ShapeDtypeStruct((rows, d), F32),
        compiler_params=_params("arbitrary", "arbitrary"), name="swiglu_ffn",
    )(unit_expert, unit_nsub, unit_block, x, w1, w3, w2)


def _router_kernel(x_ref, g_ref, wr_ref, h_ref, r_ref):
    x = x_ref[...]
    h = x * lax.rsqrt(jnp.mean(x * x, axis=-1, keepdims=True) + EPS) * g_ref[...]
    h_ref[...] = h
    logits = jnp.dot(h, wr_ref[...], preferred_element_type=F32, precision=lax.Precision.HIGHEST)
    lane = lax.broadcasted_iota(jnp.int32, logits.shape, 1)
    neg = jnp.float32(-jnp.inf)
    logits = jnp.where(lane < N_EXPERTS, logits, neg)
    m1 = jnp.max(logits, axis=-1, keepdims=True)
    i1 = jnp.min(jnp.where(logits == m1, lane, LANES), axis=-1, keepdims=True)
    rest = jnp.where(lane == i1, neg, logits)
    m2 = jnp.max(rest, axis=-1, keepdims=True)
    i2 = jnp.min(jnp.where(rest == m2, lane, LANES), axis=-1, keepdims=True)
    e = jnp.exp(m2 - m1)
    g1 = 1.0 / (1.0 + e)
    g2 = e / (1.0 + e)
    r_ref[...] = jnp.where(lane == 0, i1.astype(F32),
                           jnp.where(lane == 1, i2.astype(F32),
                                     jnp.where(lane == 2, g1, jnp.where(lane == 3, g2, 0.0))))


def norm_router(x, gains, layer, w_router_pad, tm=256):
    m, d = x.shape
    row = pl.BlockSpec((tm, d), lambda i: (i, 0))
    return pl.pallas_call(
        _router_kernel, grid=(m // tm,),
        in_specs=[row, pl.BlockSpec((None, 1, d), lambda i: (layer, 0, 0)),
                  pl.BlockSpec((d, LANES), lambda i: (0, 0))],
        out_specs=[row, pl.BlockSpec((tm, LANES), lambda i: (i, 0))],
        out_shape=[jax.ShapeDtypeStruct((m, d), F32), jax.ShapeDtypeStruct((m, LANES), F32)],
        compiler_params=_params("parallel"), name="norm_router")(x, gains, w_router_pad)


GATHER_TB = FFN_SUB


def _gather_kernel(src_ref, nv_ref, h_ref, o_ref, buf_ref, sem):
    i = pl.program_id(0)
    nv = nv_ref[i]

    def copy(blk, t):
        slot = blk % 2
        return pltpu.make_async_copy(h_ref.at[pl.ds(src_ref[blk * GATHER_TB + t], 1)],
                                     buf_ref.at[slot, pl.ds(t, 1)], sem.at[slot])

    def issue_block(blk):
        def body(t, c):
            copy(blk, t).start()
            return c
        lax.fori_loop(0, nv_ref[blk], body, 0)

    @pl.when(i == 0)
    def _():
        buf_ref[...] = jnp.zeros_like(buf_ref)
        issue_block(0)

    @pl.when(i + 1 < pl.num_programs(0))
    def _():
        issue_block(i + 1)

    def drain(t, c):
        copy(i, t).wait()
        return c

    lax.fori_loop(0, nv, drain, 0)
    row = lax.broadcasted_iota(jnp.int32, o_ref.shape, 0)
    o_ref[...] = jnp.where(row < nv, buf_ref[i % 2], 0.0).astype(o_ref.dtype)


def moe_gather(h, src, n_valid, cap):
    _, d = h.shape
    grid_spec = pltpu.PrefetchScalarGridSpec(
        num_scalar_prefetch=2, grid=(cap // GATHER_TB,),
        in_specs=[pl.BlockSpec(memory_space=pl.ANY)],
        out_specs=pl.BlockSpec((GATHER_TB, d), lambda i, s, n: (i, 0)),
        scratch_shapes=[pltpu.VMEM((2, GATHER_TB, d), h.dtype), pltpu.SemaphoreType.DMA((2,))])
    return pl.pallas_call(
        _gather_kernel, grid_spec=grid_spec,
        out_shape=jax.ShapeDtypeStruct((cap, d), BF16),
        compiler_params=_params("arbitrary"), name="moe_gather")(src, n_valid, h)


COMBINE_TB = 128


def _combine_kernel(dest_ref, x_ref, r_ref, g_ref, y_ref, xo_ref, h_ref, buf_ref, sem):
    i = pl.program_id(0)

    def copy(blk, t, k):
        slot = blk % 2
        return pltpu.make_async_copy(y_ref.at[pl.ds(dest_ref[TOP_K * (blk * COMBINE_TB + t) + k], 1)],
                                     buf_ref.at[slot, k, pl.ds(t, 1)], sem.at[slot])

    def issue_block(blk):
        def body(t, c):
            for k in range(TOP_K):
                copy(blk, t, k).start()
            return c
        lax.fori_loop(0, COMBINE_TB, body, 0)

    @pl.when(i == 0)
    def _():
        issue_block(0)

    @pl.when(i + 1 < pl.num_programs(0))
    def _():
        issue_block(i + 1)

    def drain(t, c):
        for k in range(TOP_K):
            copy(i, t, k).wait()
        return c

    lax.fori_loop(0, COMBINE_TB, drain, 0)
    r = r_ref[...]
    g1 = r[:, 2:3]
    g2 = r[:, 3:4]
    slot = i % 2
    x = x_ref[...] + (buf_ref[slot, 0] * g1 + buf_ref[slot, 1] * g2)
    xo_ref[...] = x
    y = x * lax.rsqrt(jnp.mean(x * x, axis=-1, keepdims=True) + EPS)
    h_ref[...] = (y * g_ref[...]).astype(h_ref.dtype)


def moe_combine(x, route, dest, ybuf, gains, layer):
    m, d = x.shape
    row = pl.BlockSpec((COMBINE_TB, d), lambda i, dr: (i, 0))
    grid_spec = pltpu.PrefetchScalarGridSpec(
        num_scalar_prefetch=1, grid=(m // COMBINE_TB,),
        in_specs=[row, pl.BlockSpec((COMBINE_TB, LANES), lambda i, dr: (i, 0)),
                  pl.BlockSpec((None, 1, d), lambda i, dr: (layer, 0, 0)),
                  pl.BlockSpec(memory_space=pl.ANY)],
        out_specs=[row, row],
        scratch_shapes=[pltpu.VMEM((2, TOP_K, COMBINE_TB, d), F32), pltpu.SemaphoreType.DMA((2,))])
    return pl.pallas_call(
        _combine_kernel, grid_spec=grid_spec,
        out_shape=[jax.ShapeDtypeStruct((m, d), F32), jax.ShapeDtypeStruct((m, d), BF16)],
        compiler_params=_params("arbitrary"), name="moe_combine")(dest, x, route, gains, ybuf)


def moe_layer(x1, ffn_norm, ple_norm, layer, w_router, e_w1, e_w3, e_w2, j):
    m, d = x1.shape
    wr_pad = jnp.zeros((d, LANES), F32).at[:, :N_EXPERTS].set(w_router)
    h, route = norm_router(x1, ffn_norm, layer, wr_pad)
    e_flat = route[:, :TOP_K].astype(jnp.int32).reshape(-1)
    onehot = (e_flat[:, None] == jnp.arange(N_EXPERTS)[None, :]).astype(jnp.int32)
    csum = jnp.cumsum(onehot, axis=0)
    rank = jnp.sum((csum - onehot) * onehot, axis=1)
    counts = csum[-1]
    tm = MOE_TM
    n_units_e = (counts + tm - 1) // tm
    unit_end = jnp.cumsum(n_units_e)
    unit_start = unit_end - n_units_e
    dest = (unit_start[e_flat] * tm + rank).astype(jnp.int32)
    n_units = (m * TOP_K) // tm + N_EXPERTS
    uidx = jnp.arange(n_units)
    ue = jnp.minimum(jnp.searchsorted(unit_end, uidx, side='right'), N_EXPERTS - 1).astype(jnp.int32)
    live_rows = jnp.clip(counts[ue] - (uidx - unit_start[ue]) * tm, 0, tm)
    live_rows = jnp.where(uidx < unit_end[-1], live_rows, 0)
    un = ((live_rows + FFN_SUB - 1) // FFN_SUB).astype(jnp.int32)
    last_live = jnp.maximum(unit_end[-1] - 1, 0)
    ue = jnp.where(uidx < unit_end[-1], ue, ue[last_live]).astype(jnp.int32)
    ub = jnp.where(uidx < unit_end[-1], uidx, last_live).astype(jnp.int32)

    cap = n_units * tm
    src = jnp.zeros((cap,), jnp.int32).at[dest].set(jnp.arange(m * TOP_K, dtype=jnp.int32) // TOP_K)
    blk = jnp.arange(cap // GATHER_TB)
    per_unit = tm // GATHER_TB
    n_valid = jnp.clip(live_rows[blk // per_unit] - (blk % per_unit) * GATHER_TB, 0, GATHER_TB).astype(jnp.int32)

    xbuf = moe_gather(h, src, n_valid, cap)
    ybuf = swiglu_ffn(xbuf, ue, un, ub, e_w1, e_w3, e_w2, (j,), range(1, tm // FFN_SUB + 1), tm, MOE_TF)
    return moe_combine(x1, route, dest, ybuf, ple_norm, layer)


def kernel(x, p, mix_norm, w_in, q_norm, k_norm, ssm_lambda_re, ssm_lambda_im, ssm_log_dt, ssm_b_re, ssm_b_im,
           ssm_c_re, ssm_c_im, ssm_d, ssm_glu_w, gmlp_v_norm, gmlp_ws, gmlp_b, w_branch, w_out, ffn_norm,
           dense_w1, dense_w3, dense_w2, router_w, expert_w1, expert_w3, expert_w2, ple_norm, ple_gate_w,
           ple_proj_w):
    batch, seq, d = x.shape
    depth = w_in.shape[0]
    n_in = w_in.shape[-1]
    m = batch * seq
    xs = x.reshape(m, d)
    rope_c, rope_s = rope_tables(seq)

    def g3(a):
        return a.reshape(a.shape[0], 1, a.shape[1])

    mix_norm, q_norm, k_norm, ssm_d, gmlp_v_norm, ffn_norm, ple_norm = map(
        g3, (mix_norm, q_norm, k_norm, ssm_d, gmlp_v_norm, ffn_norm, ple_norm))
    b_bcast = jnp.broadcast_to(gmlp_b[..., None], gmlp_b.shape + (MLP_GROUP_W,))
    p2 = p.reshape(depth, m, p.shape[-1])
    dense_units = m // DENSE_TM
    dense_un = jnp.full((dense_units,), DENSE_TM // FFN_SUB, jnp.int32)

    for i in range(depth):
        _, h = add_norm(xs, None, mix_norm, i)
        proj = fused_mm(m, n_in, [(h, 0, d, w_in, (i,), 0)], [], lambda dts, ex: dts[0], BF16,
                        tm=min(2048, m), tn=512, name="in_proj")
        attn = attention(proj, q_norm, k_norm, i, rope_c, rope_s, batch, seq)
        tabs = _ssm_tables(ssm_lambda_re[i], ssm_lambda_im[i], ssm_log_dt[i], ssm_b_re[i], ssm_b_im[i],
                           ssm_c_re[i], ssm_c_im[i], seq // SSM_SEGS)
        ssm = ssm_branch(proj, i, tabs, ssm_d, ssm_glu_w, batch, seq)
        mlp = gmlp_branch(proj, i, gmlp_v_norm, gmlp_ws, b_bcast, m)

        tn = 512
        merged = fused_mm(
            m, d,
            [(br, 0, br.shape[1], w_branch, (i, n), 0) for n, br in enumerate((attn, ssm, mlp))],
            [(proj, (GATE_OFF + n * d) // tn) for n in range(N_BRANCH)],
            lambda dts, ex: sum(_sigmoid(e.astype(F32)) * dt for e, dt in zip(ex, dts)),
            BF16, tm=1024, tn=tn, name="branch_merge")
        x1 = fused_mm(m, d, [(merged, 0, d, w_out, (i,), 0)], [(xs, 0)],
                      lambda dts, ex: ex[0] + dts[0], F32, tm=1024, tn=512, name="out_proj")

        j = i // 2
        if i % 2 == 0:
            _, h2 = add_norm(x1, None, ffn_norm, i)
            dense_ue = jnp.full((dense_units,), j, jnp.int32)
            y = swiglu_ffn(h2, dense_ue, dense_un, jnp.arange(dense_units, dtype=jnp.int32),
                           dense_w1, dense_w3, dense_w2, (), (DENSE_TM // FFN_SUB,), DENSE_TM, DENSE_TF)
            x2, hn = add_norm(x1, y, ple_norm, i)
        else:
            x2, hn = moe_layer(x1, ffn_norm, ple_norm, i, router_w[j], expert_w1, expert_w3, expert_w2, j)

        xs = fused_mm(
            m, d,
            [(hn, 0, d, ple_gate_w, (i,), 0), (p2[i], 0, p.shape[-1], ple_proj_w, (i,), 0)],
            [(x2, 0)],
            lambda dts, ex: ex[0] + _sigmoid(dts[0]) * dts[1], F32, tm=1024, tn=512, name="ple")
    return xs.reshape(batch, seq, d)
```

```python
import functools
import math

import jax
import jax.numpy as jnp
import numpy as np
from jax import lax
from jax.experimental import pallas as pl
from jax.experimental.pallas import tpu as pltpu

GRID_W = 64
ROPE_THETA = 10000.0
HEAD_DIM = 128
N_Q_HEADS = 8
N_KV_HEADS = 2
ATTN_W = N_Q_HEADS * HEAD_DIM
KV_W = N_KV_HEADS * HEAD_DIM
SSM_GROUP = 16
SSM_GROUPS = 64
SSM_W = SSM_GROUP * SSM_GROUPS
SSM_STATE = 64
MLP_CHUNK = 128
MLP_GROUPS = 8
MLP_GROUP_W = 128
MLP_W = MLP_GROUPS * MLP_GROUP_W
N_BRANCH = 3
N_EXPERTS = 8
TOP_K = 2
EPS = 1e-6

Q_OFF = 0
K_OFF = ATTN_W
V_OFF = K_OFF + KV_W
U_OFF = V_OFF + KV_W
ZU_OFF = U_OFF + SSM_W
ZV_OFF = ZU_OFF + MLP_W
GATE_OFF = ZV_OFF + MLP_W

V7X_VMEM_LIMIT_BYTES = 60 * 1024 * 1024
LANES = 128

BF16 = jnp.bfloat16
F32 = jnp.float32


def _params(*sem):
    return pltpu.CompilerParams(dimension_semantics=sem, vmem_limit_bytes=V7X_VMEM_LIMIT_BYTES)


def _gelu(x):
    c = math.sqrt(2.0 / math.pi)
    return 0.5 * x * (1.0 + jnp.tanh(c * (x + 0.044715 * (x * x * x))))


def _sigmoid(x):
    return 1.0 / (1.0 + jnp.exp(-x))


def _silu(x):
    return x * _sigmoid(x)


def _add_norm_kernel(*refs, has_delta):
    if has_delta:
        x_ref, d_ref, g_ref, xo_ref, h_ref = refs
        x = x_ref[...] + d_ref[...]
        xo_ref[...] = x
    else:
        x_ref, g_ref, h_ref = refs
        x = x_ref[...]
    y = x * lax.rsqrt(jnp.mean(x * x, axis=-1, keepdims=True) + EPS)
    h_ref[...] = (y * g_ref[...]).astype(h_ref.dtype)


def add_norm(x, delta, gains, layer, tm=512):
    m, d = x.shape
    row = pl.BlockSpec((tm, d), lambda i: (i, 0))
    gspec = pl.BlockSpec((None, 1, d), lambda i: (layer, 0, 0))
    if delta is None:
        h = pl.pallas_call(
            functools.partial(_add_norm_kernel, has_delta=False),
            grid=(m // tm,), in_specs=[row, gspec], out_specs=row,
            out_shape=jax.ShapeDtypeStruct((m, d), BF16),
            compiler_params=_params("parallel"), name="norm")(x, gains)
        return x, h
    xo, h = pl.pallas_call(
        functools.partial(_add_norm_kernel, has_delta=True),
        grid=(m // tm,), in_specs=[row, row, gspec], out_specs=[row, row],
        out_shape=[jax.ShapeDtypeStruct((m, d), F32), jax.ShapeDtypeStruct((m, d), BF16)],
        compiler_params=_params("parallel"), name="add_norm")(x, delta, gains)
    return xo, h


def _fused_mm_kernel(*refs, n_dots, n_extras, epilogue):
    a_refs = refs[:n_dots]
    w_refs = refs[n_dots:2 * n_dots]
    e_refs = refs[2 * n_dots:2 * n_dots + n_extras]
    o_ref = refs[2 * n_dots + n_extras]
    wb_refs = refs[2 * n_dots + n_extras + 1:]

    @pl.when(pl.program_id(1) == 0)
    def _():
        for w_ref, wb_ref in zip(w_refs, wb_refs):
            wb_ref[...] = w_ref[...].astype(BF16)

    dots = []
    for a_ref, wb_ref in zip(a_refs, wb_refs):
        a = a_ref[...]
        if a.dtype != BF16:
            a = a.astype(BF16)
        dots.append(jnp.dot(a, wb_ref[...], preferred_element_type=F32))
    extras = [e_ref[...] for e_ref in e_refs]
    o_ref[...] = epilogue(dots, extras).astype(o_ref.dtype)


def fused_mm(m, n, dots, extras, epilogue, out_dtype, tm, tn, name):
    in_specs, args, scratch = [], [], []
    for a, acb, k, _, _, _ in dots:
        in_specs.append(pl.BlockSpec((tm, k), lambda j, i, acb=acb: (i, acb)))
        args.append(a)
    for _, _, k, w, lead, off in dots:
        in_specs.append(pl.BlockSpec((None,) * len(lead) + (k, tn),
                                     lambda j, i, lead=tuple(lead), off=off: lead + (0, off + j)))
        args.append(w)
        scratch.append(pltpu.VMEM((k, tn), BF16))
    for e, off in extras:
        in_specs.append(pl.BlockSpec((tm, tn), lambda j, i, off=off: (i, off + j)))
        args.append(e)
    return pl.pallas_call(
        functools.partial(_fused_mm_kernel, n_dots=len(dots), n_extras=len(extras), epilogue=epilogue),
        grid=(n // tn, m // tm), in_specs=in_specs,
        out_specs=pl.BlockSpec((tm, tn), lambda j, i: (i, j)),
        out_shape=jax.ShapeDtypeStruct((m, n), out_dtype),
        scratch_shapes=scratch,
        compiler_params=_params("arbitrary", "arbitrary"), name=name)(*args)


def _rope(x, c, s):
    lane = lax.broadcasted_iota(jnp.int32, x.shape, x.ndim - 1)
    quarter = HEAD_DIM // 4
    partner = jnp.where((lane % (2 * quarter)) < quarter,
                        pltpu.roll(x, HEAD_DIM - quarter, x.ndim - 1),
                        pltpu.roll(x, quarter, x.ndim - 1))
    return x * c + partner * s


def _head_norm(x, g):
    return x * lax.rsqrt(jnp.mean(x * x, axis=-1, keepdims=True) + EPS) * g


def _attn_kernel(q_ref, k_ref, v_ref, cq_ref, sq_ref, ck_ref, sk_ref, qg_ref, kg_ref, o_ref, ks_ref, vs_ref, *,
                 rep):
    @pl.when(pl.program_id(2) == 0)
    def _():
        k = _head_norm(k_ref[...].astype(F32), kg_ref[...])
        ks_ref[...] = _rope(k, ck_ref[...], sk_ref[...]).astype(BF16)
        vs_ref[:, 0:HEAD_DIM] = v_ref[...]
        vs_ref[:, HEAD_DIM:2 * HEAD_DIM] = jnp.ones((v_ref.shape[0], HEAD_DIM), BF16)

    scale = HEAD_DIM ** -0.5 * math.log2(math.e)
    cq = cq_ref[...]
    sq = sq_ref[...]
    for hh in range(rep):
        sl = slice(hh * HEAD_DIM, (hh + 1) * HEAD_DIM)
        q = _head_norm(q_ref[:, sl].astype(F32), qg_ref[...])
        q = (_rope(q, cq, sq) * scale).astype(BF16)
        s = lax.dot_general(q, ks_ref[...], (((1,), (1,)), ((), ())), preferred_element_type=F32)
        m = jnp.max(s, axis=-1, keepdims=True)
        p = jnp.exp2(s - m).astype(BF16)
        o = jnp.dot(p, vs_ref[...], preferred_element_type=F32)
        o_ref[:, sl] = (o[:, 0:HEAD_DIM] / o[:, HEAD_DIM:HEAD_DIM + 1]).astype(o_ref.dtype)


def attention(proj, q_gain, k_gain, layer, rope_c, rope_s, batch, seq, tq=256):
    rep = N_Q_HEADS // N_KV_HEADS
    qw = rep * HEAD_DIM
    nq = seq // tq
    gspec = pl.BlockSpec((None, 1, HEAD_DIM), lambda b, g, i: (layer, 0, 0))
    return pl.pallas_call(
        functools.partial(_attn_kernel, rep=rep),
        grid=(batch, N_KV_HEADS, nq),
        in_specs=[
            pl.BlockSpec((tq, qw), lambda b, g, i: (b * nq + i, Q_OFF // qw + g)),
            pl.BlockSpec((seq, HEAD_DIM), lambda b, g, i: (b, K_OFF // HEAD_DIM + g)),
            pl.BlockSpec((seq, HEAD_DIM), lambda b, g, i: (b, V_OFF // HEAD_DIM + g)),
            pl.BlockSpec((tq, HEAD_DIM), lambda b, g, i: (i, 0)),
            pl.BlockSpec((tq, HEAD_DIM), lambda b, g, i: (i, 0)),
            pl.BlockSpec((seq, HEAD_DIM), lambda b, g, i: (0, 0)),
            pl.BlockSpec((seq, HEAD_DIM), lambda b, g, i: (0, 0)),
            gspec, gspec,
        ],
        out_specs=pl.BlockSpec((tq, qw), lambda b, g, i: (b * nq + i, g)),
        out_shape=jax.ShapeDtypeStruct((batch * seq, ATTN_W), BF16),
        scratch_shapes=[pltpu.VMEM((seq, HEAD_DIM), BF16), pltpu.VMEM((seq, 2 * HEAD_DIM), BF16)],
        compiler_params=_params("arbitrary", "arbitrary", "arbitrary"), name="attention",
    )(proj, proj, proj, rope_c, rope_s, rope_c, rope_s, q_gain, k_gain)


def rope_tables(seq):
    rows = seq // GRID_W
    t = jnp.arange(seq)
    pos = jnp.stack([t // GRID_W - rows // 2, t % GRID_W - GRID_W // 2], axis=-1).astype(F32)
    n_freq = HEAD_DIM // 4
    inv_freq = ROPE_THETA ** (-jnp.arange(n_freq, dtype=F32) / n_freq)
    ang = pos[:, :, None] * inv_freq
    cos, sin = jnp.cos(ang), jnp.sin(ang)
    c = jnp.concatenate([cos[:, 0], cos[:, 0], cos[:, 1], cos[:, 1]], axis=-1)
    s = jnp.concatenate([-sin[:, 0], sin[:, 0], -sin[:, 1], sin[:, 1]], axis=-1)
    return c, s


SSM_SEGS = 8
SSM_BLK = 16
SSM_PG = 2
SSM_PW = SSM_PG * SSM_STATE
SSM_ROW = SSM_BLK * SSM_PG * SSM_GROUP


def _ssm_kernel(u_ref, bw_ref, tp_ref, v_ref, a16_ref, at_ref, d_ref, o_ref, z_ref, s_ref, e_ref, *,
                n_seq, n_blocks):
    ns = n_seq * SSM_SEGS
    pw = SSM_PW
    u = u_ref[...]
    z_ref[...] = jnp.dot(u, bw_ref[...], preferred_element_type=F32)

    def part(k):
        return slice(k * pw, (k + 1) * pw)

    afr, afi, abr, abi = [jnp.broadcast_to(a16_ref[:, part(k)], (ns, pw)) for k in range(4)]

    def rows(blk):
        return slice(blk * ns, (blk + 1) * ns)

    def scan(state, keep):
        fr, fi, br, bi = state
        for i in range(n_blocks):
            mf, mb = i, n_blocks - 1 - i
            if keep:
                s_ref[rows(mf), part(0)] = fr.astype(BF16)
                s_ref[rows(mf), part(1)] = fi.astype(BF16)
                s_ref[rows(mb), part(2)] = br.astype(BF16)
                s_ref[rows(mb), part(3)] = bi.astype(BF16)
            zfr, zfi = z_ref[rows(mf), part(0)], z_ref[rows(mf), part(1)]
            zbr, zbi = z_ref[rows(mb), part(2)], z_ref[rows(mb), part(3)]
            fr, fi = afr * fr - afi * fi + zfr, afr * fi + afi * fr + zfi
            br, bi = abr * br - abi * bi + zbr, abr * bi + abi * br + zbi
        return fr, fi, br, bi

    zero = jnp.zeros((ns, pw), F32)
    ends = scan((zero, zero, zero, zero), keep=False)
    for k in range(4):
        e_ref[:, part(k)] = ends[k]
    row0 = jnp.zeros((1, pw), F32)
    for b in range(n_seq):
        for kr, ki, order in ((0, 1, range(SSM_SEGS)), (2, 3, range(SSM_SEGS - 1, -1, -1))):
            tr, ti = at_ref[:, part(kr)], at_ref[:, part(ki)]
            cr, ci = row0, row0
            for q in order:
                row = slice(b * SSM_SEGS + q, b * SSM_SEGS + q + 1)
                er, ei = e_ref[row, part(kr)], e_ref[row, part(ki)]
                e_ref[row, part(kr)] = cr
                e_ref[row, part(ki)] = ci
                cr, ci = tr * cr - ti * ci + er, tr * ci + ti * cr + ei
    scan(tuple(e_ref[:, part(k)] for k in range(4)), keep=True)

    y = jnp.dot(u, tp_ref[...], preferred_element_type=F32)
    y = y + jnp.dot(s_ref[...], v_ref[...], preferred_element_type=F32)
    y = y + d_ref[...] * u.astype(F32)
    o_ref[...] = _gelu(y).astype(o_ref.dtype)


def _ssm_tables(lam_re, lam_im, log_dt, b_re, b_im, c_re, c_im, d_skip, seg_len):
    g, p, c, r, pg = SSM_GROUPS, SSM_STATE, SSM_GROUP, SSM_BLK, SSM_PG
    npair = g // pg
    lam = lax.complex(lam_re.astype(F32), lam_im.astype(F32))
    dt = jnp.exp(log_dt.astype(F32))[..., None]
    lam_bar = jnp.exp(lam * dt)
    b_bar = ((lam_bar - 1.0) / lam)[..., None] * lax.complex(b_re.astype(F32), b_im.astype(F32))
    c_cplx = lax.complex(c_re.astype(F32), c_im.astype(F32))
    apow = jnp.exp((lam * dt)[..., None] * jnp.arange(r + 1, dtype=F32))
    rr = jnp.arange(r)
    eye = jnp.eye(pg, dtype=F32)

    wf = jnp.einsum('gpr,gpc->grcp', apow[0][:, :, r - 1 - rr], b_bar[0])
    wb = jnp.einsum('gpr,gpc->grcp', apow[1][:, :, rr], b_bar[1])
    w4 = jnp.stack([jnp.real(wf), jnp.imag(wf), jnp.real(wb), jnp.imag(wb)], axis=3)
    w4 = w4.reshape(npair, pg, r, c, 4, p)
    bw = jnp.einsum('aircqp,ij->aricqjp', w4, eye).reshape(npair, SSM_ROW, 4 * SSM_PW)

    vf = jnp.einsum('gcp,gpr->gprc', c_cplx[0], apow[0][:, :, 1 + rr])
    vb = jnp.einsum('gcp,gpr->gprc', c_cplx[1], apow[1][:, :, r - rr])
    v4 = jnp.stack([jnp.real(vf), -jnp.imag(vf), jnp.real(vb), -jnp.imag(vb)], axis=1)
    v4 = v4.reshape(npair, pg, 4, p, r, c)
    v = jnp.einsum('ajqprc,ji->aqjpric', v4, eye).reshape(npair, 4 * SSM_PW, SSM_ROW)

    kf = jnp.real(jnp.einsum('gcp,gpk,gpd->gkcd', c_cplx[0], apow[0][:, :, :r], b_bar[0]))
    kb = jnp.real(jnp.einsum('gcp,gpk,gpd->gkcd', c_cplx[1], apow[1][:, :, :r], b_bar[1]))
    lag = rr[None, :] - rr[:, None]
    tf = jnp.where((lag >= 0)[None, :, :, None, None], kf[:, jnp.clip(lag, 0, r - 1)], 0.0)
    tb = jnp.where((lag <= 0)[None, :, :, None, None], kb[:, jnp.clip(-lag, 0, r - 1)], 0.0)
    t = jnp.transpose(tf + tb, (0, 1, 4, 2, 3)).reshape(npair, pg, r, c, r, c)
    tp = jnp.einsum('airdsc,ij->aridsjc', t, eye).reshape(npair, SSM_ROW, SSM_ROW)

    def parts(x):
        y = jnp.stack([jnp.real(x[0]), jnp.imag(x[0]), jnp.real(x[1]), jnp.imag(x[1])], axis=0)
        return jnp.transpose(y.reshape(4, npair, SSM_PW), (1, 0, 2)).reshape(npair, 1, 4 * SSM_PW)

    a_blk = parts(apow[..., r])
    a_seg = parts(jnp.exp(lam * dt * seg_len))
    d_row = jnp.broadcast_to(d_skip.astype(F32).reshape(npair, 1, pg * c), (npair, r, pg * c))
    return bw.astype(BF16), tp.astype(BF16), v.astype(BF16), a_blk, a_seg, d_row.reshape(npair, 1, SSM_ROW)


def ssm_branch(proj, layer, tabs, w_glu, batch, seq):
    bw, tp, v, a_blk, a_seg, d_row = tabs
    seg_len = seq // SSM_SEGS
    n_blocks = seg_len // SSM_BLK
    ns = batch * SSM_SEGS
    m = batch * seq
    npair = SSM_GROUPS // SSM_PG
    pc = SSM_PG * SSM_GROUP
    rows = n_blocks * ns
    u = proj[:, U_OFF:U_OFF + SSM_W].reshape(batch, SSM_SEGS, n_blocks, SSM_BLK, npair, pc)
    u = jnp.transpose(u, (4, 2, 0, 1, 3, 5)).reshape(npair, rows, SSM_ROW)
    mat = pl.BlockSpec((None, SSM_ROW, SSM_ROW), lambda i: (i, 0, 0))
    vec = pl.BlockSpec((None, 1, SSM_ROW), lambda i: (i, 0, 0))
    act = pl.BlockSpec((None, rows, SSM_ROW), lambda i: (i, 0, 0))
    y = pl.pallas_call(
        functools.partial(_ssm_kernel, n_seq=batch, n_blocks=n_blocks),
        grid=(npair,), in_specs=[act, mat, mat, mat, vec, vec, vec], out_specs=act,
        out_shape=jax.ShapeDtypeStruct((npair, rows, SSM_ROW), BF16),
        scratch_shapes=[pltpu.VMEM((rows, 4 * SSM_PW), F32), pltpu.VMEM((rows, 4 * SSM_PW), BF16),
                        pltpu.VMEM((ns, 4 * SSM_PW), F32)],
        compiler_params=_params("parallel"), name="ssm")(u, bw, tp, v, a_blk, a_seg, d_row)
    y = y.reshape(npair, n_blocks, batch, SSM_SEGS, SSM_BLK, pc)
    y = jnp.transpose(y, (2, 3, 1, 4, 0, 5)).reshape(m, SSM_W)
    return fused_mm(m, SSM_W, [(y, 0, SSM_W, w_glu, (layer,), 0)], [(y, 0)],
                    lambda dts, ex: ex[0].astype(F32) * _sigmoid(dts[0]), BF16, tm=1024, tn=512, name="ssm_glu")


GMLP_NC = 4


def _gmlp_kernel(zu0_ref, zu1_ref, zv0_ref, zv1_ref, g_ref, ws_ref, bb_ref, o_ref):
    half = MLP_W // 2
    zv = jnp.concatenate([zv0_ref[...], zv1_ref[...]], axis=-1).astype(F32)
    v = _gelu(zv)
    v = (v * lax.rsqrt(jnp.mean(v * v, axis=-1, keepdims=True) + EPS) * g_ref[...]).astype(BF16)
    for g in range(MLP_GROUPS):
        cs = slice(g * MLP_GROUP_W, (g + 1) * MLP_GROUP_W)
        vg = jnp.concatenate([v[n * MLP_CHUNK:(n + 1) * MLP_CHUNK, cs] for n in range(GMLP_NC)], axis=-1)
        s = jnp.dot(ws_ref[g].astype(BF16), vg, preferred_element_type=F32)
        zu_ref = zu0_ref if g * MLP_GROUP_W < half else zu1_ref
        us = slice((g * MLP_GROUP_W) % half, (g * MLP_GROUP_W) % half + MLP_GROUP_W)
        for n in range(GMLP_NC):
            rs = slice(n * MLP_CHUNK, (n + 1) * MLP_CHUNK)
            sn = s[:, n * MLP_GROUP_W:(n + 1) * MLP_GROUP_W] + bb_ref[g]
            o_ref[rs, cs] = (_gelu(zu_ref[rs, us].astype(F32)) * sn).astype(o_ref.dtype)


def gmlp_branch(proj, layer, v_gain, w_s, b_bcast, m):
    rows = GMLP_NC * MLP_CHUNK
    half = MLP_W // 2

    def zspec(off):
        return pl.BlockSpec((rows, half), lambda i, off=off: (i, off // half))

    return pl.pallas_call(
        _gmlp_kernel, grid=(m // rows,),
        in_specs=[zspec(ZU_OFF), zspec(ZU_OFF + half), zspec(ZV_OFF), zspec(ZV_OFF + half),
                  pl.BlockSpec((None, 1, MLP_W), lambda i: (layer, 0, 0)),
                  pl.BlockSpec((None, MLP_GROUPS, MLP_CHUNK, MLP_CHUNK), lambda i: (layer, 0, 0, 0)),
                  pl.BlockSpec((None, MLP_GROUPS, MLP_CHUNK, MLP_GROUP_W), lambda i: (layer, 0, 0, 0))],
        out_specs=pl.BlockSpec((rows, MLP_W), lambda i: (i, 0)),
        out_shape=jax.ShapeDtypeStruct((m, MLP_W), BF16),
        compiler_params=_params("parallel"), name="gmlp",
    )(proj, proj, proj, proj, v_gain, w_s, b_bcast)


DENSE_TM = 1024
DENSE_TF = 512
MOE_TM = 1280
MOE_TF = 256
FFN_SUB = 256


def _ffn_kernel(ue_ref, un_ref, ub_ref, x_ref, w1_ref, w3_ref, w2_ref, o_ref, *, live_counts):
    del ue_ref, ub_ref
    u = pl.program_id(0)
    j = pl.program_id(1)
    nsub = un_ref[u]

    @pl.when(j == 0)
    def _():
        o_ref[...] = jnp.zeros_like(o_ref)

    for k in live_counts:
        @pl.when(nsub == k)
        def _(k=k):
            rows = k * FFN_SUB
            x = x_ref[0:rows, :]
            h1 = jnp.dot(x, w1_ref[...].astype(BF16), preferred_element_type=F32)
            h3 = jnp.dot(x, w3_ref[...].astype(BF16), preferred_element_type=F32)
            act = (_silu(h1) * h3).astype(BF16)
            o_ref[0:rows, :] += jnp.dot(act, w2_ref[...].astype(BF16), preferred_element_type=F32)


def swiglu_ffn(x, unit_expert, unit_nsub, unit_block, w1, w3, w2, lead, live_counts, tm, tf):
    rows, d = x.shape
    f = w1.shape[-1]
    n_units = rows // tm
    nf = f // tf
    nl = len(lead)

    def wmap_up(u, j, ue, un, ub):
        return tuple(lead) + (ue[u], 0, jnp.where(un[u] > 0, j, nf - 1))

    def wmap_down(u, j, ue, un, ub):
        return tuple(lead) + (ue[u], jnp.where(un[u] > 0, j, nf - 1), 0)

    def rmap(u, j, ue, un, ub):
        return (ub[u], 0)

    grid_spec = pltpu.PrefetchScalarGridSpec(
        num_scalar_prefetch=3, grid=(n_units, nf),
        in_specs=[pl.BlockSpec((tm, d), rmap),
                  pl.BlockSpec((None,) * (nl + 1) + (d, tf), wmap_up),
                  pl.BlockSpec((None,) * (nl + 1) + (d, tf), wmap_up),
                  pl.BlockSpec((None,) * (nl + 1) + (tf, d), wmap_down)],
        out_specs=pl.BlockSpec((tm, d), lambda u, j, ue, un, ub: (u, 0)))
    return pl.pallas_call(
        functools.partial(_ffn_kernel, live_counts=tuple(live_counts)), grid_spec=grid_spec,
        out_shape=jax.ShapeDtypeStruct((rows, d), F32),
        compiler_params=_params("arbitrary", "arbitrary"), name="swiglu_ffn",
    )(unit_expert, unit_nsub, unit_block, x, w1, w3, w2)


def _router_kernel(x_ref, g_ref, wr_ref, h_ref, r_ref):
    x = x_ref[...]
    h = x * lax.rsqrt(jnp.mean(x * x, axis=-1, keepdims=True) + EPS) * g_ref[...]
    h_ref[...] = h
    logits = jnp.dot(h, wr_ref[...], preferred_element_type=F32, precision=lax.Precision.HIGHEST)
    lane = lax.broadcasted_iota(jnp.int32, logits.shape, 1)
    neg = jnp.float32(-jnp.inf)
    logits = jnp.where(lane < N_EXPERTS, logits, neg)
    m1 = jnp.max(logits, axis=-1, keepdims=True)
    i1 = jnp.min(jnp.where(logits == m1, lane, LANES), axis=-1, keepdims=True)
    rest = jnp.where(lane == i1, neg, logits)
    m2 = jnp.max(rest, axis=-1, keepdims=True)
    i2 = jnp.min(jnp.where(rest == m2, lane, LANES), axis=-1, keepdims=True)
    e = jnp.exp(m2 - m1)
    g1 = 1.0 / (1.0 + e)
    g2 = e / (1.0 + e)
    r_ref[...] = jnp.where(lane == 0, i1.astype(F32),
                           jnp.where(lane == 1, i2.astype(F32),
                                     jnp.where(lane == 2, g1, jnp.where(lane == 3, g2, 0.0))))


def norm_router(x, gains, layer, w_router_pad, tm=256):
    m, d = x.shape
    row = pl.BlockSpec((tm, d), lambda i: (i, 0))
    return pl.pallas_call(
        _router_kernel, grid=(m // tm,),
        in_specs=[row, pl.BlockSpec((None, 1, d), lambda i: (layer, 0, 0)),
                  pl.BlockSpec((d, LANES), lambda i: (0, 0))],
        out_specs=[row, pl.BlockSpec((tm, LANES), lambda i: (i, 0))],
        out_shape=[jax.ShapeDtypeStruct((m, d), F32), jax.ShapeDtypeStruct((m, LANES), F32)],
        compiler_params=_params("parallel"), name="norm_router")(x, gains, w_router_pad)


DMA_GROUP = 8


def _gather_kernel(src_ref, nv_ref, h_ref, o_ref, buf_ref, sem):
    i = pl.program_id(0)
    tb = buf_ref.shape[0]
    base = i * tb
    nv = nv_ref[i]
    ngroups = nv // DMA_GROUP

    @pl.when(i == 0)
    def _():
        buf_ref[...] = jnp.zeros_like(buf_ref)

    def row_copy(t):
        return pltpu.make_async_copy(h_ref.at[pl.ds(src_ref[base + t], 1)], buf_ref.at[pl.ds(t, 1)], sem)

    def group_copy():
        return pltpu.make_async_copy(h_ref.at[pl.ds(0, DMA_GROUP)], buf_ref.at[pl.ds(0, DMA_GROUP)], sem)

    def issue_group(c, carry):
        for r in range(DMA_GROUP):
            row_copy(c * DMA_GROUP + r).start()
        return carry

    def issue_row(t, carry):
        row_copy(t).start()
        return carry

    def wait_group(c, carry):
        group_copy().wait()
        return carry

    def wait_row(t, carry):
        row_copy(t).wait()
        return carry

    lax.fori_loop(0, ngroups, issue_group, 0)
    lax.fori_loop(ngroups * DMA_GROUP, nv, issue_row, 0)
    lax.fori_loop(0, ngroups, wait_group, 0)
    lax.fori_loop(ngroups * DMA_GROUP, nv, wait_row, 0)
    row = lax.broadcasted_iota(jnp.int32, o_ref.shape, 0)
    o_ref[...] = jnp.where(row < nv, buf_ref[...], 0.0).astype(o_ref.dtype)


def moe_gather(h, src, n_valid, cap, tb):
    _, d = h.shape
    grid_spec = pltpu.PrefetchScalarGridSpec(
        num_scalar_prefetch=2, grid=(cap // tb,),
        in_specs=[pl.BlockSpec(memory_space=pl.ANY)],
        out_specs=pl.BlockSpec((tb, d), lambda i, s, n: (i, 0)),
        scratch_shapes=[pltpu.VMEM((tb, d), h.dtype), pltpu.SemaphoreType.DMA(())])
    return pl.pallas_call(
        _gather_kernel, grid_spec=grid_spec,
        out_shape=jax.ShapeDtypeStruct((cap, d), BF16),
        compiler_params=_params("arbitrary"), name="moe_gather")(src, n_valid, h)


COMBINE_TB = 256


def _combine_kernel(dest_ref, x_ref, r_ref, g_ref, y_ref, xo_ref, h_ref, buf_ref, sem):
    base = pl.program_id(0) * COMBINE_TB

    def row_copy(t, k):
        return pltpu.make_async_copy(y_ref.at[pl.ds(dest_ref[TOP_K * (base + t) + k], 1)],
                                     buf_ref.at[k, pl.ds(t, 1)], sem)

    def issue_group(c, carry):
        for r in range(DMA_GROUP // TOP_K):
            for k in range(TOP_K):
                row_copy(c * (DMA_GROUP // TOP_K) + r, k).start()
        return carry

    lax.fori_loop(0, COMBINE_TB * TOP_K // DMA_GROUP, issue_group, 0)
    for k in range(TOP_K):
        pltpu.make_async_copy(y_ref.at[pl.ds(0, COMBINE_TB)], buf_ref.at[k], sem).wait()
    r = r_ref[...]
    g1 = r[:, 2:3]
    g2 = r[:, 3:4]
    x = x_ref[...] + (buf_ref[0] * g1 + buf_ref[1] * g2)
    xo_ref[...] = x
    y = x * lax.rsqrt(jnp.mean(x * x, axis=-1, keepdims=True) + EPS)
    h_ref[...] = (y * g_ref[...]).astype(h_ref.dtype)


def moe_combine(x, route, dest, ybuf, gains, layer):
    m, d = x.shape
    row = pl.BlockSpec((COMBINE_TB, d), lambda i, dr: (i, 0))
    grid_spec = pltpu.PrefetchScalarGridSpec(
        num_scalar_prefetch=1, grid=(m // COMBINE_TB,),
        in_specs=[row, pl.BlockSpec((COMBINE_TB, LANES), lambda i, dr: (i, 0)),
                  pl.BlockSpec((None, 1, d), lambda i, dr: (layer, 0, 0)),
                  pl.BlockSpec(memory_space=pl.ANY)],
        out_specs=[row, row],
        scratch_shapes=[pltpu.VMEM((TOP_K, COMBINE_TB, d), F32), pltpu.SemaphoreType.DMA(())])
    return pl.pallas_call(
        _combine_kernel, grid_spec=grid_spec,
        out_shape=[jax.ShapeDtypeStruct((m, d), F32), jax.ShapeDtypeStruct((m, d), BF16)],
        compiler_params=_params("arbitrary"), name="moe_combine")(dest, x, route, gains, ybuf)


def moe_layer(x1, ffn_norm, ple_norm, layer, w_router, e_w1, e_w3, e_w2, j):
    m, d = x1.shape
    wr_pad = jnp.zeros((d, LANES), F32).at[:, :N_EXPERTS].set(w_router)
    h, route = norm_router(x1, ffn_norm, layer, wr_pad)
    e_flat = route[:, :TOP_K].astype(jnp.int32).reshape(-1)
    onehot = (e_flat[:, None] == jnp.arange(N_EXPERTS)[None, :]).astype(jnp.int32)
    csum = jnp.cumsum(onehot, axis=0)
    rank = jnp.sum((csum - onehot) * onehot, axis=1)
    counts = csum[-1]
    tm = MOE_TM
    n_units_e = (counts + tm - 1) // tm
    unit_end = jnp.cumsum(n_units_e)
    unit_start = unit_end - n_units_e
    dest = (unit_start[e_flat] * tm + rank).astype(jnp.int32)
    n_units = (m * TOP_K) // tm + N_EXPERTS
    uidx = jnp.arange(n_units)
    ue = jnp.minimum(jnp.searchsorted(unit_end, uidx, side='right'), N_EXPERTS - 1).astype(jnp.int32)
    live_rows = jnp.clip(counts[ue] - (uidx - unit_start[ue]) * tm, 0, tm)
    live_rows = jnp.where(uidx < unit_end[-1], live_rows, 0)
    un = ((live_rows + FFN_SUB - 1) // FFN_SUB).astype(jnp.int32)
    last_live = jnp.maximum(unit_end[-1] - 1, 0)
    ue = jnp.where(uidx < unit_end[-1], ue, ue[last_live]).astype(jnp.int32)
    ub = jnp.where(uidx < unit_end[-1], uidx, last_live).astype(jnp.int32)

    cap = n_units * tm
    src = jnp.zeros((cap,), jnp.int32).at[dest].set(jnp.arange(m * TOP_K, dtype=jnp.int32) // TOP_K)

    xbuf = moe_gather(h, src, live_rows.astype(jnp.int32), cap, tm)
    ybuf = swiglu_ffn(xbuf, ue, un, ub, e_w1, e_w3, e_w2, (j,), range(1, tm // FFN_SUB + 1), tm, MOE_TF)
    return moe_combine(x1, route, dest, ybuf, ple_norm, layer)


def kernel(x, p, mix_norm, w_in, q_norm, k_norm, ssm_lambda_re, ssm_lambda_im, ssm_log_dt, ssm_b_re, ssm_b_im,
           ssm_c_re, ssm_c_im, ssm_d, ssm_glu_w, gmlp_v_norm, gmlp_ws, gmlp_b, w_branch, w_out, ffn_norm,
           dense_w1, dense_w3, dense_w2, router_w, expert_w1, expert_w3, expert_w2, ple_norm, ple_gate_w,
           ple_proj_w):
    batch, seq, d = x.shape
    depth = w_in.shape[0]
    n_in = w_in.shape[-1]
    m = batch * seq
    xs = x.reshape(m, d)
    rope_c, rope_s = rope_tables(seq)

    def g3(a):
        return a.reshape(a.shape[0], 1, a.shape[1])

    mix_norm, q_norm, k_norm, gmlp_v_norm, ffn_norm, ple_norm = map(
        g3, (mix_norm, q_norm, k_norm, gmlp_v_norm, ffn_norm, ple_norm))
    b_bcast = jnp.broadcast_to(gmlp_b[..., None], gmlp_b.shape + (MLP_GROUP_W,))
    p2 = p.reshape(depth, m, p.shape[-1])
    dense_units = m // DENSE_TM
    dense_un = jnp.full((dense_units,), DENSE_TM // FFN_SUB, jnp.int32)

    for i in range(depth):
        _, h = add_norm(xs, None, mix_norm, i)
        proj = fused_mm(m, n_in, [(h, 0, d, w_in, (i,), 0)], [], lambda dts, ex: dts[0], BF16,
                        tm=min(2048, m), tn=512, name="in_proj")
        attn = attention(proj, q_norm, k_norm, i, rope_c, rope_s, batch, seq)
        tabs = _ssm_tables(ssm_lambda_re[i], ssm_lambda_im[i], ssm_log_dt[i], ssm_b_re[i], ssm_b_im[i],
                           ssm_c_re[i], ssm_c_im[i], ssm_d[i], seq // SSM_SEGS)
        ssm = ssm_branch(proj, i, tabs, ssm_glu_w, batch, seq)
        mlp = gmlp_branch(proj, i, gmlp_v_norm, gmlp_ws, b_bcast, m)

        tn = 512
        merged = fused_mm(
            m, d,
            [(br, 0, br.shape[1], w_branch, (i, n), 0) for n, br in enumerate((attn, ssm, mlp))],
            [(proj, (GATE_OFF + n * d) // tn) for n in range(N_BRANCH)],
            lambda dts, ex: sum(_sigmoid(e.astype(F32)) * dt for e, dt in zip(ex, dts)),
            BF16, tm=1024, tn=tn, name="branch_merge")
        x1 = fused_mm(m, d, [(merged, 0, d, w_out, (i,), 0)], [(xs, 0)],
                      lambda dts, ex: ex[0] + dts[0], F32, tm=1024, tn=512, name="out_proj")

        j = i // 2
        if i % 2 == 0:
            _, h2 = add_norm(x1, None, ffn_norm, i)
            dense_ue = jnp.full((dense_units,), j, jnp.int32)
            y = swiglu_ffn(h2, dense_ue, dense_un, jnp.arange(dense_units, dtype=jnp.int32),
                           dense_w1, dense_w3, dense_w2, (), (DENSE_TM // FFN_SUB,), DENSE_TM, DENSE_TF)
            x2, hn = add_norm(x1, y, ple_norm, i)
        else:
            x2, hn = moe_layer(x1, ffn_norm, ple_norm, i, router_w[j], expert_w1, expert_w3, expert_w2, j)

        xs = fused_mm(
            m, d,
            [(hn, 0, d, ple_gate_w, (i,), 0), (p2[i], 0, p.shape[-1], ple_proj_w, (i,), 0)],
            [(x2, 0)],
            lambda dts, ex: ex[0] + _sigmoid(dts[0]) * dts[1], F32, tm=1024, tn=512, name="ple")
    return xs.reshape(batch, seq, d)
```

```python
import functools
import math

import jax
import jax.numpy as jnp
import numpy as np
from jax import lax
from jax.experimental import pallas as pl
from jax.experimental.pallas import tpu as pltpu

GRID_W = 64
ROPE_THETA = 10000.0
HEAD_DIM = 128
N_Q_HEADS = 8
N_KV_HEADS = 2
ATTN_W = N_Q_HEADS * HEAD_DIM
KV_W = N_KV_HEADS * HEAD_DIM
SSM_GROUP = 16
SSM_GROUPS = 64
SSM_W = SSM_GROUP * SSM_GROUPS
SSM_STATE = 64
MLP_CHUNK = 128
MLP_GROUPS = 8
MLP_GROUP_W = 128
MLP_W = MLP_GROUPS * MLP_GROUP_W
N_BRANCH = 3
N_EXPERTS = 8
TOP_K = 2
EPS = 1e-6

Q_OFF = 0
K_OFF = ATTN_W
V_OFF = K_OFF + KV_W
U_OFF = V_OFF + KV_W
ZU_OFF = U_OFF + SSM_W
ZV_OFF = ZU_OFF + MLP_W
GATE_OFF = ZV_OFF + MLP_W

V7X_VMEM_LIMIT_BYTES = 60 * 1024 * 1024
LANES = 128

BF16 = jnp.bfloat16
F32 = jnp.float32


def _params(*sem):
    return pltpu.CompilerParams(dimension_semantics=sem, vmem_limit_bytes=V7X_VMEM_LIMIT_BYTES)


def _gelu(x):
    c = math.sqrt(2.0 / math.pi)
    return 0.5 * x * (1.0 + jnp.tanh(c * (x + 0.044715 * (x * x * x))))


def _sigmoid(x):
    return 1.0 / (1.0 + jnp.exp(-x))


def _silu(x):
    return x * _sigmoid(x)


def _add_norm_kernel(*refs, has_delta):
    if has_delta:
        x_ref, d_ref, g_ref, xo_ref, h_ref = refs
        x = x_ref[...] + d_ref[...]
        xo_ref[...] = x
    else:
        x_ref, g_ref, h_ref = refs
        x = x_ref[...]
    y = x * lax.rsqrt(jnp.mean(x * x, axis=-1, keepdims=True) + EPS)
    h_ref[...] = (y * g_ref[...]).astype(h_ref.dtype)


def add_norm(x, delta, gains, layer, tm=512):
    m, d = x.shape
    row = pl.BlockSpec((tm, d), lambda i: (i, 0))
    gspec = pl.BlockSpec((None, 1, d), lambda i: (layer, 0, 0))
    if delta is None:
        h = pl.pallas_call(
            functools.partial(_add_norm_kernel, has_delta=False),
            grid=(m // tm,), in_specs=[row, gspec], out_specs=row,
            out_shape=jax.ShapeDtypeStruct((m, d), BF16),
            compiler_params=_params("parallel"), name="norm")(x, gains)
        return x, h
    xo, h = pl.pallas_call(
        functools.partial(_add_norm_kernel, has_delta=True),
        grid=(m // tm,), in_specs=[row, row, gspec], out_specs=[row, row],
        out_shape=[jax.ShapeDtypeStruct((m, d), F32), jax.ShapeDtypeStruct((m, d), BF16)],
        compiler_params=_params("parallel"), name="add_norm")(x, delta, gains)
    return xo, h


def _fused_mm_kernel(*refs, n_dots, n_extras, epilogue):
    a_refs = refs[:n_dots]
    w_refs = refs[n_dots:2 * n_dots]
    e_refs = refs[2 * n_dots:2 * n_dots + n_extras]
    o_ref = refs[2 * n_dots + n_extras]
    wb_refs = refs[2 * n_dots + n_extras + 1:]

    @pl.when(pl.program_id(1) == 0)
    def _():
        for w_ref, wb_ref in zip(w_refs, wb_refs):
            wb_ref[...] = w_ref[...].astype(BF16)

    dots = []
    for a_ref, wb_ref in zip(a_refs, wb_refs):
        a = a_ref[...]
        if a.dtype != BF16:
            a = a.astype(BF16)
        dots.append(jnp.dot(a, wb_ref[...], preferred_element_type=F32))
    extras = [e_ref[...] for e_ref in e_refs]
    o_ref[...] = epilogue(dots, extras).astype(o_ref.dtype)


def fused_mm(m, n, dots, extras, epilogue, out_dtype, tm, tn, name):
    in_specs, args, scratch = [], [], []
    for a, acb, k, _, _, _ in dots:
        in_specs.append(pl.BlockSpec((tm, k), lambda j, i, acb=acb: (i, acb)))
        args.append(a)
    for _, _, k, w, lead, off in dots:
        in_specs.append(pl.BlockSpec((None,) * len(lead) + (k, tn),
                                     lambda j, i, lead=tuple(lead), off=off: lead + (0, off + j)))
        args.append(w)
        scratch.append(pltpu.VMEM((k, tn), BF16))
    for e, off in extras:
        in_specs.append(pl.BlockSpec((tm, tn), lambda j, i, off=off: (i, off + j)))
        args.append(e)
    return pl.pallas_call(
        functools.partial(_fused_mm_kernel, n_dots=len(dots), n_extras=len(extras), epilogue=epilogue),
        grid=(n // tn, m // tm), in_specs=in_specs,
        out_specs=pl.BlockSpec((tm, tn), lambda j, i: (i, j)),
        out_shape=jax.ShapeDtypeStruct((m, n), out_dtype),
        scratch_shapes=scratch,
        compiler_params=_params("arbitrary", "arbitrary"), name=name)(*args)


def _rope(x, c, s):
    lane = lax.broadcasted_iota(jnp.int32, x.shape, x.ndim - 1)
    quarter = HEAD_DIM // 4
    partner = jnp.where((lane % (2 * quarter)) < quarter,
                        pltpu.roll(x, HEAD_DIM - quarter, x.ndim - 1),
                        pltpu.roll(x, quarter, x.ndim - 1))
    return x * c + partner * s


def _head_norm(x, g):
    return x * lax.rsqrt(jnp.mean(x * x, axis=-1, keepdims=True) + EPS) * g


def _attn_kernel(q_ref, k_ref, v_ref, cq_ref, sq_ref, ck_ref, sk_ref, qg_ref, kg_ref, o_ref, ks_ref, vs_ref, *,
                 rep):
    @pl.when(pl.program_id(2) == 0)
    def _():
        k = _head_norm(k_ref[...].astype(F32), kg_ref[...])
        ks_ref[...] = _rope(k, ck_ref[...], sk_ref[...]).astype(BF16)
        vs_ref[:, 0:HEAD_DIM] = v_ref[...]
        vs_ref[:, HEAD_DIM:2 * HEAD_DIM] = jnp.ones((v_ref.shape[0], HEAD_DIM), BF16)

    scale = HEAD_DIM ** -0.5 * math.log2(math.e)
    cq = cq_ref[...]
    sq = sq_ref[...]
    for hh in range(rep):
        sl = slice(hh * HEAD_DIM, (hh + 1) * HEAD_DIM)
        q = _head_norm(q_ref[:, sl].astype(F32), qg_ref[...])
        q = (_rope(q, cq, sq) * scale).astype(BF16)
        s = lax.dot_general(q, ks_ref[...], (((1,), (1,)), ((), ())), preferred_element_type=F32)
        m = jnp.max(s, axis=-1, keepdims=True)
        p = jnp.exp2(s - m).astype(BF16)
        o = jnp.dot(p, vs_ref[...], preferred_element_type=F32)
        o_ref[:, sl] = (o[:, 0:HEAD_DIM] / o[:, HEAD_DIM:HEAD_DIM + 1]).astype(o_ref.dtype)


def attention(proj, q_gain, k_gain, layer, rope_c, rope_s, batch, seq, tq=256):
    rep = N_Q_HEADS // N_KV_HEADS
    qw = rep * HEAD_DIM
    nq = seq // tq
    gspec = pl.BlockSpec((None, 1, HEAD_DIM), lambda b, g, i: (layer, 0, 0))
    return pl.pallas_call(
        functools.partial(_attn_kernel, rep=rep),
        grid=(batch, N_KV_HEADS, nq),
        in_specs=[
            pl.BlockSpec((tq, qw), lambda b, g, i: (b * nq + i, Q_OFF // qw + g)),
            pl.BlockSpec((seq, HEAD_DIM), lambda b, g, i: (b, K_OFF // HEAD_DIM + g)),
            pl.BlockSpec((seq, HEAD_DIM), lambda b, g, i: (b, V_OFF // HEAD_DIM + g)),
            pl.BlockSpec((tq, HEAD_DIM), lambda b, g, i: (i, 0)),
            pl.BlockSpec((tq, HEAD_DIM), lambda b, g, i: (i, 0)),
            pl.BlockSpec((seq, HEAD_DIM), lambda b, g, i: (0, 0)),
            pl.BlockSpec((seq, HEAD_DIM), lambda b, g, i: (0, 0)),
            gspec, gspec,
        ],
        out_specs=pl.BlockSpec((tq, qw), lambda b, g, i: (b * nq + i, g)),
        out_shape=jax.ShapeDtypeStruct((batch * seq, ATTN_W), BF16),
        scratch_shapes=[pltpu.VMEM((seq, HEAD_DIM), BF16), pltpu.VMEM((seq, 2 * HEAD_DIM), BF16)],
        compiler_params=_params("arbitrary", "arbitrary", "arbitrary"), name="attention",
    )(proj, proj, proj, rope_c, rope_s, rope_c, rope_s, q_gain, k_gain)


def rope_tables(seq):
    rows = seq // GRID_W
    t = jnp.arange(seq)
    pos = jnp.stack([t // GRID_W - rows // 2, t % GRID_W - GRID_W // 2], axis=-1).astype(F32)
    n_freq = HEAD_DIM // 4
    inv_freq = ROPE_THETA ** (-jnp.arange(n_freq, dtype=F32) / n_freq)
    ang = pos[:, :, None] * inv_freq
    cos, sin = jnp.cos(ang), jnp.sin(ang)
    c = jnp.concatenate([cos[:, 0], cos[:, 0], cos[:, 1], cos[:, 1]], axis=-1)
    s = jnp.concatenate([-sin[:, 0], sin[:, 0], -sin[:, 1], sin[:, 1]], axis=-1)
    return c, s


SSM_SEGS = 8
SSM_BLK = 16
SSM_PG = 2
SSM_PW = SSM_PG * SSM_STATE
SSM_ROW = SSM_BLK * SSM_PG * SSM_GROUP


def _ssm_kernel(u_ref, bw_ref, h_ref, vt_ref, a16_ref, at_ref, d_ref, o_ref, z_ref, s_ref, e_ref, t_ref, *,
                n_seq, n_blocks):
    ns = n_seq * SSM_SEGS
    pw = SSM_PW
    u = u_ref[...]
    z_ref[...] = jnp.dot(u, bw_ref[...], preferred_element_type=F32)

    def part(k):
        return slice(k * pw, (k + 1) * pw)

    afr, afi, abr, abi = [jnp.broadcast_to(a16_ref[:, part(k)], (ns, pw)) for k in range(4)]

    def rows(blk):
        return slice(blk * ns, (blk + 1) * ns)

    def scan(state, keep):
        fr, fi, br, bi = state
        for i in range(n_blocks):
            mf, mb = i, n_blocks - 1 - i
            if keep:
                s_ref[rows(mf), part(0)] = fr.astype(BF16)
                s_ref[rows(mf), part(1)] = fi.astype(BF16)
                s_ref[rows(mb), part(2)] = br.astype(BF16)
                s_ref[rows(mb), part(3)] = bi.astype(BF16)
            zfr, zfi = z_ref[rows(mf), part(0)], z_ref[rows(mf), part(1)]
            zbr, zbi = z_ref[rows(mb), part(2)], z_ref[rows(mb), part(3)]
            fr, fi = afr * fr - afi * fi + zfr, afr * fi + afi * fr + zfi
            br, bi = abr * br - abi * bi + zbr, abr * bi + abi * br + zbi
        return fr, fi, br, bi

    zero = jnp.zeros((ns, pw), F32)
    ends = scan((zero, zero, zero, zero), keep=False)
    for k in range(4):
        e_ref[:, part(k)] = ends[k]
    row0 = jnp.zeros((1, pw), F32)
    for b in range(n_seq):
        for kr, ki, order in ((0, 1, range(SSM_SEGS)), (2, 3, range(SSM_SEGS - 1, -1, -1))):
            tr, ti = at_ref[:, part(kr)], at_ref[:, part(ki)]
            cr, ci = row0, row0
            for q in order:
                row = slice(b * SSM_SEGS + q, b * SSM_SEGS + q + 1)
                er, ei = e_ref[row, part(kr)], e_ref[row, part(ki)]
                e_ref[row, part(kr)] = cr
                e_ref[row, part(ki)] = ci
                cr, ci = tr * cr - ti * ci + er, tr * ci + ti * cr + ei
    scan(tuple(e_ref[:, part(k)] for k in range(4)), keep=True)

    xw = SSM_PG * SSM_GROUP
    h = h_ref[...]
    for rp in range(SSM_BLK):
        off = xw * (SSM_BLK - 1 - rp)
        t_ref[rp * xw:(rp + 1) * xw, :] = h[:, off:off + SSM_ROW].astype(BF16)

    y = jnp.dot(u, t_ref[...], preferred_element_type=F32)
    y = y + lax.dot_general(s_ref[...], vt_ref[...], (((1,), (1,)), ((), ())), preferred_element_type=F32)
    y = y + d_ref[...] * u.astype(F32)
    o_ref[...] = _gelu(y).astype(o_ref.dtype)


def _ssm_tables(lam_re, lam_im, log_dt, b_re, b_im, c_re, c_im, d_skip, seg_len):
    g, p, c, r, pg = SSM_GROUPS, SSM_STATE, SSM_GROUP, SSM_BLK, SSM_PG
    npair = g // pg
    lre, lim = lam_re.astype(F32), lam_im.astype(F32)
    dt = jnp.exp(log_dt.astype(F32))[..., None]

    def cexp(xr, xi):
        e = jnp.exp(xr)
        return e * jnp.cos(xi), e * jnp.sin(xi)

    def cmul(ar, ai, br, bi):
        return ar * br - ai * bi, ar * bi + ai * br

    steps = jnp.arange(r + 1, dtype=F32)[None, :, None, None]
    pr, pi = cexp((lre * dt)[:, None] * steps, (lim * dt)[:, None] * steps)
    den = lre * lre + lim * lim
    qr = ((pr[:, 1] - 1.0) * lre + pi[:, 1] * lim) / den
    qi = (pi[:, 1] * lre - (pr[:, 1] - 1.0) * lim) / den
    bbr, bbi = cmul(qr[..., None], qi[..., None], b_re.astype(F32), b_im.astype(F32))
    ccr, cci = c_re.astype(F32), c_im.astype(F32)

    pw_r, pw_i = pr.reshape(2, r + 1, npair, SSM_PW), pi.reshape(2, r + 1, npair, SSM_PW)
    bt_r = jnp.transpose(bbr, (0, 3, 1, 2)).reshape(2, c, npair, SSM_PW)
    bt_i = jnp.transpose(bbi, (0, 3, 1, 2)).reshape(2, c, npair, SSM_PW)
    ct_r = jnp.transpose(ccr, (0, 2, 1, 3)).reshape(2, c, npair, SSM_PW)
    ct_i = jnp.transpose(cci, (0, 2, 1, 3)).reshape(2, c, npair, SSM_PW)
    own = (jnp.arange(SSM_PW)[None, :] // p == jnp.arange(pg)[:, None]).astype(F32)

    def operator(ar, ai, mr, mi, neg_im):
        ar, ai = [jnp.transpose(t, (1, 0, 2))[:, :, None, None, :] for t in (ar, ai)]
        mr, mi = [jnp.transpose(t, (1, 0, 2))[:, None, None, :, :] for t in (mr, mi)]
        xr, xi = cmul(ar, ai, mr, mi)
        msk = own[None, None, :, None, :]
        shape = (npair, SSM_ROW, SSM_PW)
        return (xr * msk).reshape(shape), ((-xi if neg_im else xi) * msk).reshape(shape)

    bw = jnp.concatenate(
        operator(pw_r[0, :r][::-1], pw_i[0, :r][::-1], bt_r[0], bt_i[0], False)
        + operator(pw_r[1, :r], pw_i[1, :r], bt_r[1], bt_i[1], False), axis=-1)
    vt = jnp.concatenate(
        operator(pw_r[0, 1:], pw_i[0, 1:], ct_r[0], ct_i[0], True)
        + operator(pw_r[1, 1:][::-1], pw_i[1, 1:][::-1], ct_r[1], ct_i[1], True), axis=-1)

    xr, xi = cmul(ccr[:, :, None], cci[:, :, None], jnp.transpose(pr[:, :r], (0, 2, 1, 3))[:, :, :, None],
                  jnp.transpose(pi[:, :r], (0, 2, 1, 3))[:, :, :, None])
    klag = jnp.einsum('zgkcq,zgqd->zgkdc', jnp.concatenate([xr, -xi], axis=-1),
                      jnp.concatenate([bbr, bbi], axis=2))
    kf, kb = klag[0], klag[1]
    lagged = jnp.concatenate([kb[:, :0:-1], (kf[:, 0] + kb[:, 0])[:, None], kf[:, 1:]], axis=1)
    lagged = lagged.reshape(npair, pg, 2 * r - 1, c, c)
    h = jnp.einsum('aindc,ij->aidnjc', lagged, jnp.eye(pg, dtype=F32)).reshape(npair, pg * c, 2 * r - 1, pg * c)
    h = jnp.pad(h, ((0, 0), (0, 0), (0, 1), (0, 0))).reshape(npair, pg * c, 2 * r * pg * c)

    def parts(xr, xi):
        y = jnp.stack([xr[0], xi[0], xr[1], xi[1]], axis=0).reshape(4, npair, SSM_PW)
        return jnp.transpose(y, (1, 0, 2)).reshape(npair, 1, 4 * SSM_PW)

    a_blk = parts(pr[:, r], pi[:, r])
    a_seg = parts(*cexp(lre * dt * seg_len, lim * dt * seg_len))
    d_row = jnp.broadcast_to(d_skip.astype(F32).reshape(npair, 1, pg * c), (npair, r, pg * c))
    return bw.astype(BF16), h, vt.astype(BF16), a_blk, a_seg, d_row.reshape(npair, 1, SSM_ROW)


def ssm_branch(proj, layer, tabs, w_glu, batch, seq):
    bw, h, vt, a_blk, a_seg, d_row = tabs
    seg_len = seq // SSM_SEGS
    n_blocks = seg_len // SSM_BLK
    ns = batch * SSM_SEGS
    m = batch * seq
    npair = SSM_GROUPS // SSM_PG
    pc = SSM_PG * SSM_GROUP
    rows = n_blocks * ns
    u = proj[:, U_OFF:U_OFF + SSM_W].reshape(batch, SSM_SEGS, n_blocks, SSM_BLK, npair, pc)
    u = jnp.transpose(u, (4, 2, 0, 1, 3, 5)).reshape(npair, rows, SSM_ROW)
    mat = pl.BlockSpec((None, SSM_ROW, SSM_ROW), lambda i: (i, 0, 0))
    vec = pl.BlockSpec((None, 1, SSM_ROW), lambda i: (i, 0, 0))
    act = pl.BlockSpec((None, rows, SSM_ROW), lambda i: (i, 0, 0))
    y = pl.pallas_call(
        functools.partial(_ssm_kernel, n_seq=batch, n_blocks=n_blocks),
        grid=(npair,),
        in_specs=[act, mat, pl.BlockSpec((None,) + h.shape[1:], lambda i: (i, 0, 0)), mat, vec, vec, vec],
        out_specs=act,
        out_shape=jax.ShapeDtypeStruct((npair, rows, SSM_ROW), BF16),
        scratch_shapes=[pltpu.VMEM((rows, 4 * SSM_PW), F32), pltpu.VMEM((rows, 4 * SSM_PW), BF16),
                        pltpu.VMEM((ns, 4 * SSM_PW), F32), pltpu.VMEM((SSM_ROW, SSM_ROW), BF16)],
        compiler_params=_params("parallel"), name="ssm")(u, bw, h, vt, a_blk, a_seg, d_row)
    y = y.reshape(npair, n_blocks, batch, SSM_SEGS, SSM_BLK, pc)
    y = jnp.transpose(y, (2, 3, 1, 4, 0, 5)).reshape(m, SSM_W)
    return fused_mm(m, SSM_W, [(y, 0, SSM_W, w_glu, (layer,), 0)], [(y, 0)],
                    lambda dts, ex: ex[0].astype(F32) * _sigmoid(dts[0]), BF16, tm=1024, tn=512, name="ssm_glu")


GMLP_NC = 4


def _gmlp_kernel(zu0_ref, zu1_ref, zv0_ref, zv1_ref, g_ref, ws_ref, bb_ref, o_ref):
    half = MLP_W // 2
    zv = jnp.concatenate([zv0_ref[...], zv1_ref[...]], axis=-1).astype(F32)
    v = _gelu(zv)
    v = (v * lax.rsqrt(jnp.mean(v * v, axis=-1, keepdims=True) + EPS) * g_ref[...]).astype(BF16)
    for g in range(MLP_GROUPS):
        cs = slice(g * MLP_GROUP_W, (g + 1) * MLP_GROUP_W)
        vg = jnp.concatenate([v[n * MLP_CHUNK:(n + 1) * MLP_CHUNK, cs] for n in range(GMLP_NC)], axis=-1)
        s = jnp.dot(ws_ref[g].astype(BF16), vg, preferred_element_type=F32)
        zu_ref = zu0_ref if g * MLP_GROUP_W < half else zu1_ref
        us = slice((g * MLP_GROUP_W) % half, (g * MLP_GROUP_W) % half + MLP_GROUP_W)
        for n in range(GMLP_NC):
            rs = slice(n * MLP_CHUNK, (n + 1) * MLP_CHUNK)
            sn = s[:, n * MLP_GROUP_W:(n + 1) * MLP_GROUP_W] + bb_ref[g]
            o_ref[rs, cs] = (_gelu(zu_ref[rs, us].astype(F32)) * sn).astype(o_ref.dtype)


def gmlp_branch(proj, layer, v_gain, w_s, b_bcast, m):
    rows = GMLP_NC * MLP_CHUNK
    half = MLP_W // 2

    def zspec(off):
        return pl.BlockSpec((rows, half), lambda i, off=off: (i, off // half))

    return pl.pallas_call(
        _gmlp_kernel, grid=(m // rows,),
        in_specs=[zspec(ZU_OFF), zspec(ZU_OFF + half), zspec(ZV_OFF), zspec(ZV_OFF + half),
                  pl.BlockSpec((None, 1, MLP_W), lambda i: (layer, 0, 0)),
                  pl.BlockSpec((None, MLP_GROUPS, MLP_CHUNK, MLP_CHUNK), lambda i: (layer, 0, 0, 0)),
                  pl.BlockSpec((None, MLP_GROUPS, MLP_CHUNK, MLP_GROUP_W), lambda i: (layer, 0, 0, 0))],
        out_specs=pl.BlockSpec((rows, MLP_W), lambda i: (i, 0)),
        out_shape=jax.ShapeDtypeStruct((m, MLP_W), BF16),
        compiler_params=_params("parallel"), name="gmlp",
    )(proj, proj, proj, proj, v_gain, w_s, b_bcast)


DENSE_TM = 1024
DENSE_TF = 512
MOE_TM = 1280
MOE_TF = 256
FFN_SUB = 256


def _ffn_kernel(ue_ref, un_ref, ub_ref, x_ref, w1_ref, w3_ref, w2_ref, o_ref, *, live_counts):
    del ue_ref, ub_ref
    u = pl.program_id(0)
    j = pl.program_id(1)
    nsub = un_ref[u]

    @pl.when(j == 0)
    def _():
        o_ref[...] = jnp.zeros_like(o_ref)

    for k in live_counts:
        @pl.when(nsub == k)
        def _(k=k):
            rows = k * FFN_SUB
            x = x_ref[0:rows, :]
            h1 = jnp.dot(x, w1_ref[...].astype(BF16), preferred_element_type=F32)
            h3 = jnp.dot(x, w3_ref[...].astype(BF16), preferred_element_type=F32)
            act = (_silu(h1) * h3).astype(BF16)
            o_ref[0:rows, :] += jnp.dot(act, w2_ref[...].astype(BF16), preferred_element_type=F32)


def swiglu_ffn(x, unit_expert, unit_nsub, unit_block, w1, w3, w2, lead, live_counts, tm, tf):
    rows, d = x.shape
    f = w1.shape[-1]
    n_units = rows // tm
    nf = f // tf
    nl = len(lead)

    def wmap_up(u, j, ue, un, ub):
        return tuple(lead) + (ue[u], 0, jnp.where(un[u] > 0, j, nf - 1))

    def wmap_down(u, j, ue, un, ub):
        return tuple(lead) + (ue[u], jnp.where(un[u] > 0, j, nf - 1), 0)

    def rmap(u, j, ue, un, ub):
        return (ub[u], 0)

    grid_spec = pltpu.PrefetchScalarGridSpec(
        num_scalar_prefetch=3, grid=(n_units, nf),
        in_specs=[pl.BlockSpec((tm, d), rmap),
                  pl.BlockSpec((None,) * (nl + 1) + (d, tf), wmap_up),
                  pl.BlockSpec((None,) * (nl + 1) + (d, tf), wmap_up),
                  pl.BlockSpec((None,) * (nl + 1) + (tf, d), wmap_down)],
        out_specs=pl.BlockSpec((tm, d), lambda u, j, ue, un, ub: (u, 0)))
    return pl.pallas_call(
        functools.partial(_ffn_kernel, live_counts=tuple(live_counts)), grid_spec=grid_spec,
        out_shape=jax.ShapeDtypeStruct((rows, d), F32),
        compiler_params=_params("arbitrary", "arbitrary"), name="swiglu_ffn",
    )(unit_expert, unit_nsub, unit_block, x, w1, w3, w2)


def _router_kernel(x_ref, g_ref, wr_ref, h_ref, r_ref):
    x = x_ref[...]
    h = x * lax.rsqrt(jnp.mean(x * x, axis=-1, keepdims=True) + EPS) * g_ref[...]
    h_ref[...] = h
    logits = jnp.dot(h, wr_ref[...], preferred_element_type=F32, precision=lax.Precision.HIGHEST)
    lane = lax.broadcasted_iota(jnp.int32, logits.shape, 1)
    neg = jnp.float32(-jnp.inf)
    logits = jnp.where(lane < N_EXPERTS, logits, neg)
    m1 = jnp.max(logits, axis=-1, keepdims=True)
    i1 = jnp.min(jnp.where(logits == m1, lane, LANES), axis=-1, keepdims=True)
    rest = jnp.where(lane == i1, neg, logits)
    m2 = jnp.max(rest, axis=-1, keepdims=True)
    i2 = jnp.min(jnp.where(rest == m2, lane, LANES), axis=-1, keepdims=True)
    e = jnp.exp(m2 - m1)
    g1 = 1.0 / (1.0 + e)
    g2 = e / (1.0 + e)
    r_ref[...] = jnp.where(lane == 0, i1.astype(F32),
                           jnp.where(lane == 1, i2.astype(F32),
                                     jnp.where(lane == 2, g1, jnp.where(lane == 3, g2, 0.0))))


def norm_router(x, gains, layer, w_router_pad, tm=256):
    m, d = x.shape
    row = pl.BlockSpec((tm, d), lambda i: (i, 0))
    return pl.pallas_call(
        _router_kernel, grid=(m // tm,),
        in_specs=[row, pl.BlockSpec((None, 1, d), lambda i: (layer, 0, 0)),
                  pl.BlockSpec((d, LANES), lambda i: (0, 0))],
        out_specs=[row, pl.BlockSpec((tm, LANES), lambda i: (i, 0))],
        out_shape=[jax.ShapeDtypeStruct((m, d), F32), jax.ShapeDtypeStruct((m, LANES), F32)],
        compiler_params=_params("parallel"), name="norm_router")(x, gains, w_router_pad)


DMA_GROUP = 8


def _gather_kernel(src_ref, nv_ref, h_ref, o_ref, buf_ref, sem):
    i = pl.program_id(0)
    tb = buf_ref.shape[0]
    base = i * tb
    nv = nv_ref[i]
    ngroups = nv // DMA_GROUP

    @pl.when(i == 0)
    def _():
        buf_ref[...] = jnp.zeros_like(buf_ref)

    def row_copy(t):
        return pltpu.make_async_copy(h_ref.at[pl.ds(src_ref[base + t], 1)], buf_ref.at[pl.ds(t, 1)], sem)

    def group_copy():
        return pltpu.make_async_copy(h_ref.at[pl.ds(0, DMA_GROUP)], buf_ref.at[pl.ds(0, DMA_GROUP)], sem)

    def issue_group(c, carry):
        for r in range(DMA_GROUP):
            row_copy(c * DMA_GROUP + r).start()
        return carry

    def issue_row(t, carry):
        row_copy(t).start()
        return carry

    def wait_group(c, carry):
        group_copy().wait()
        return carry

    def wait_row(t, carry):
        row_copy(t).wait()
        return carry

    lax.fori_loop(0, ngroups, issue_group, 0)
    lax.fori_loop(ngroups * DMA_GROUP, nv, issue_row, 0)
    lax.fori_loop(0, ngroups, wait_group, 0)
    lax.fori_loop(ngroups * DMA_GROUP, nv, wait_row, 0)
    row = lax.broadcasted_iota(jnp.int32, o_ref.shape, 0)
    o_ref[...] = jnp.where(row < nv, buf_ref[...], 0.0).astype(o_ref.dtype)


def moe_gather(h, src, n_valid, cap, tb):
    _, d = h.shape
    grid_spec = pltpu.PrefetchScalarGridSpec(
        num_scalar_prefetch=2, grid=(cap // tb,),
        in_specs=[pl.BlockSpec(memory_space=pl.ANY)],
        out_specs=pl.BlockSpec((tb, d), lambda i, s, n: (i, 0)),
        scratch_shapes=[pltpu.VMEM((tb, d), h.dtype), pltpu.SemaphoreType.DMA(())])
    return pl.pallas_call(
        _gather_kernel, grid_spec=grid_spec,
        out_shape=jax.ShapeDtypeStruct((cap, d), BF16),
        compiler_params=_params("arbitrary"), name="moe_gather")(src, n_valid, h)


COMBINE_TB = 256


def _combine_kernel(dest_ref, x_ref, r_ref, g_ref, y_ref, xo_ref, h_ref, buf_ref, sem):
    base = pl.program_id(0) * COMBINE_TB

    def row_copy(t, k):
        return pltpu.make_async_copy(y_ref.at[pl.ds(dest_ref[TOP_K * (base + t) + k], 1)],
                                     buf_ref.at[k, pl.ds(t, 1)], sem)

    def issue_group(c, carry):
        for r in range(DMA_GROUP // TOP_K):
            for k in range(TOP_K):
                row_copy(c * (DMA_GROUP // TOP_K) + r, k).start()
        return carry

    lax.fori_loop(0, COMBINE_TB * TOP_K // DMA_GROUP, issue_group, 0)
    for k in range(TOP_K):
        pltpu.make_async_copy(y_ref.at[pl.ds(0, COMBINE_TB)], buf_ref.at[k], sem).wait()
    r = r_ref[...]
    g1 = r[:, 2:3]
    g2 = r[:, 3:4]
    x = x_ref[...] + (buf_ref[0] * g1 + buf_ref[1] * g2)
    xo_ref[...] = x
    y = x * lax.rsqrt(jnp.mean(x * x, axis=-1, keepdims=True) + EPS)
    h_ref[...] = (y * g_ref[...]).astype(h_ref.dtype)


def moe_combine(x, route, dest, ybuf, gains, layer):
    m, d = x.shape
    row = pl.BlockSpec((COMBINE_TB, d), lambda i, dr: (i, 0))
    grid_spec = pltpu.PrefetchScalarGridSpec(
        num_scalar_prefetch=1, grid=(m // COMBINE_TB,),
        in_specs=[row, pl.BlockSpec((COMBINE_TB, LANES), lambda i, dr: (i, 0)),
                  pl.BlockSpec((None, 1, d), lambda i, dr: (layer, 0, 0)),
                  pl.BlockSpec(memory_space=pl.ANY)],
        out_specs=[row, row],
        scratch_shapes=[pltpu.VMEM((TOP_K, COMBINE_TB, d), F32), pltpu.SemaphoreType.DMA(())])
    return pl.pallas_call(
        _combine_kernel, grid_spec=grid_spec,
        out_shape=[jax.ShapeDtypeStruct((m, d), F32), jax.ShapeDtypeStruct((m, d), BF16)],
        compiler_params=_params("arbitrary"), name="moe_combine")(dest, x, route, gains, ybuf)


def moe_layer(x1, ffn_norm, ple_norm, layer, w_router, e_w1, e_w3, e_w2, j):
    m, d = x1.shape
    wr_pad = jnp.zeros((d, LANES), F32).at[:, :N_EXPERTS].set(w_router)
    h, route = norm_router(x1, ffn_norm, layer, wr_pad)
    e_flat = route[:, :TOP_K].astype(jnp.int32).reshape(-1)
    onehot = (e_flat[:, None] == jnp.arange(N_EXPERTS)[None, :]).astype(jnp.int32)
    csum = jnp.cumsum(onehot, axis=0)
    rank = jnp.sum((csum - onehot) * onehot, axis=1)
    counts = csum[-1]
    tm = MOE_TM
    n_units_e = (counts + tm - 1) // tm
    unit_end = jnp.cumsum(n_units_e)
    unit_start = unit_end - n_units_e
    dest = (unit_start[e_flat] * tm + rank).astype(jnp.int32)
    n_units = (m * TOP_K) // tm + N_EXPERTS
    uidx = jnp.arange(n_units)
    ue = jnp.minimum(jnp.searchsorted(unit_end, uidx, side='right'), N_EXPERTS - 1).astype(jnp.int32)
    live_rows = jnp.clip(counts[ue] - (uidx - unit_start[ue]) * tm, 0, tm)
    live_rows = jnp.where(uidx < unit_end[-1], live_rows, 0)
    un = ((live_rows + FFN_SUB - 1) // FFN_SUB).astype(jnp.int32)
    last_live = jnp.maximum(unit_end[-1] - 1, 0)
    ue = jnp.where(uidx < unit_end[-1], ue, ue[last_live]).astype(jnp.int32)
    ub = jnp.where(uidx < unit_end[-1], uidx, last_live).astype(jnp.int32)

    cap = n_units * tm
    src = jnp.zeros((cap,), jnp.int32).at[dest].set(jnp.arange(m * TOP_K, dtype=jnp.int32) // TOP_K)

    xbuf = moe_gather(h, src, live_rows.astype(jnp.int32), cap, tm)
    ybuf = swiglu_ffn(xbuf, ue, un, ub, e_w1, e_w3, e_w2, (j,), range(1, tm // FFN_SUB + 1), tm, MOE_TF)
    return moe_combine(x1, route, dest, ybuf, ple_norm, layer)


def kernel(x, p, mix_norm, w_in, q_norm, k_norm, ssm_lambda_re, ssm_lambda_im, ssm_log_dt, ssm_b_re, ssm_b_im,
           ssm_c_re, ssm_c_im, ssm_d, ssm_glu_w, gmlp_v_norm, gmlp_ws, gmlp_b, w_branch, w_out, ffn_norm,
           dense_w1, dense_w3, dense_w2, router_w, expert_w1, expert_w3, expert_w2, ple_norm, ple_gate_w,
           ple_proj_w):
    batch, seq, d = x.shape
    depth = w_in.shape[0]
    n_in = w_in.shape[-1]
    m = batch * seq
    xs = x.reshape(m, d)
    rope_c, rope_s = rope_tables(seq)

    def g3(a):
        return a.reshape(a.shape[0], 1, a.shape[1])

    mix_norm, q_norm, k_norm, gmlp_v_norm, ffn_norm, ple_norm = map(
        g3, (mix_norm, q_norm, k_norm, gmlp_v_norm, ffn_norm, ple_norm))
    b_bcast = jnp.broadcast_to(gmlp_b[..., None], gmlp_b.shape + (MLP_GROUP_W,))
    p2 = p.reshape(depth, m, p.shape[-1])
    dense_units = m // DENSE_TM
    dense_un = jnp.full((dense_units,), DENSE_TM // FFN_SUB, jnp.int32)

    for i in range(depth):
        _, h = add_norm(xs, None, mix_norm, i)
        proj = fused_mm(m, n_in, [(h, 0, d, w_in, (i,), 0)], [], lambda dts, ex: dts[0], BF16,
                        tm=min(2048, m), tn=512, name="in_proj")
        attn = attention(proj, q_norm, k_norm, i, rope_c, rope_s, batch, seq)
        tabs = _ssm_tables(ssm_lambda_re[i], ssm_lambda_im[i], ssm_log_dt[i], ssm_b_re[i], ssm_b_im[i],
                           ssm_c_re[i], ssm_c_im[i], ssm_d[i], seq // SSM_SEGS)
        ssm = ssm_branch(proj, i, tabs, ssm_glu_w, batch, seq)
        mlp = gmlp_branch(proj, i, gmlp_v_norm, gmlp_ws, b_bcast, m)

        tn = 512
        merged = fused_mm(
            m, d,
            [(br, 0, br.shape[1], w_branch, (i, n), 0) for n, br in enumerate((attn, ssm, mlp))],
            [(proj, (GATE_OFF + n * d) // tn) for n in range(N_BRANCH)],
            lambda dts, ex: sum(_sigmoid(e.astype(F32)) * dt for e, dt in zip(ex, dts)),
            BF16, tm=1024, tn=tn, name="branch_merge")
        x1 = fused_mm(m, d, [(merged, 0, d, w_out, (i,), 0)], [(xs, 0)],
                      lambda dts, ex: ex[0] + dts[0], F32, tm=1024, tn=512, name="out_proj")

        j = i // 2
        if i % 2 == 0:
            _, h2 = add_norm(x1, None, ffn_norm, i)
            dense_ue = jnp.full((dense_units,), j, jnp.int32)
            y = swiglu_ffn(h2, dense_ue, dense_un, jnp.arange(dense_units, dtype=jnp.int32),
                           dense_w1, dense_w3, dense_w2, (), (DENSE_TM // FFN_SUB,), DENSE_TM, DENSE_TF)
            x2, hn = add_norm(x1, y, ple_norm, i)
        else:
            x2, hn = moe_layer(x1, ffn_norm, ple_norm, i, router_w[j], expert_w1, expert_w3, expert_w2, j)

        xs = fused_mm(
            m, d,
            [(hn, 0, d, ple_gate_w, (i,), 0), (p2[i], 0, p.shape[-1], ple_proj_w, (i,), 0)],
            [(x2, 0)],
            lambda dts, ex: ex[0] + _sigmoid(dts[0]) * dts[1], F32, tm=1024, tn=512, name="ple")
    return xs.reshape(batch, seq, d)
```

```python
import functools
import math

import jax
import jax.numpy as jnp
import numpy as np
from jax import lax
from jax.experimental import pallas as pl
from jax.experimental.pallas import tpu as pltpu

GRID_W = 64
ROPE_THETA = 10000.0
HEAD_DIM = 128
N_Q_HEADS = 8
N_KV_HEADS = 2
ATTN_W = N_Q_HEADS * HEAD_DIM
KV_W = N_KV_HEADS * HEAD_DIM
SSM_GROUP = 16
SSM_GROUPS = 64
SSM_W = SSM_GROUP * SSM_GROUPS
SSM_STATE = 64
MLP_CHUNK = 128
MLP_GROUPS = 8
MLP_GROUP_W = 128
MLP_W = MLP_GROUPS * MLP_GROUP_W
N_BRANCH = 3
N_EXPERTS = 8
TOP_K = 2
EPS = 1e-6

Q_OFF = 0
K_OFF = ATTN_W
V_OFF = K_OFF + KV_W
U_OFF = V_OFF + KV_W
ZU_OFF = U_OFF + SSM_W
ZV_OFF = ZU_OFF + MLP_W
GATE_OFF = ZV_OFF + MLP_W

V7X_VMEM_LIMIT_BYTES = 62 * 1024 * 1024
LANES = 128

BF16 = jnp.bfloat16
F32 = jnp.float32


def _params(*sem):
    return pltpu.CompilerParams(dimension_semantics=sem, vmem_limit_bytes=V7X_VMEM_LIMIT_BYTES)


def _gelu(x):
    c = math.sqrt(2.0 / math.pi)
    return 0.5 * x * (1.0 + jnp.tanh(c * (x + 0.044715 * (x * x * x))))


def _sigmoid(x):
    return 1.0 / (1.0 + jnp.exp(-x))


def _silu(x):
    return x * _sigmoid(x)


def _add_norm_kernel(*refs, has_delta):
    if has_delta:
        x_ref, d_ref, g_ref, xo_ref, h_ref = refs
        x = x_ref[...] + d_ref[...]
        xo_ref[...] = x
    else:
        x_ref, g_ref, h_ref = refs
        x = x_ref[...]
    y = x * lax.rsqrt(jnp.mean(x * x, axis=-1, keepdims=True) + EPS)
    h_ref[...] = (y * g_ref[...]).astype(h_ref.dtype)


def add_norm(x, delta, gains, layer, tm=512):
    m, d = x.shape
    row = pl.BlockSpec((tm, d), lambda i: (i, 0))
    gspec = pl.BlockSpec((None, 1, d), lambda i: (layer, 0, 0))
    if delta is None:
        h = pl.pallas_call(
            functools.partial(_add_norm_kernel, has_delta=False),
            grid=(m // tm,), in_specs=[row, gspec], out_specs=row,
            out_shape=jax.ShapeDtypeStruct((m, d), BF16),
            compiler_params=_params("parallel"), name="norm")(x, gains)
        return x, h
    xo, h = pl.pallas_call(
        functools.partial(_add_norm_kernel, has_delta=True),
        grid=(m // tm,), in_specs=[row, row, gspec], out_specs=[row, row],
        out_shape=[jax.ShapeDtypeStruct((m, d), F32), jax.ShapeDtypeStruct((m, d), BF16)],
        compiler_params=_params("parallel"), name="add_norm")(x, delta, gains)
    return xo, h


def _fused_mm_kernel(*refs, n_dots, n_extras, epilogue):
    a_refs = refs[:n_dots]
    w_refs = refs[n_dots:2 * n_dots]
    e_refs = refs[2 * n_dots:2 * n_dots + n_extras]
    o_ref = refs[2 * n_dots + n_extras]
    wb_refs = refs[2 * n_dots + n_extras + 1:]

    @pl.when(pl.program_id(1) == 0)
    def _():
        for w_ref, wb_ref in zip(w_refs, wb_refs):
            wb_ref[...] = w_ref[...].astype(BF16)

    dots = []
    for a_ref, wb_ref in zip(a_refs, wb_refs):
        a = a_ref[...]
        if a.dtype != BF16:
            a = a.astype(BF16)
        dots.append(jnp.dot(a, wb_ref[...], preferred_element_type=F32))
    extras = [e_ref[...] for e_ref in e_refs]
    o_ref[...] = epilogue(dots, extras).astype(o_ref.dtype)


def fused_mm(m, n, dots, extras, epilogue, out_dtype, tm, tn, name):
    in_specs, args, scratch = [], [], []
    for a, acb, k, _, _, _ in dots:
        in_specs.append(pl.BlockSpec((tm, k), lambda j, i, acb=acb: (i, acb)))
        args.append(a)
    for _, _, k, w, lead, off in dots:
        in_specs.append(pl.BlockSpec((None,) * len(lead) + (k, tn),
                                     lambda j, i, lead=tuple(lead), off=off: lead + (0, off + j)))
        args.append(w)
        scratch.append(pltpu.VMEM((k, tn), BF16))
    for e, off in extras:
        in_specs.append(pl.BlockSpec((tm, tn), lambda j, i, off=off: (i, off + j)))
        args.append(e)
    return pl.pallas_call(
        functools.partial(_fused_mm_kernel, n_dots=len(dots), n_extras=len(extras), epilogue=epilogue),
        grid=(n // tn, m // tm), in_specs=in_specs,
        out_specs=pl.BlockSpec((tm, tn), lambda j, i: (i, j)),
        out_shape=jax.ShapeDtypeStruct((m, n), out_dtype),
        scratch_shapes=scratch,
        compiler_params=_params("arbitrary", "arbitrary"), name=name)(*args)


def _rope(x, c, s):
    lane = lax.broadcasted_iota(jnp.int32, x.shape, x.ndim - 1)
    quarter = HEAD_DIM // 4
    partner = jnp.where((lane % (2 * quarter)) < quarter,
                        pltpu.roll(x, HEAD_DIM - quarter, x.ndim - 1),
                        pltpu.roll(x, quarter, x.ndim - 1))
    return x * c + partner * s


def _head_norm(x, g):
    return x * lax.rsqrt(jnp.mean(x * x, axis=-1, keepdims=True) + EPS) * g


def _attn_kernel(q_ref, k_ref, v_ref, cq_ref, sq_ref, ck_ref, sk_ref, qg_ref, kg_ref, o_ref, ks_ref, vs_ref, *,
                 rep):
    @pl.when(pl.program_id(2) == 0)
    def _():
        k = _head_norm(k_ref[...].astype(F32), kg_ref[...])
        ks_ref[...] = _rope(k, ck_ref[...], sk_ref[...]).astype(BF16)
        vs_ref[:, 0:HEAD_DIM] = v_ref[...]
        vs_ref[:, HEAD_DIM:2 * HEAD_DIM] = jnp.ones((v_ref.shape[0], HEAD_DIM), BF16)

    scale = HEAD_DIM ** -0.5 * math.log2(math.e)
    cq = cq_ref[...]
    sq = sq_ref[...]
    for hh in range(rep):
        sl = slice(hh * HEAD_DIM, (hh + 1) * HEAD_DIM)
        q = _head_norm(q_ref[:, sl].astype(F32), qg_ref[...])
        q = (_rope(q, cq, sq) * scale).astype(BF16)
        s = lax.dot_general(q, ks_ref[...], (((1,), (1,)), ((), ())), preferred_element_type=F32)
        m = jnp.max(s, axis=-1, keepdims=True)
        p = jnp.exp2(s - m).astype(BF16)
        o = jnp.dot(p, vs_ref[...], preferred_element_type=F32)
        o_ref[:, sl] = (o[:, 0:HEAD_DIM] / o[:, HEAD_DIM:HEAD_DIM + 1]).astype(o_ref.dtype)


def attention(proj, q_gain, k_gain, layer, rope_c, rope_s, batch, seq, tq=256):
    rep = N_Q_HEADS // N_KV_HEADS
    qw = rep * HEAD_DIM
    nq = seq // tq
    gspec = pl.BlockSpec((None, 1, HEAD_DIM), lambda b, g, i: (layer, 0, 0))
    return pl.pallas_call(
        functools.partial(_attn_kernel, rep=rep),
        grid=(batch, N_KV_HEADS, nq),
        in_specs=[
            pl.BlockSpec((tq, qw), lambda b, g, i: (b * nq + i, Q_OFF // qw + g)),
            pl.BlockSpec((seq, HEAD_DIM), lambda b, g, i: (b, K_OFF // HEAD_DIM + g)),
            pl.BlockSpec((seq, HEAD_DIM), lambda b, g, i: (b, V_OFF // HEAD_DIM + g)),
            pl.BlockSpec((tq, HEAD_DIM), lambda b, g, i: (i, 0)),
            pl.BlockSpec((tq, HEAD_DIM), lambda b, g, i: (i, 0)),
            pl.BlockSpec((seq, HEAD_DIM), lambda b, g, i: (0, 0)),
            pl.BlockSpec((seq, HEAD_DIM), lambda b, g, i: (0, 0)),
            gspec, gspec,
        ],
        out_specs=pl.BlockSpec((tq, qw), lambda b, g, i: (b * nq + i, g)),
        out_shape=jax.ShapeDtypeStruct((batch * seq, ATTN_W), BF16),
        scratch_shapes=[pltpu.VMEM((seq, HEAD_DIM), BF16), pltpu.VMEM((seq, 2 * HEAD_DIM), BF16)],
        compiler_params=_params("arbitrary", "arbitrary", "arbitrary"), name="attention",
    )(proj, proj, proj, rope_c, rope_s, rope_c, rope_s, q_gain, k_gain)


def rope_tables(seq):
    rows = seq // GRID_W
    t = jnp.arange(seq)
    pos = jnp.stack([t // GRID_W - rows // 2, t % GRID_W - GRID_W // 2], axis=-1).astype(F32)
    n_freq = HEAD_DIM // 4
    inv_freq = ROPE_THETA ** (-jnp.arange(n_freq, dtype=F32) / n_freq)
    ang = pos[:, :, None] * inv_freq
    cos, sin = jnp.cos(ang), jnp.sin(ang)
    c = jnp.concatenate([cos[:, 0], cos[:, 0], cos[:, 1], cos[:, 1]], axis=-1)
    s = jnp.concatenate([-sin[:, 0], sin[:, 0], -sin[:, 1], sin[:, 1]], axis=-1)
    return c, s


SSM_SEGS = 8
SSM_BLK = 16
SSM_PG = 2
SSM_PW = SSM_PG * SSM_STATE
SSM_ROW = SSM_BLK * SSM_PG * SSM_GROUP


def _ssm_kernel(u_ref, pw_ref, bt_ref, ct_ref, h_ref, at_ref, d_ref, o_ref, z_ref, s_ref, e_ref, t_ref, bw_ref,
                vt_ref, *, n_seq, n_blocks):
    ns = n_seq * SSM_SEGS
    pw = SSM_PW
    xw = SSM_PG * SSM_GROUP

    def part(k):
        return slice(k * pw, (k + 1) * pw)

    lane = lax.broadcasted_iota(jnp.int32, (SSM_GROUP, pw), 1)
    owns = [lane // SSM_STATE == gp for gp in range(SSM_PG)]

    def build(dst_ref, m_ref, d, step, k, neg_im):
        ar, ai = pw_ref[2 * d, k:k + 1, :], pw_ref[2 * d + 1, k:k + 1, :]
        mr, mi = m_ref[2 * d], m_ref[2 * d + 1]
        xr, xi = ar * mr - ai * mi, ar * mi + ai * mr
        if neg_im:
            xi = -xi
        for gp, own in enumerate(owns):
            rws = slice(step * xw + gp * SSM_GROUP, step * xw + (gp + 1) * SSM_GROUP)
            dst_ref[rws, part(2 * d)] = jnp.where(own, xr, 0.0).astype(BF16)
            dst_ref[rws, part(2 * d + 1)] = jnp.where(own, xi, 0.0).astype(BF16)

    for step in range(SSM_BLK):
        build(bw_ref, bt_ref, 0, step, SSM_BLK - 1 - step, False)
        build(bw_ref, bt_ref, 1, step, step, False)
        build(vt_ref, ct_ref, 0, step, step + 1, True)
        build(vt_ref, ct_ref, 1, step, SSM_BLK - step, True)

    u = u_ref[...]
    z_ref[...] = jnp.dot(u, bw_ref[...], preferred_element_type=F32)
    afr, afi, abr, abi = [jnp.broadcast_to(pw_ref[k, SSM_BLK:SSM_BLK + 1, :], (ns, pw)) for k in range(4)]

    def rows(blk):
        return slice(blk * ns, (blk + 1) * ns)

    def scan(state, keep):
        fr, fi, br, bi = state
        for i in range(n_blocks):
            mf, mb = i, n_blocks - 1 - i
            if keep:
                s_ref[rows(mf), part(0)] = fr.astype(BF16)
                s_ref[rows(mf), part(1)] = fi.astype(BF16)
                s_ref[rows(mb), part(2)] = br.astype(BF16)
                s_ref[rows(mb), part(3)] = bi.astype(BF16)
            zfr, zfi = z_ref[rows(mf), part(0)], z_ref[rows(mf), part(1)]
            zbr, zbi = z_ref[rows(mb), part(2)], z_ref[rows(mb), part(3)]
            fr, fi = afr * fr - afi * fi + zfr, afr * fi + afi * fr + zfi
            br, bi = abr * br - abi * bi + zbr, abr * bi + abi * br + zbi
        return fr, fi, br, bi

    zero = jnp.zeros((ns, pw), F32)
    ends = scan((zero, zero, zero, zero), keep=False)
    for k in range(4):
        e_ref[:, part(k)] = ends[k]
    row0 = jnp.zeros((1, pw), F32)
    for b in range(n_seq):
        for kr, ki, order in ((0, 1, range(SSM_SEGS)), (2, 3, range(SSM_SEGS - 1, -1, -1))):
            tr, ti = at_ref[:, part(kr)], at_ref[:, part(ki)]
            cr, ci = row0, row0
            for q in order:
                row = slice(b * SSM_SEGS + q, b * SSM_SEGS + q + 1)
                er, ei = e_ref[row, part(kr)], e_ref[row, part(ki)]
                e_ref[row, part(kr)] = cr
                e_ref[row, part(ki)] = ci
                cr, ci = tr * cr - ti * ci + er, tr * ci + ti * cr + ei
    scan(tuple(e_ref[:, part(k)] for k in range(4)), keep=True)

    h = h_ref[...]
    for rp in range(SSM_BLK):
        off = xw * (SSM_BLK - 1 - rp)
        t_ref[rp * xw:(rp + 1) * xw, :] = h[:, off:off + SSM_ROW].astype(BF16)

    y = jnp.dot(u, t_ref[...], preferred_element_type=F32)
    y = y + lax.dot_general(s_ref[...], vt_ref[...], (((1,), (1,)), ((), ())), preferred_element_type=F32)
    y = y + d_ref[...] * u.astype(F32)
    o_ref[...] = _gelu(y).astype(o_ref.dtype)


def _ssm_tables(lam_re, lam_im, log_dt, b_re, b_im, c_re, c_im, d_skip, seg_len):
    g, c, r, pg = SSM_GROUPS, SSM_GROUP, SSM_BLK, SSM_PG
    npair = g // pg
    lre, lim = lam_re.astype(F32), lam_im.astype(F32)
    dt = jnp.exp(log_dt.astype(F32))[..., None]

    def cexp(xr, xi):
        e = jnp.exp(xr)
        return e * jnp.cos(xi), e * jnp.sin(xi)

    def cmul(ar, ai, br, bi):
        return ar * br - ai * bi, ar * bi + ai * br

    steps = jnp.arange(r + 1, dtype=F32)[None, :, None, None]
    pr, pi = cexp((lre * dt)[:, None] * steps, (lim * dt)[:, None] * steps)
    den = lre * lre + lim * lim
    qr = ((pr[:, 1] - 1.0) * lre + pi[:, 1] * lim) / den
    qi = (pi[:, 1] * lre - (pr[:, 1] - 1.0) * lim) / den
    bbr, bbi = cmul(qr[..., None], qi[..., None], b_re.astype(F32), b_im.astype(F32))
    ccr, cci = c_re.astype(F32), c_im.astype(F32)

    def per_pair(xr, xi):
        y = jnp.stack([xr, xi], axis=1).reshape(2, 2, xr.shape[1], npair, SSM_PW)
        return jnp.transpose(y, (3, 0, 1, 2, 4)).reshape(npair, 4, xr.shape[1], SSM_PW)

    pw4 = per_pair(pr, pi)
    bt4 = per_pair(jnp.transpose(bbr, (0, 3, 1, 2)), jnp.transpose(bbi, (0, 3, 1, 2)))
    ct4 = per_pair(jnp.transpose(ccr, (0, 2, 1, 3)), jnp.transpose(cci, (0, 2, 1, 3)))

    xr, xi = cmul(ccr[:, :, None], cci[:, :, None], jnp.transpose(pr[:, :r], (0, 2, 1, 3))[:, :, :, None],
                  jnp.transpose(pi[:, :r], (0, 2, 1, 3))[:, :, :, None])
    klag = jnp.einsum('zgkcq,zgqd->zgkdc', jnp.concatenate([xr, -xi], axis=-1),
                      jnp.concatenate([bbr, bbi], axis=2))
    kf, kb = klag[0], klag[1]
    lagged = jnp.concatenate([kb[:, :0:-1], (kf[:, 0] + kb[:, 0])[:, None], kf[:, 1:]], axis=1)
    lagged = lagged.reshape(npair, pg, 2 * r - 1, c, c)
    h = jnp.einsum('aindc,ij->aidnjc', lagged, jnp.eye(pg, dtype=F32)).reshape(npair, pg * c, 2 * r - 1, pg * c)
    h = jnp.pad(h, ((0, 0), (0, 0), (0, 1), (0, 0))).reshape(npair, pg * c, 2 * r * pg * c)

    def parts(xr, xi):
        y = jnp.stack([xr[0], xi[0], xr[1], xi[1]], axis=0).reshape(4, npair, SSM_PW)
        return jnp.transpose(y, (1, 0, 2)).reshape(npair, 1, 4 * SSM_PW)

    a_seg = parts(*cexp(lre * dt * seg_len, lim * dt * seg_len))
    d_row = jnp.broadcast_to(d_skip.astype(F32).reshape(npair, 1, pg * c), (npair, r, pg * c))
    return pw4, bt4, ct4, h, a_seg, d_row.reshape(npair, 1, SSM_ROW)


def ssm_branch(proj, layer, tabs, w_glu, batch, seq):
    pw4, bt4, ct4, h, a_seg, d_row = tabs
    seg_len = seq // SSM_SEGS
    n_blocks = seg_len // SSM_BLK
    ns = batch * SSM_SEGS
    m = batch * seq
    npair = SSM_GROUPS // SSM_PG
    pc = SSM_PG * SSM_GROUP
    rows = n_blocks * ns
    u = proj[:, U_OFF:U_OFF + SSM_W].reshape(batch, SSM_SEGS, n_blocks, SSM_BLK, npair, pc)
    u = jnp.transpose(u, (4, 2, 0, 1, 3, 5)).reshape(npair, rows, SSM_ROW)
    def per_pair(a):
        return pl.BlockSpec((None,) + a.shape[1:], lambda i, nd=a.ndim: (i,) + (0,) * (nd - 1))

    operator = pltpu.VMEM((SSM_ROW, 4 * SSM_PW), BF16)
    y = pl.pallas_call(
        functools.partial(_ssm_kernel, n_seq=batch, n_blocks=n_blocks),
        grid=(npair,),
        in_specs=[per_pair(a) for a in (u, pw4, bt4, ct4, h, a_seg, d_row)],
        out_specs=per_pair(u),
        out_shape=jax.ShapeDtypeStruct((npair, rows, SSM_ROW), BF16),
        scratch_shapes=[pltpu.VMEM((rows, 4 * SSM_PW), F32), pltpu.VMEM((rows, 4 * SSM_PW), BF16),
                        pltpu.VMEM((ns, 4 * SSM_PW), F32), pltpu.VMEM((SSM_ROW, SSM_ROW), BF16),
                        operator, operator],
        compiler_params=_params("parallel"), name="ssm")(u, pw4, bt4, ct4, h, a_seg, d_row)
    y = y.reshape(npair, n_blocks, batch, SSM_SEGS, SSM_BLK, pc)
    y = jnp.transpose(y, (2, 3, 1, 4, 0, 5)).reshape(m, SSM_W)
    return fused_mm(m, SSM_W, [(y, 0, SSM_W, w_glu, (layer,), 0)], [(y, 0)],
                    lambda dts, ex: ex[0].astype(F32) * _sigmoid(dts[0]), BF16, tm=1024, tn=512, name="ssm_glu")


GMLP_NC = 4


def _gmlp_kernel(zu0_ref, zu1_ref, zv0_ref, zv1_ref, g_ref, ws_ref, bb_ref, o_ref):
    half = MLP_W // 2
    zv = jnp.concatenate([zv0_ref[...], zv1_ref[...]], axis=-1).astype(F32)
    v = _gelu(zv)
    v = (v * lax.rsqrt(jnp.mean(v * v, axis=-1, keepdims=True) + EPS) * g_ref[...]).astype(BF16)
    for g in range(MLP_GROUPS):
        cs = slice(g * MLP_GROUP_W, (g + 1) * MLP_GROUP_W)
        vg = jnp.concatenate([v[n * MLP_CHUNK:(n + 1) * MLP_CHUNK, cs] for n in range(GMLP_NC)], axis=-1)
        s = jnp.dot(ws_ref[g].astype(BF16), vg, preferred_element_type=F32)
        zu_ref = zu0_ref if g * MLP_GROUP_W < half else zu1_ref
        us = slice((g * MLP_GROUP_W) % half, (g * MLP_GROUP_W) % half + MLP_GROUP_W)
        for n in range(GMLP_NC):
            rs = slice(n * MLP_CHUNK, (n + 1) * MLP_CHUNK)
            sn = s[:, n * MLP_GROUP_W:(n + 1) * MLP_GROUP_W] + bb_ref[g]
            o_ref[rs, cs] = (_gelu(zu_ref[rs, us].astype(F32)) * sn).astype(o_ref.dtype)


def gmlp_branch(proj, layer, v_gain, w_s, b_bcast, m):
    rows = GMLP_NC * MLP_CHUNK
    half = MLP_W // 2

    def zspec(off):
        return pl.BlockSpec((rows, half), lambda i, off=off: (i, off // half))

    return pl.pallas_call(
        _gmlp_kernel, grid=(m // rows,),
        in_specs=[zspec(ZU_OFF), zspec(ZU_OFF + half), zspec(ZV_OFF), zspec(ZV_OFF + half),
                  pl.BlockSpec((None, 1, MLP_W), lambda i: (layer, 0, 0)),
                  pl.BlockSpec((None, MLP_GROUPS, MLP_CHUNK, MLP_CHUNK), lambda i: (layer, 0, 0, 0)),
                  pl.BlockSpec((None, MLP_GROUPS, MLP_CHUNK, MLP_GROUP_W), lambda i: (layer, 0, 0, 0))],
        out_specs=pl.BlockSpec((rows, MLP_W), lambda i: (i, 0)),
        out_shape=jax.ShapeDtypeStruct((m, MLP_W), BF16),
        compiler_params=_params("parallel"), name="gmlp",
    )(proj, proj, proj, proj, v_gain, w_s, b_bcast)


DENSE_TM = 1024
DENSE_TF = 512
MOE_TM = 1280
MOE_TF = 512
MOE_FC = 256
FFN_SUB = 256


def _ffn_kernel(ue_ref, un_ref, ub_ref, x_ref, w1_ref, w3_ref, w2_ref, o_ref, *, live_counts, fc):
    del ue_ref, ub_ref
    u = pl.program_id(0)
    j = pl.program_id(1)
    nsub = un_ref[u]

    @pl.when(j == 0)
    def _():
        o_ref[...] = jnp.zeros_like(o_ref)

    tf = w2_ref.shape[0]
    for k in live_counts:
        @pl.when(nsub == k)
        def _(k=k):
            rows = k * FFN_SUB
            x = x_ref[0:rows, :]
            for f0 in range(0, tf, fc):
                fs = slice(f0, f0 + fc)
                h1 = jnp.dot(x, w1_ref[:, fs].astype(BF16), preferred_element_type=F32)
                h3 = jnp.dot(x, w3_ref[:, fs].astype(BF16), preferred_element_type=F32)
                act = (_silu(h1) * h3).astype(BF16)
                o_ref[0:rows, :] += jnp.dot(act, w2_ref[fs, :].astype(BF16), preferred_element_type=F32)


def swiglu_ffn(x, unit_expert, unit_nsub, unit_block, w1, w3, w2, lead, live_counts, tm, tf, fc):
    rows, d = x.shape
    f = w1.shape[-1]
    n_units = rows // tm
    nf = f // tf
    nl = len(lead)

    def wmap_up(u, j, ue, un, ub):
        return tuple(lead) + (ue[u], 0, jnp.where(un[u] > 0, j, nf - 1))

    def wmap_down(u, j, ue, un, ub):
        return tuple(lead) + (ue[u], jnp.where(un[u] > 0, j, nf - 1), 0)

    def rmap(u, j, ue, un, ub):
        return (ub[u], 0)

    grid_spec = pltpu.PrefetchScalarGridSpec(
        num_scalar_prefetch=3, grid=(n_units, nf),
        in_specs=[pl.BlockSpec((tm, d), rmap),
                  pl.BlockSpec((None,) * (nl + 1) + (d, tf), wmap_up),
                  pl.BlockSpec((None,) * (nl + 1) + (d, tf), wmap_up),
                  pl.BlockSpec((None,) * (nl + 1) + (tf, d), wmap_down)],
        out_specs=pl.BlockSpec((tm, d), lambda u, j, ue, un, ub: (u, 0)))
    return pl.pallas_call(
        functools.partial(_ffn_kernel, live_counts=tuple(live_counts), fc=fc), grid_spec=grid_spec,
        out_shape=jax.ShapeDtypeStruct((rows, d), F32),
        compiler_params=_params("arbitrary", "arbitrary"), name="swiglu_ffn",
    )(unit_expert, unit_nsub, unit_block, x, w1, w3, w2)


def _router_kernel(x_ref, g_ref, wr_ref, h_ref, r_ref):
    x = x_ref[...]
    h = x * lax.rsqrt(jnp.mean(x * x, axis=-1, keepdims=True) + EPS) * g_ref[...]
    h_ref[...] = h
    logits = jnp.dot(h, wr_ref[...], preferred_element_type=F32, precision=lax.Precision.HIGHEST)
    lane = lax.broadcasted_iota(jnp.int32, logits.shape, 1)
    neg = jnp.float32(-jnp.inf)
    logits = jnp.where(lane < N_EXPERTS, logits, neg)
    m1 = jnp.max(logits, axis=-1, keepdims=True)
    i1 = jnp.min(jnp.where(logits == m1, lane, LANES), axis=-1, keepdims=True)
    rest = jnp.where(lane == i1, neg, logits)
    m2 = jnp.max(rest, axis=-1, keepdims=True)
    i2 = jnp.min(jnp.where(rest == m2, lane, LANES), axis=-1, keepdims=True)
    e = jnp.exp(m2 - m1)
    g1 = 1.0 / (1.0 + e)
    g2 = e / (1.0 + e)
    r_ref[...] = jnp.where(lane == 0, i1.astype(F32),
                           jnp.where(lane == 1, i2.astype(F32),
                                     jnp.where(lane == 2, g1, jnp.where(lane == 3, g2, 0.0))))


def norm_router(x, gains, layer, w_router_pad, tm=256):
    m, d = x.shape
    row = pl.BlockSpec((tm, d), lambda i: (i, 0))
    return pl.pallas_call(
        _router_kernel, grid=(m // tm,),
        in_specs=[row, pl.BlockSpec((None, 1, d), lambda i: (layer, 0, 0)),
                  pl.BlockSpec((d, LANES), lambda i: (0, 0))],
        out_specs=[row, pl.BlockSpec((tm, LANES), lambda i: (i, 0))],
        out_shape=[jax.ShapeDtypeStruct((m, d), F32), jax.ShapeDtypeStruct((m, LANES), F32)],
        compiler_params=_params("parallel"), name="norm_router")(x, gains, w_router_pad)


DMA_GROUP = 8


def _gather_kernel(src_ref, nv_ref, h_ref, o_ref, buf_ref, sem):
    i = pl.program_id(0)
    tb = buf_ref.shape[0]
    base = i * tb
    nv = nv_ref[i]
    ngroups = nv // DMA_GROUP

    @pl.when(i == 0)
    def _():
        buf_ref[...] = jnp.zeros_like(buf_ref)

    def row_copy(t):
        return pltpu.make_async_copy(h_ref.at[pl.ds(src_ref[base + t], 1)], buf_ref.at[pl.ds(t, 1)], sem)

    def group_copy():
        return pltpu.make_async_copy(h_ref.at[pl.ds(0, DMA_GROUP)], buf_ref.at[pl.ds(0, DMA_GROUP)], sem)

    def issue_group(c, carry):
        for r in range(DMA_GROUP):
            row_copy(c * DMA_GROUP + r).start()
        return carry

    def issue_row(t, carry):
        row_copy(t).start()
        return carry

    def wait_group(c, carry):
        group_copy().wait()
        return carry

    def wait_row(t, carry):
        row_copy(t).wait()
        return carry

    lax.fori_loop(0, ngroups, issue_group, 0)
    lax.fori_loop(ngroups * DMA_GROUP, nv, issue_row, 0)
    lax.fori_loop(0, ngroups, wait_group, 0)
    lax.fori_loop(ngroups * DMA_GROUP, nv, wait_row, 0)
    row = lax.broadcasted_iota(jnp.int32, o_ref.shape, 0)
    o_ref[...] = jnp.where(row < nv, buf_ref[...], 0.0).astype(o_ref.dtype)


def moe_gather(h, src, n_valid, cap, tb):
    _, d = h.shape
    grid_spec = pltpu.PrefetchScalarGridSpec(
        num_scalar_prefetch=2, grid=(cap // tb,),
        in_specs=[pl.BlockSpec(memory_space=pl.ANY)],
        out_specs=pl.BlockSpec((tb, d), lambda i, s, n: (i, 0)),
        scratch_shapes=[pltpu.VMEM((tb, d), h.dtype), pltpu.SemaphoreType.DMA(())])
    return pl.pallas_call(
        _gather_kernel, grid_spec=grid_spec,
        out_shape=jax.ShapeDtypeStruct((cap, d), BF16),
        compiler_params=_params("arbitrary"), name="moe_gather")(src, n_valid, h)


COMBINE_TB = 256


def _combine_kernel(dest_ref, x_ref, r_ref, g_ref, y_ref, xo_ref, h_ref, buf_ref, sem):
    base = pl.program_id(0) * COMBINE_TB

    def row_copy(t, k):
        return pltpu.make_async_copy(y_ref.at[pl.ds(dest_ref[TOP_K * (base + t) + k], 1)],
                                     buf_ref.at[k, pl.ds(t, 1)], sem)

    def issue_group(c, carry):
        for r in range(DMA_GROUP // TOP_K):
            for k in range(TOP_K):
                row_copy(c * (DMA_GROUP // TOP_K) + r, k).start()
        return carry

    lax.fori_loop(0, COMBINE_TB * TOP_K // DMA_GROUP, issue_group, 0)
    for k in range(TOP_K):
        pltpu.make_async_copy(y_ref.at[pl.ds(0, COMBINE_TB)], buf_ref.at[k], sem).wait()
    r = r_ref[...]
    g1 = r[:, 2:3]
    g2 = r[:, 3:4]
    x = x_ref[...] + (buf_ref[0] * g1 + buf_ref[1] * g2)
    xo_ref[...] = x
    y = x * lax.rsqrt(jnp.mean(x * x, axis=-1, keepdims=True) + EPS)
    h_ref[...] = (y * g_ref[...]).astype(h_ref.dtype)


def moe_combine(x, route, dest, ybuf, gains, layer):
    m, d = x.shape
    row = pl.BlockSpec((COMBINE_TB, d), lambda i, dr: (i, 0))
    grid_spec = pltpu.PrefetchScalarGridSpec(
        num_scalar_prefetch=1, grid=(m // COMBINE_TB,),
        in_specs=[row, pl.BlockSpec((COMBINE_TB, LANES), lambda i, dr: (i, 0)),
                  pl.BlockSpec((None, 1, d), lambda i, dr: (layer, 0, 0)),
                  pl.BlockSpec(memory_space=pl.ANY)],
        out_specs=[row, row],
        scratch_shapes=[pltpu.VMEM((TOP_K, COMBINE_TB, d), F32), pltpu.SemaphoreType.DMA(())])
    return pl.pallas_call(
        _combine_kernel, grid_spec=grid_spec,
        out_shape=[jax.ShapeDtypeStruct((m, d), F32), jax.ShapeDtypeStruct((m, d), BF16)],
        compiler_params=_params("arbitrary"), name="moe_combine")(dest, x, route, gains, ybuf)


def moe_layer(x1, ffn_norm, ple_norm, layer, w_router, e_w1, e_w3, e_w2, j):
    m, d = x1.shape
    wr_pad = jnp.zeros((d, LANES), F32).at[:, :N_EXPERTS].set(w_router)
    h, route = norm_router(x1, ffn_norm, layer, wr_pad)
    e_flat = route[:, :TOP_K].astype(jnp.int32).reshape(-1)
    onehot = (e_flat[:, None] == jnp.arange(N_EXPERTS)[None, :]).astype(jnp.int32)
    csum = jnp.cumsum(onehot, axis=0)
    rank = jnp.sum((csum - onehot) * onehot, axis=1)
    counts = csum[-1]
    tm = MOE_TM
    n_units_e = (counts + tm - 1) // tm
    unit_end = jnp.cumsum(n_units_e)
    unit_start = unit_end - n_units_e
    dest = (unit_start[e_flat] * tm + rank).astype(jnp.int32)
    n_units = (m * TOP_K) // tm + N_EXPERTS
    uidx = jnp.arange(n_units)
    ue = jnp.minimum(jnp.searchsorted(unit_end, uidx, side='right'), N_EXPERTS - 1).astype(jnp.int32)
    live_rows = jnp.clip(counts[ue] - (uidx - unit_start[ue]) * tm, 0, tm)
    live_rows = jnp.where(uidx < unit_end[-1], live_rows, 0)
    un = ((live_rows + FFN_SUB - 1) // FFN_SUB).astype(jnp.int32)
    last_live = jnp.maximum(unit_end[-1] - 1, 0)
    ue = jnp.where(uidx < unit_end[-1], ue, ue[last_live]).astype(jnp.int32)
    ub = jnp.where(uidx < unit_end[-1], uidx, last_live).astype(jnp.int32)

    cap = n_units * tm
    src = jnp.zeros((cap,), jnp.int32).at[dest].set(jnp.arange(m * TOP_K, dtype=jnp.int32) // TOP_K)

    xbuf = moe_gather(h, src, live_rows.astype(jnp.int32), cap, tm)
    ybuf = swiglu_ffn(xbuf, ue, un, ub, e_w1, e_w3, e_w2, (j,), range(1, tm // FFN_SUB + 1), tm, MOE_TF,
                      MOE_FC)
    return moe_combine(x1, route, dest, ybuf, ple_norm, layer)


def kernel(x, p, mix_norm, w_in, q_norm, k_norm, ssm_lambda_re, ssm_lambda_im, ssm_log_dt, ssm_b_re, ssm_b_im,
           ssm_c_re, ssm_c_im, ssm_d, ssm_glu_w, gmlp_v_norm, gmlp_ws, gmlp_b, w_branch, w_out, ffn_norm,
           dense_w1, dense_w3, dense_w2, router_w, expert_w1, expert_w3, expert_w2, ple_norm, ple_gate_w,
           ple_proj_w):
    batch, seq, d = x.shape
    depth = w_in.shape[0]
    n_in = w_in.shape[-1]
    m = batch * seq
    xs = x.reshape(m, d)
    rope_c, rope_s = rope_tables(seq)

    def g3(a):
        return a.reshape(a.shape[0], 1, a.shape[1])

    mix_norm, q_norm, k_norm, gmlp_v_norm, ffn_norm, ple_norm = map(
        g3, (mix_norm, q_norm, k_norm, gmlp_v_norm, ffn_norm, ple_norm))
    b_bcast = jnp.broadcast_to(gmlp_b[..., None], gmlp_b.shape + (MLP_GROUP_W,))
    p2 = p.reshape(depth, m, p.shape[-1])
    dense_units = m // DENSE_TM
    dense_un = jnp.full((dense_units,), DENSE_TM // FFN_SUB, jnp.int32)

    for i in range(depth):
        _, h = add_norm(xs, None, mix_norm, i)
        proj = fused_mm(m, n_in, [(h, 0, d, w_in, (i,), 0)], [], lambda dts, ex: dts[0], BF16,
                        tm=min(2048, m), tn=512, name="in_proj")
        attn = attention(proj, q_norm, k_norm, i, rope_c, rope_s, batch, seq)
        tabs = _ssm_tables(ssm_lambda_re[i], ssm_lambda_im[i], ssm_log_dt[i], ssm_b_re[i], ssm_b_im[i],
                           ssm_c_re[i], ssm_c_im[i], ssm_d[i], seq // SSM_SEGS)
        ssm = ssm_branch(proj, i, tabs, ssm_glu_w, batch, seq)
        mlp = gmlp_branch(proj, i, gmlp_v_norm, gmlp_ws, b_bcast, m)

        tn = 512
        merged = fused_mm(
            m, d,
            [(br, 0, br.shape[1], w_branch, (i, n), 0) for n, br in enumerate((attn, ssm, mlp))],
            [(proj, (GATE_OFF + n * d) // tn) for n in range(N_BRANCH)],
            lambda dts, ex: sum(_sigmoid(e.astype(F32)) * dt for e, dt in zip(ex, dts)),
            BF16, tm=1024, tn=tn, name="branch_merge")
        x1 = fused_mm(m, d, [(merged, 0, d, w_out, (i,), 0)], [(xs, 0)],
                      lambda dts, ex: ex[0] + dts[0], F32, tm=1024, tn=512, name="out_proj")

        j = i // 2
        if i % 2 == 0:
            _, h2 = add_norm(x1, None, ffn_norm, i)
            dense_ue = jnp.full((dense_units,), j, jnp.int32)
            y = swiglu_ffn(h2, dense_ue, dense_un, jnp.arange(dense_units, dtype=jnp.int32),
                           dense_w1, dense_w3, dense_w2, (), (DENSE_TM // FFN_SUB,), DENSE_TM, DENSE_TF,
                           DENSE_TF)
            x2, hn = add_norm(x1, y, ple_norm, i)
        else:
            x2, hn = moe_layer(x1, ffn_norm, ple_norm, i, router_w[j], expert_w1, expert_w3, expert_w2, j)

        xs = fused_mm(
            m, d,
            [(hn, 0, d, ple_gate_w, (i,), 0), (p2[i], 0, p.shape[-1], ple_proj_w, (i,), 0)],
            [(x2, 0)],
            lambda dts, ex: ex[0] + _sigmoid(dts[0]) * dts[1], F32, tm=1024, tn=512, name="ple")
    return xs.reshape(batch, seq, d)
```

```python
import functools
import math

import jax
import jax.numpy as jnp
import numpy as np
from jax import lax
from jax.experimental import pallas as pl
from jax.experimental.pallas import tpu as pltpu

GRID_W = 64
ROPE_THETA = 10000.0
HEAD_DIM = 128
N_Q_HEADS = 8
N_KV_HEADS = 2
ATTN_W = N_Q_HEADS * HEAD_DIM
KV_W = N_KV_HEADS * HEAD_DIM
SSM_GROUP = 16
SSM_GROUPS = 64
SSM_W = SSM_GROUP * SSM_GROUPS
SSM_STATE = 64
MLP_CHUNK = 128
MLP_GROUPS = 8
MLP_GROUP_W = 128
MLP_W = MLP_GROUPS * MLP_GROUP_W
N_BRANCH = 3
N_EXPERTS = 8
TOP_K = 2
EPS = 1e-6

Q_OFF = 0
K_OFF = ATTN_W
V_OFF = K_OFF + KV_W
U_OFF = V_OFF + KV_W
ZU_OFF = U_OFF + SSM_W
ZV_OFF = ZU_OFF + MLP_W
GATE_OFF = ZV_OFF + MLP_W

V7X_VMEM_LIMIT_BYTES = 62 * 1024 * 1024
LANES = 128

BF16 = jnp.bfloat16
F32 = jnp.float32


def _params(*sem):
    return pltpu.CompilerParams(dimension_semantics=sem, vmem_limit_bytes=V7X_VMEM_LIMIT_BYTES)


def _gelu(x):
    c = math.sqrt(2.0 / math.pi)
    return 0.5 * x * (1.0 + jnp.tanh(c * (x + 0.044715 * (x * x * x))))


def _sigmoid(x):
    return 1.0 / (1.0 + jnp.exp(-x))


def _silu(x):
    return x * _sigmoid(x)


def _add_norm_kernel(*refs, has_delta):
    if has_delta:
        x_ref, d_ref, g_ref, xo_ref, h_ref = refs
        x = x_ref[...] + d_ref[...]
        xo_ref[...] = x
    else:
        x_ref, g_ref, h_ref = refs
        x = x_ref[...]
    y = x * lax.rsqrt(jnp.mean(x * x, axis=-1, keepdims=True) + EPS)
    h_ref[...] = (y * g_ref[...]).astype(h_ref.dtype)


def add_norm(x, delta, gains, layer, tm=512):
    m, d = x.shape
    row = pl.BlockSpec((tm, d), lambda i: (i, 0))
    gspec = pl.BlockSpec((None, 1, d), lambda i: (layer, 0, 0))
    if delta is None:
        h = pl.pallas_call(
            functools.partial(_add_norm_kernel, has_delta=False),
            grid=(m // tm,), in_specs=[row, gspec], out_specs=row,
            out_shape=jax.ShapeDtypeStruct((m, d), BF16),
            compiler_params=_params("parallel"), name="norm")(x, gains)
        return x, h
    xo, h = pl.pallas_call(
        functools.partial(_add_norm_kernel, has_delta=True),
        grid=(m // tm,), in_specs=[row, row, gspec], out_specs=[row, row],
        out_shape=[jax.ShapeDtypeStruct((m, d), F32), jax.ShapeDtypeStruct((m, d), BF16)],
        compiler_params=_params("parallel"), name="add_norm")(x, delta, gains)
    return xo, h


def _fused_mm_kernel(*refs, n_dots, n_extras, epilogue):
    a_refs = refs[:n_dots]
    w_refs = refs[n_dots:2 * n_dots]
    e_refs = refs[2 * n_dots:2 * n_dots + n_extras]
    o_ref = refs[2 * n_dots + n_extras]
    wb_refs = refs[2 * n_dots + n_extras + 1:]

    @pl.when(pl.program_id(1) == 0)
    def _():
        for w_ref, wb_ref in zip(w_refs, wb_refs):
            wb_ref[...] = w_ref[...].astype(BF16)

    dots = []
    for a_ref, wb_ref in zip(a_refs, wb_refs):
        a = a_ref[...]
        if a.dtype != BF16:
            a = a.astype(BF16)
        dots.append(jnp.dot(a, wb_ref[...], preferred_element_type=F32))
    extras = [e_ref[...] for e_ref in e_refs]
    o_ref[...] = epilogue(dots, extras).astype(o_ref.dtype)


def fused_mm(m, n, dots, extras, epilogue, out_dtype, tm, tn, name):
    in_specs, args, scratch = [], [], []
    for a, acb, k, _, _, _ in dots:
        in_specs.append(pl.BlockSpec((tm, k), lambda j, i, acb=acb: (i, acb)))
        args.append(a)
    for _, _, k, w, lead, off in dots:
        in_specs.append(pl.BlockSpec((None,) * len(lead) + (k, tn),
                                     lambda j, i, lead=tuple(lead), off=off: lead + (0, off + j)))
        args.append(w)
        scratch.append(pltpu.VMEM((k, tn), BF16))
    for e, off in extras:
        in_specs.append(pl.BlockSpec((tm, tn), lambda j, i, off=off: (i, off + j)))
        args.append(e)
    return pl.pallas_call(
        functools.partial(_fused_mm_kernel, n_dots=len(dots), n_extras=len(extras), epilogue=epilogue),
        grid=(n // tn, m // tm), in_specs=in_specs,
        out_specs=pl.BlockSpec((tm, tn), lambda j, i: (i, j)),
        out_shape=jax.ShapeDtypeStruct((m, n), out_dtype),
        scratch_shapes=scratch,
        compiler_params=_params("arbitrary", "arbitrary"), name=name)(*args)


def _rope(x, c, s):
    lane = lax.broadcasted_iota(jnp.int32, x.shape, x.ndim - 1)
    quarter = HEAD_DIM // 4
    partner = jnp.where((lane % (2 * quarter)) < quarter,
                        pltpu.roll(x, HEAD_DIM - quarter, x.ndim - 1),
                        pltpu.roll(x, quarter, x.ndim - 1))
    return x * c + partner * s


def _head_norm(x, g):
    return x * lax.rsqrt(jnp.mean(x * x, axis=-1, keepdims=True) + EPS) * g


def _attn_kernel(q_ref, k_ref, v_ref, cq_ref, sq_ref, ck_ref, sk_ref, qg_ref, kg_ref, o_ref, ks_ref, vs_ref, *,
                 rep):
    @pl.when(pl.program_id(2) == 0)
    def _():
        k = _head_norm(k_ref[...].astype(F32), kg_ref[...])
        ks_ref[...] = _rope(k, ck_ref[...], sk_ref[...]).astype(BF16)
        vs_ref[:, 0:HEAD_DIM] = v_ref[...]
        vs_ref[:, HEAD_DIM:2 * HEAD_DIM] = jnp.ones((v_ref.shape[0], HEAD_DIM), BF16)

    scale = HEAD_DIM ** -0.5 * math.log2(math.e)
    cq = cq_ref[...]
    sq = sq_ref[...]
    for hh in range(rep):
        sl = slice(hh * HEAD_DIM, (hh + 1) * HEAD_DIM)
        q = _head_norm(q_ref[:, sl].astype(F32), qg_ref[...])
        q = (_rope(q, cq, sq) * scale).astype(BF16)
        s = lax.dot_general(q, ks_ref[...], (((1,), (1,)), ((), ())), preferred_element_type=F32)
        m = jnp.max(s, axis=-1, keepdims=True)
        p = jnp.exp2(s - m).astype(BF16)
        o = jnp.dot(p, vs_ref[...], preferred_element_type=F32)
        o_ref[:, sl] = (o[:, 0:HEAD_DIM] / o[:, HEAD_DIM:HEAD_DIM + 1]).astype(o_ref.dtype)


def attention(proj, q_gain, k_gain, layer, rope_c, rope_s, batch, seq, tq=512):
    rep = N_Q_HEADS // N_KV_HEADS
    qw = rep * HEAD_DIM
    nq = seq // tq
    gspec = pl.BlockSpec((None, 1, HEAD_DIM), lambda b, g, i: (layer, 0, 0))
    return pl.pallas_call(
        functools.partial(_attn_kernel, rep=rep),
        grid=(batch, N_KV_HEADS, nq),
        in_specs=[
            pl.BlockSpec((tq, qw), lambda b, g, i: (b * nq + i, Q_OFF // qw + g)),
            pl.BlockSpec((seq, HEAD_DIM), lambda b, g, i: (b, K_OFF // HEAD_DIM + g)),
            pl.BlockSpec((seq, HEAD_DIM), lambda b, g, i: (b, V_OFF // HEAD_DIM + g)),
            pl.BlockSpec((tq, HEAD_DIM), lambda b, g, i: (i, 0)),
            pl.BlockSpec((tq, HEAD_DIM), lambda b, g, i: (i, 0)),
            pl.BlockSpec((seq, HEAD_DIM), lambda b, g, i: (0, 0)),
            pl.BlockSpec((seq, HEAD_DIM), lambda b, g, i: (0, 0)),
            gspec, gspec,
        ],
        out_specs=pl.BlockSpec((tq, qw), lambda b, g, i: (b * nq + i, g)),
        out_shape=jax.ShapeDtypeStruct((batch * seq, ATTN_W), BF16),
        scratch_shapes=[pltpu.VMEM((seq, HEAD_DIM), BF16), pltpu.VMEM((seq, 2 * HEAD_DIM), BF16)],
        compiler_params=_params("arbitrary", "arbitrary", "arbitrary"), name="attention",
    )(proj, proj, proj, rope_c, rope_s, rope_c, rope_s, q_gain, k_gain)


def rope_tables(seq):
    rows = seq // GRID_W
    t = jnp.arange(seq)
    pos = jnp.stack([t // GRID_W - rows // 2, t % GRID_W - GRID_W // 2], axis=-1).astype(F32)
    n_freq = HEAD_DIM // 4
    inv_freq = ROPE_THETA ** (-jnp.arange(n_freq, dtype=F32) / n_freq)
    ang = pos[:, :, None] * inv_freq
    cos, sin = jnp.cos(ang), jnp.sin(ang)
    c = jnp.concatenate([cos[:, 0], cos[:, 0], cos[:, 1], cos[:, 1]], axis=-1)
    s = jnp.concatenate([-sin[:, 0], sin[:, 0], -sin[:, 1], sin[:, 1]], axis=-1)
    return c, s


SSM_SEGS = 8
SSM_BLK = 16
SSM_PG = 2
SSM_PW = SSM_PG * SSM_STATE
SSM_ROW = SSM_BLK * SSM_PG * SSM_GROUP


def _ssm_kernel(u_ref, pw_ref, bt_ref, ct_ref, h_ref, at_ref, d_ref, o_ref, z_ref, s_ref, e_ref, t_ref, bw_ref,
                vt_ref, *, n_seq, n_blocks):
    ns = n_seq * SSM_SEGS
    pw = SSM_PW
    xw = SSM_PG * SSM_GROUP

    def part(k):
        return slice(k * pw, (k + 1) * pw)

    lane = lax.broadcasted_iota(jnp.int32, (SSM_GROUP, pw), 1)
    owns = [lane // SSM_STATE == gp for gp in range(SSM_PG)]

    def pair_lanes(ref):
        return jnp.concatenate([ref[gp] for gp in range(SSM_PG)], axis=-1)

    apow, bbar, cmat, aseg = pair_lanes(pw_ref), pair_lanes(bt_ref), pair_lanes(ct_ref), pair_lanes(at_ref)

    def build(dst_ref, m, d, step, k, neg_im):
        ar, ai = apow[2 * d, k:k + 1, :], apow[2 * d + 1, k:k + 1, :]
        mr, mi = m[2 * d], m[2 * d + 1]
        xr, xi = ar * mr - ai * mi, ar * mi + ai * mr
        if neg_im:
            xi = -xi
        for gp, own in enumerate(owns):
            rws = slice(step * xw + gp * SSM_GROUP, step * xw + (gp + 1) * SSM_GROUP)
            dst_ref[rws, part(2 * d)] = jnp.where(own, xr, 0.0).astype(BF16)
            dst_ref[rws, part(2 * d + 1)] = jnp.where(own, xi, 0.0).astype(BF16)

    for step in range(SSM_BLK):
        build(bw_ref, bbar, 0, step, SSM_BLK - 1 - step, False)
        build(bw_ref, bbar, 1, step, step, False)
        build(vt_ref, cmat, 0, step, step + 1, True)
        build(vt_ref, cmat, 1, step, SSM_BLK - step, True)

    u = u_ref[...]
    z_ref[...] = jnp.dot(u, bw_ref[...], preferred_element_type=F32)
    afr, afi, abr, abi = [jnp.broadcast_to(apow[k, SSM_BLK:SSM_BLK + 1, :], (ns, pw)) for k in range(4)]

    def rows(blk):
        return slice(blk * ns, (blk + 1) * ns)

    def scan(state, keep):
        fr, fi, br, bi = state
        for i in range(n_blocks):
            mf, mb = i, n_blocks - 1 - i
            if keep:
                s_ref[rows(mf), part(0)] = fr.astype(BF16)
                s_ref[rows(mf), part(1)] = fi.astype(BF16)
                s_ref[rows(mb), part(2)] = br.astype(BF16)
                s_ref[rows(mb), part(3)] = bi.astype(BF16)
            zfr, zfi = z_ref[rows(mf), part(0)], z_ref[rows(mf), part(1)]
            zbr, zbi = z_ref[rows(mb), part(2)], z_ref[rows(mb), part(3)]
            fr, fi = afr * fr - afi * fi + zfr, afr * fi + afi * fr + zfi
            br, bi = abr * br - abi * bi + zbr, abr * bi + abi * br + zbi
        return fr, fi, br, bi

    zero = jnp.zeros((ns, pw), F32)
    ends = scan((zero, zero, zero, zero), keep=False)
    for k in range(4):
        e_ref[:, part(k)] = ends[k]
    row0 = jnp.zeros((1, pw), F32)
    for b in range(n_seq):
        for kr, ki, order in ((0, 1, range(SSM_SEGS)), (2, 3, range(SSM_SEGS - 1, -1, -1))):
            tr, ti = aseg[kr], aseg[ki]
            cr, ci = row0, row0
            for q in order:
                row = slice(b * SSM_SEGS + q, b * SSM_SEGS + q + 1)
                er, ei = e_ref[row, part(kr)], e_ref[row, part(ki)]
                e_ref[row, part(kr)] = cr
                e_ref[row, part(ki)] = ci
                cr, ci = tr * cr - ti * ci + er, tr * ci + ti * cr + ei
    scan(tuple(e_ref[:, part(k)] for k in range(4)), keep=True)

    h = h_ref[...]
    for rp in range(SSM_BLK):
        off = xw * (SSM_BLK - 1 - rp)
        t_ref[rp * xw:(rp + 1) * xw, :] = h[:, off:off + SSM_ROW].astype(BF16)

    y = jnp.dot(u, t_ref[...], preferred_element_type=F32)
    y = y + lax.dot_general(s_ref[...], vt_ref[...], (((1,), (1,)), ((), ())), preferred_element_type=F32)
    y = y + d_ref[...] * u.astype(F32)
    o_ref[...] = _gelu(y).astype(o_ref.dtype)


def _ssm_tables(lam_re, lam_im, log_dt, b_re, b_im, c_re, c_im, d_skip, seg_len):
    g, c, r, pg = SSM_GROUPS, SSM_GROUP, SSM_BLK, SSM_PG
    npair = g // pg
    lre, lim = lam_re.astype(F32), lam_im.astype(F32)
    dt = jnp.exp(log_dt.astype(F32))[..., None]

    def cexp(xr, xi):
        e = jnp.exp(xr)
        return e * jnp.cos(xi), e * jnp.sin(xi)

    def cmul(ar, ai, br, bi):
        return ar * br - ai * bi, ar * bi + ai * br

    steps = jnp.arange(r + 1, dtype=F32)[None, :, None, None]
    pr, pi = cexp((lre * dt)[:, None] * steps, (lim * dt)[:, None] * steps)
    den = lre * lre + lim * lim
    qr = ((pr[:, 1] - 1.0) * lre + pi[:, 1] * lim) / den
    qi = (pi[:, 1] * lre - (pr[:, 1] - 1.0) * lim) / den
    bbr, bbi = cmul(qr[..., None], qi[..., None], b_re.astype(F32), b_im.astype(F32))
    ccr, cci = c_re.astype(F32), c_im.astype(F32)

    def per_group(xr, xi):
        y = jnp.stack([xr, xi], axis=1)
        return jnp.transpose(y, (3, 0, 1, 2, 4)).reshape(g, 4, xr.shape[1], xr.shape[3])

    pw4 = per_group(pr, pi)
    bt4 = per_group(jnp.transpose(bbr, (0, 3, 1, 2)), jnp.transpose(bbi, (0, 3, 1, 2)))
    ct4 = per_group(jnp.transpose(ccr, (0, 2, 1, 3)), jnp.transpose(cci, (0, 2, 1, 3)))

    xr, xi = cmul(ccr[:, :, None], cci[:, :, None], jnp.transpose(pr[:, :r], (0, 2, 1, 3))[:, :, :, None],
                  jnp.transpose(pi[:, :r], (0, 2, 1, 3))[:, :, :, None])
    klag = jnp.einsum('zgkcq,zgqd->zgkdc', jnp.concatenate([xr, -xi], axis=-1),
                      jnp.concatenate([bbr, bbi], axis=2))
    kf, kb = klag[0], klag[1]
    lagged = jnp.concatenate([kb[:, :0:-1], (kf[:, 0] + kb[:, 0])[:, None], kf[:, 1:]], axis=1)
    lagged = lagged.reshape(npair, pg, 2 * r - 1, c, c)
    h = jnp.einsum('aindc,ij->aidnjc', lagged, jnp.eye(pg, dtype=F32)).reshape(npair, pg * c, 2 * r - 1, pg * c)
    h = jnp.pad(h, ((0, 0), (0, 0), (0, 1), (0, 0))).reshape(npair, pg * c, 2 * r * pg * c)

    sr, si = cexp(lre * dt * seg_len, lim * dt * seg_len)
    a_seg = per_group(sr[:, None], si[:, None])
    d_row = jnp.broadcast_to(d_skip.astype(F32).reshape(npair, 1, pg * c), (npair, r, pg * c))
    return pw4, bt4, ct4, h, a_seg, d_row.reshape(npair, 1, SSM_ROW)


def ssm_branch(proj, layer, tabs, w_glu, batch, seq):
    pw4, bt4, ct4, h, a_seg, d_row = tabs
    seg_len = seq // SSM_SEGS
    n_blocks = seg_len // SSM_BLK
    ns = batch * SSM_SEGS
    m = batch * seq
    npair = SSM_GROUPS // SSM_PG
    pc = SSM_PG * SSM_GROUP
    rows = n_blocks * ns
    u = proj[:, U_OFF:U_OFF + SSM_W].reshape(batch, SSM_SEGS, n_blocks, SSM_BLK, npair, pc)
    u = jnp.transpose(u, (4, 2, 0, 1, 3, 5)).reshape(npair, rows, SSM_ROW)
    def per_pair(a):
        return pl.BlockSpec((None,) + a.shape[1:], lambda i, nd=a.ndim: (i,) + (0,) * (nd - 1))

    operator = pltpu.VMEM((SSM_ROW, 4 * SSM_PW), BF16)
    y = pl.pallas_call(
        functools.partial(_ssm_kernel, n_seq=batch, n_blocks=n_blocks),
        grid=(npair,),
        in_specs=[per_pair(u)] + [pl.BlockSpec((SSM_PG,) + a.shape[1:], lambda i: (i, 0, 0, 0))
                                  for a in (pw4, bt4, ct4)] + [per_pair(h)]
        + [pl.BlockSpec((SSM_PG,) + a_seg.shape[1:], lambda i: (i, 0, 0, 0)), per_pair(d_row)],
        out_specs=per_pair(u),
        out_shape=jax.ShapeDtypeStruct((npair, rows, SSM_ROW), BF16),
        scratch_shapes=[pltpu.VMEM((rows, 4 * SSM_PW), F32), pltpu.VMEM((rows, 4 * SSM_PW), BF16),
                        pltpu.VMEM((ns, 4 * SSM_PW), F32), pltpu.VMEM((SSM_ROW, SSM_ROW), BF16),
                        operator, operator],
        compiler_params=_params("parallel"), name="ssm")(u, pw4, bt4, ct4, h, a_seg, d_row)
    y = y.reshape(npair, n_blocks, batch, SSM_SEGS, SSM_BLK, pc)
    y = jnp.transpose(y, (2, 3, 1, 4, 0, 5)).reshape(m, SSM_W)
    return fused_mm(m, SSM_W, [(y, 0, SSM_W, w_glu, (layer,), 0)], [(y, 0)],
                    lambda dts, ex: ex[0].astype(F32) * _sigmoid(dts[0]), BF16, tm=1024, tn=512, name="ssm_glu")


GMLP_NC = 4


def _gmlp_kernel(zu0_ref, zu1_ref, zv0_ref, zv1_ref, g_ref, ws_ref, bb_ref, o_ref):
    half = MLP_W // 2
    zv = jnp.concatenate([zv0_ref[...], zv1_ref[...]], axis=-1).astype(F32)
    v = _gelu(zv)
    v = (v * lax.rsqrt(jnp.mean(v * v, axis=-1, keepdims=True) + EPS) * g_ref[...]).astype(BF16)
    for g in range(MLP_GROUPS):
        cs = slice(g * MLP_GROUP_W, (g + 1) * MLP_GROUP_W)
        vg = jnp.concatenate([v[n * MLP_CHUNK:(n + 1) * MLP_CHUNK, cs] for n in range(GMLP_NC)], axis=-1)
        s = jnp.dot(ws_ref[g].astype(BF16), vg, preferred_element_type=F32)
        zu_ref = zu0_ref if g * MLP_GROUP_W < half else zu1_ref
        us = slice((g * MLP_GROUP_W) % half, (g * MLP_GROUP_W) % half + MLP_GROUP_W)
        for n in range(GMLP_NC):
            rs = slice(n * MLP_CHUNK, (n + 1) * MLP_CHUNK)
            sn = s[:, n * MLP_GROUP_W:(n + 1) * MLP_GROUP_W] + bb_ref[g]
            o_ref[rs, cs] = (_gelu(zu_ref[rs, us].astype(F32)) * sn).astype(o_ref.dtype)


def gmlp_branch(proj, layer, v_gain, w_s, b_bcast, m):
    rows = GMLP_NC * MLP_CHUNK
    half = MLP_W // 2

    def zspec(off):
        return pl.BlockSpec((rows, half), lambda i, off=off: (i, off // half))

    return pl.pallas_call(
        _gmlp_kernel, grid=(m // rows,),
        in_specs=[zspec(ZU_OFF), zspec(ZU_OFF + half), zspec(ZV_OFF), zspec(ZV_OFF + half),
                  pl.BlockSpec((None, 1, MLP_W), lambda i: (layer, 0, 0)),
                  pl.BlockSpec((None, MLP_GROUPS, MLP_CHUNK, MLP_CHUNK), lambda i: (layer, 0, 0, 0)),
                  pl.BlockSpec((None, MLP_GROUPS, MLP_CHUNK, MLP_GROUP_W), lambda i: (layer, 0, 0, 0))],
        out_specs=pl.BlockSpec((rows, MLP_W), lambda i: (i, 0)),
        out_shape=jax.ShapeDtypeStruct((m, MLP_W), BF16),
        compiler_params=_params("parallel"), name="gmlp",
    )(proj, proj, proj, proj, v_gain, w_s, b_bcast)


DENSE_TM = 1024
DENSE_TF = 512
MOE_TM = 1280
MOE_TF = 512
MOE_FC = 256
FFN_SUB = 256


def _ffn_kernel(ue_ref, un_ref, ub_ref, x_ref, w1_ref, w3_ref, w2_ref, o_ref, *, live_counts, fc):
    del ue_ref, ub_ref
    u = pl.program_id(0)
    j = pl.program_id(1)
    nsub = un_ref[u]

    @pl.when(j == 0)
    def _():
        o_ref[...] = jnp.zeros_like(o_ref)

    tf = w2_ref.shape[0]
    for k in live_counts:
        @pl.when(nsub == k)
        def _(k=k):
            rows = k * FFN_SUB
            x = x_ref[0:rows, :]
            for f0 in range(0, tf, fc):
                fs = slice(f0, f0 + fc)
                h1 = jnp.dot(x, w1_ref[:, fs].astype(BF16), preferred_element_type=F32)
                h3 = jnp.dot(x, w3_ref[:, fs].astype(BF16), preferred_element_type=F32)
                act = (_silu(h1) * h3).astype(BF16)
                o_ref[0:rows, :] += jnp.dot(act, w2_ref[fs, :].astype(BF16), preferred_element_type=F32)


def swiglu_ffn(x, unit_expert, unit_nsub, unit_block, w1, w3, w2, lead, live_counts, tm, tf, fc):
    rows, d = x.shape
    f = w1.shape[-1]
    n_units = rows // tm
    nf = f // tf
    nl = len(lead)

    def wmap_up(u, j, ue, un, ub):
        return tuple(lead) + (ue[u], 0, jnp.where(un[u] > 0, j, nf - 1))

    def wmap_down(u, j, ue, un, ub):
        return tuple(lead) + (ue[u], jnp.where(un[u] > 0, j, nf - 1), 0)

    def rmap(u, j, ue, un, ub):
        return (ub[u], 0)

    grid_spec = pltpu.PrefetchScalarGridSpec(
        num_scalar_prefetch=3, grid=(n_units, nf),
        in_specs=[pl.BlockSpec((tm, d), rmap),
                  pl.BlockSpec((None,) * (nl + 1) + (d, tf), wmap_up),
                  pl.BlockSpec((None,) * (nl + 1) + (d, tf), wmap_up),
                  pl.BlockSpec((None,) * (nl + 1) + (tf, d), wmap_down)],
        out_specs=pl.BlockSpec((tm, d), lambda u, j, ue, un, ub: (u, 0)))
    return pl.pallas_call(
        functools.partial(_ffn_kernel, live_counts=tuple(live_counts), fc=fc), grid_spec=grid_spec,
        out_shape=jax.ShapeDtypeStruct((rows, d), F32),
        compiler_params=_params("arbitrary", "arbitrary"), name="swiglu_ffn",
    )(unit_expert, unit_nsub, unit_block, x, w1, w3, w2)


def _router_kernel(x_ref, g_ref, wr_ref, h_ref, r_ref):
    x = x_ref[...]
    h = x * lax.rsqrt(jnp.mean(x * x, axis=-1, keepdims=True) + EPS) * g_ref[...]
    h_ref[...] = h
    logits = jnp.dot(h, wr_ref[...], preferred_element_type=F32, precision=lax.Precision.HIGHEST)
    lane = lax.broadcasted_iota(jnp.int32, logits.shape, 1)
    neg = jnp.float32(-jnp.inf)
    logits = jnp.where(lane < N_EXPERTS, logits, neg)
    m1 = jnp.max(logits, axis=-1, keepdims=True)
    i1 = jnp.min(jnp.where(logits == m1, lane, LANES), axis=-1, keepdims=True)
    rest = jnp.where(lane == i1, neg, logits)
    m2 = jnp.max(rest, axis=-1, keepdims=True)
    i2 = jnp.min(jnp.where(rest == m2, lane, LANES), axis=-1, keepdims=True)
    e = jnp.exp(m2 - m1)
    g1 = 1.0 / (1.0 + e)
    g2 = e / (1.0 + e)
    r_ref[...] = jnp.where(lane == 0, i1.astype(F32),
                           jnp.where(lane == 1, i2.astype(F32),
                                     jnp.where(lane == 2, g1, jnp.where(lane == 3, g2, 0.0))))


def norm_router(x, gains, layer, w_router_pad, tm=256):
    m, d = x.shape
    row = pl.BlockSpec((tm, d), lambda i: (i, 0))
    return pl.pallas_call(
        _router_kernel, grid=(m // tm,),
        in_specs=[row, pl.BlockSpec((None, 1, d), lambda i: (layer, 0, 0)),
                  pl.BlockSpec((d, LANES), lambda i: (0, 0))],
        out_specs=[row, pl.BlockSpec((tm, LANES), lambda i: (i, 0))],
        out_shape=[jax.ShapeDtypeStruct((m, d), F32), jax.ShapeDtypeStruct((m, LANES), F32)],
        compiler_params=_params("parallel"), name="norm_router")(x, gains, w_router_pad)


DMA_GROUP = 8


def _gather_kernel(src_ref, nv_ref, h_ref, o_ref, buf_ref, sem):
    i = pl.program_id(0)
    tb = buf_ref.shape[0]
    base = i * tb
    nv = nv_ref[i]
    ngroups = nv // DMA_GROUP

    @pl.when(i == 0)
    def _():
        buf_ref[...] = jnp.zeros_like(buf_ref)

    def row_copy(t):
        return pltpu.make_async_copy(h_ref.at[pl.ds(src_ref[base + t], 1)], buf_ref.at[pl.ds(t, 1)], sem)

    def group_copy():
        return pltpu.make_async_copy(h_ref.at[pl.ds(0, DMA_GROUP)], buf_ref.at[pl.ds(0, DMA_GROUP)], sem)

    def issue_group(c, carry):
        for r in range(DMA_GROUP):
            row_copy(c * DMA_GROUP + r).start()
        return carry

    def issue_row(t, carry):
        row_copy(t).start()
        return carry

    def wait_group(c, carry):
        group_copy().wait()
        return carry

    def wait_row(t, carry):
        row_copy(t).wait()
        return carry

    lax.fori_loop(0, ngroups, issue_group, 0)
    lax.fori_loop(ngroups * DMA_GROUP, nv, issue_row, 0)
    lax.fori_loop(0, ngroups, wait_group, 0)
    lax.fori_loop(ngroups * DMA_GROUP, nv, wait_row, 0)
    row = lax.broadcasted_iota(jnp.int32, o_ref.shape, 0)
    o_ref[...] = jnp.where(row < nv, buf_ref[...], 0.0).astype(o_ref.dtype)


def moe_gather(h, src, n_valid, cap, tb):
    _, d = h.shape
    grid_spec = pltpu.PrefetchScalarGridSpec(
        num_scalar_prefetch=2, grid=(cap // tb,),
        in_specs=[pl.BlockSpec(memory_space=pl.ANY)],
        out_specs=pl.BlockSpec((tb, d), lambda i, s, n: (i, 0)),
        scratch_shapes=[pltpu.VMEM((tb, d), h.dtype), pltpu.SemaphoreType.DMA(())])
    return pl.pallas_call(
        _gather_kernel, grid_spec=grid_spec,
        out_shape=jax.ShapeDtypeStruct((cap, d), BF16),
        compiler_params=_params("arbitrary"), name="moe_gather")(src, n_valid, h)


COMBINE_TB = 256


def _combine_kernel(dest_ref, x_ref, r_ref, g_ref, y_ref, xo_ref, h_ref, buf_ref, sem):
    base = pl.program_id(0) * COMBINE_TB

    def row_copy(t, k):
        return pltpu.make_async_copy(y_ref.at[pl.ds(dest_ref[TOP_K * (base + t) + k], 1)],
                                     buf_ref.at[k, pl.ds(t, 1)], sem)

    def issue_group(c, carry):
        for r in range(DMA_GROUP // TOP_K):
            for k in range(TOP_K):
                row_copy(c * (DMA_GROUP // TOP_K) + r, k).start()
        return carry

    lax.fori_loop(0, COMBINE_TB * TOP_K // DMA_GROUP, issue_group, 0)
    for k in range(TOP_K):
        pltpu.make_async_copy(y_ref.at[pl.ds(0, COMBINE_TB)], buf_ref.at[k], sem).wait()
    r = r_ref[...]
    g1 = r[:, 2:3]
    g2 = r[:, 3:4]
    x = x_ref[...] + (buf_ref[0] * g1 + buf_ref[1] * g2)
    xo_ref[...] = x
    y = x * lax.rsqrt(jnp.mean(x * x, axis=-1, keepdims=True) + EPS)
    h_ref[...] = (y * g_ref[...]).astype(h_ref.dtype)


def moe_combine(x, route, dest, ybuf, gains, layer):
    m, d = x.shape
    row = pl.BlockSpec((COMBINE_TB, d), lambda i, dr: (i, 0))
    grid_spec = pltpu.PrefetchScalarGridSpec(
        num_scalar_prefetch=1, grid=(m // COMBINE_TB,),
        in_specs=[row, pl.BlockSpec((COMBINE_TB, LANES), lambda i, dr: (i, 0)),
                  pl.BlockSpec((None, 1, d), lambda i, dr: (layer, 0, 0)),
                  pl.BlockSpec(memory_space=pl.ANY)],
        out_specs=[row, row],
        scratch_shapes=[pltpu.VMEM((TOP_K, COMBINE_TB, d), F32), pltpu.SemaphoreType.DMA(())])
    return pl.pallas_call(
        _combine_kernel, grid_spec=grid_spec,
        out_shape=[jax.ShapeDtypeStruct((m, d), F32), jax.ShapeDtypeStruct((m, d), BF16)],
        compiler_params=_params("arbitrary"), name="moe_combine")(dest, x, route, gains, ybuf)


def moe_layer(x1, ffn_norm, ple_norm, layer, w_router, e_w1, e_w3, e_w2, j):
    m, d = x1.shape
    wr_pad = jnp.zeros((d, LANES), F32).at[:, :N_EXPERTS].set(w_router)
    h, route = norm_router(x1, ffn_norm, layer, wr_pad)
    e_flat = route[:, :TOP_K].astype(jnp.int32).reshape(-1)
    onehot = (e_flat[:, None] == jnp.arange(N_EXPERTS)[None, :]).astype(jnp.int32)
    csum = jnp.cumsum(onehot, axis=0)
    rank = jnp.sum((csum - onehot) * onehot, axis=1)
    counts = csum[-1]
    tm = MOE_TM
    n_units_e = (counts + tm - 1) // tm
    unit_end = jnp.cumsum(n_units_e)
    unit_start = unit_end - n_units_e
    dest = (unit_start[e_flat] * tm + rank).astype(jnp.int32)
    n_units = (m * TOP_K) // tm + N_EXPERTS
    uidx = jnp.arange(n_units)
    ue = jnp.minimum(jnp.searchsorted(unit_end, uidx, side='right'), N_EXPERTS - 1).astype(jnp.int32)
    live_rows = jnp.clip(counts[ue] - (uidx - unit_start[ue]) * tm, 0, tm)
    live_rows = jnp.where(uidx < unit_end[-1], live_rows, 0)
    un = ((live_rows + FFN_SUB - 1) // FFN_SUB).astype(jnp.int32)
    last_live = jnp.maximum(unit_end[-1] - 1, 0)
    ue = jnp.where(uidx < unit_end[-1], ue, ue[last_live]).astype(jnp.int32)
    ub = jnp.where(uidx < unit_end[-1], uidx, last_live).astype(jnp.int32)

    cap = n_units * tm
    src = jnp.zeros((cap,), jnp.int32).at[dest].set(jnp.arange(m * TOP_K, dtype=jnp.int32) // TOP_K)

    xbuf = moe_gather(h, src, live_rows.astype(jnp.int32), cap, tm)
    ybuf = swiglu_ffn(xbuf, ue, un, ub, e_w1, e_w3, e_w2, (j,), range(1, tm // FFN_SUB + 1), tm, MOE_TF,
                      MOE_FC)
    return moe_combine(x1, route, dest, ybuf, ple_norm, layer)


def kernel(x, p, mix_norm, w_in, q_norm, k_norm, ssm_lambda_re, ssm_lambda_im, ssm_log_dt, ssm_b_re, ssm_b_im,
           ssm_c_re, ssm_c_im, ssm_d, ssm_glu_w, gmlp_v_norm, gmlp_ws, gmlp_b, w_branch, w_out, ffn_norm,
           dense_w1, dense_w3, dense_w2, router_w, expert_w1, expert_w3, expert_w2, ple_norm, ple_gate_w,
           ple_proj_w):
    batch, seq, d = x.shape
    depth = w_in.shape[0]
    n_in = w_in.shape[-1]
    m = batch * seq
    xs = x.reshape(m, d)
    rope_c, rope_s = rope_tables(seq)

    def g3(a):
        return a.reshape(a.shape[0], 1, a.shape[1])

    mix_norm, q_norm, k_norm, gmlp_v_norm, ffn_norm, ple_norm = map(
        g3, (mix_norm, q_norm, k_norm, gmlp_v_norm, ffn_norm, ple_norm))
    b_bcast = jnp.broadcast_to(gmlp_b[..., None], gmlp_b.shape + (MLP_GROUP_W,))
    p2 = p.reshape(depth, m, p.shape[-1])
    dense_units = m // DENSE_TM
    dense_un = jnp.full((dense_units,), DENSE_TM // FFN_SUB, jnp.int32)

    for i in range(depth):
        _, h = add_norm(xs, None, mix_norm, i)
        proj = fused_mm(m, n_in, [(h, 0, d, w_in, (i,), 0)], [], lambda dts, ex: dts[0], BF16,
                        tm=min(2048, m), tn=512, name="in_proj")
        attn = attention(proj, q_norm, k_norm, i, rope_c, rope_s, batch, seq)
        tabs = _ssm_tables(ssm_lambda_re[i], ssm_lambda_im[i], ssm_log_dt[i], ssm_b_re[i], ssm_b_im[i],
                           ssm_c_re[i], ssm_c_im[i], ssm_d[i], seq // SSM_SEGS)
        ssm = ssm_branch(proj, i, tabs, ssm_glu_w, batch, seq)
        mlp = gmlp_branch(proj, i, gmlp_v_norm, gmlp_ws, b_bcast, m)

        tn = 512
        merged = fused_mm(
            m, d,
            [(br, 0, br.shape[1], w_branch, (i, n), 0) for n, br in enumerate((attn, ssm, mlp))],
            [(proj, (GATE_OFF + n * d) // tn) for n in range(N_BRANCH)],
            lambda dts, ex: sum(_sigmoid(e.astype(F32)) * dt for e, dt in zip(ex, dts)),
            BF16, tm=1024, tn=tn, name="branch_merge")
        x1 = fused_mm(m, d, [(merged, 0, d, w_out, (i,), 0)], [(xs, 0)],
                      lambda dts, ex: ex[0] + dts[0], F32, tm=min(2048, m), tn=512, name="out_proj")

        j = i // 2
        if i % 2 == 0:
            _, h2 = add_norm(x1, None, ffn_norm, i)
            dense_ue = jnp.full((dense_units,), j, jnp.int32)
            y = swiglu_ffn(h2, dense_ue, dense_un, jnp.arange(dense_units, dtype=jnp.int32),
                           dense_w1, dense_w3, dense_w2, (), (DENSE_TM // FFN_SUB,), DENSE_TM, DENSE_TF,
                           DENSE_TF)
            x2, hn = add_norm(x1, y, ple_norm, i)
        else:
            x2, hn = moe_layer(x1, ffn_norm, ple_norm, i, router_w[j], expert_w1, expert_w3, expert_w2, j)

        xs = fused_mm(
            m, d,
            [(hn, 0, d, ple_gate_w, (i,), 0), (p2[i], 0, p.shape[-1], ple_proj_w, (i,), 0)],
            [(x2, 0)],
            lambda dts, ex: ex[0] + _sigmoid(dts[0]) * dts[1], F32, tm=min(2048, m), tn=512, name="ple")
    return xs.reshape(batch, seq, d)
```

```python
import functools
import math

import jax
import jax.numpy as jnp
import numpy as np
from jax import lax
from jax.experimental import pallas as pl
from jax.experimental.pallas import tpu as pltpu

GRID_W = 64
ROPE_THETA = 10000.0
HEAD_DIM = 128
N_Q_HEADS = 8
N_KV_HEADS = 2
ATTN_W = N_Q_HEADS * HEAD_DIM
KV_W = N_KV_HEADS * HEAD_DIM
SSM_GROUP = 16
SSM_GROUPS = 64
SSM_W = SSM_GROUP * SSM_GROUPS
SSM_STATE = 64
MLP_CHUNK = 128
MLP_GROUPS = 8
MLP_GROUP_W = 128
MLP_W = MLP_GROUPS * MLP_GROUP_W
N_BRANCH = 3
N_EXPERTS = 8
TOP_K = 2
EPS = 1e-6

Q_OFF = 0
K_OFF = ATTN_W
V_OFF = K_OFF + KV_W
U_OFF = V_OFF + KV_W
ZU_OFF = U_OFF + SSM_W
ZV_OFF = ZU_OFF + MLP_W
GATE_OFF = ZV_OFF + MLP_W

V7X_VMEM_LIMIT_BYTES = 62 * 1024 * 1024
LANES = 128

BF16 = jnp.bfloat16
F32 = jnp.float32


def _params(*sem):
    return pltpu.CompilerParams(dimension_semantics=sem, vmem_limit_bytes=V7X_VMEM_LIMIT_BYTES)


def _gelu(x):
    c = math.sqrt(2.0 / math.pi)
    return 0.5 * x * (1.0 + jnp.tanh(c * (x + 0.044715 * (x * x * x))))


def _sigmoid(x):
    return 1.0 / (1.0 + jnp.exp(-x))


def _silu(x):
    return x * _sigmoid(x)


def _add_norm_kernel(*refs, has_delta):
    if has_delta:
        x_ref, d_ref, g_ref, xo_ref, h_ref = refs
        x = x_ref[...] + d_ref[...]
        xo_ref[...] = x
    else:
        x_ref, g_ref, h_ref = refs
        x = x_ref[...]
    y = x * lax.rsqrt(jnp.mean(x * x, axis=-1, keepdims=True) + EPS)
    h_ref[...] = (y * g_ref[...]).astype(h_ref.dtype)


def add_norm(x, delta, gains, layer, tm=512):
    m, d = x.shape
    row = pl.BlockSpec((tm, d), lambda i: (i, 0))
    gspec = pl.BlockSpec((None, 1, d), lambda i: (layer, 0, 0))
    if delta is None:
        h = pl.pallas_call(
            functools.partial(_add_norm_kernel, has_delta=False),
            grid=(m // tm,), in_specs=[row, gspec], out_specs=row,
            out_shape=jax.ShapeDtypeStruct((m, d), BF16),
            compiler_params=_params("parallel"), name="norm")(x, gains)
        return x, h
    xo, h = pl.pallas_call(
        functools.partial(_add_norm_kernel, has_delta=True),
        grid=(m // tm,), in_specs=[row, row, gspec], out_specs=[row, row],
        out_shape=[jax.ShapeDtypeStruct((m, d), F32), jax.ShapeDtypeStruct((m, d), BF16)],
        compiler_params=_params("parallel"), name="add_norm")(x, delta, gains)
    return xo, h


def _fused_mm_kernel(*refs, n_dots, n_extras, epilogue):
    a_refs = refs[:n_dots]
    w_refs = refs[n_dots:2 * n_dots]
    e_refs = refs[2 * n_dots:2 * n_dots + n_extras]
    o_ref = refs[2 * n_dots + n_extras]
    wb_refs = refs[2 * n_dots + n_extras + 1:]

    @pl.when(pl.program_id(1) == 0)
    def _():
        for w_ref, wb_ref in zip(w_refs, wb_refs):
            wb_ref[...] = w_ref[...].astype(BF16)

    dots = []
    for a_ref, wb_ref in zip(a_refs, wb_refs):
        a = a_ref[...]
        if a.dtype != BF16:
            a = a.astype(BF16)
        dots.append(jnp.dot(a, wb_ref[...], preferred_element_type=F32))
    extras = [e_ref[...] for e_ref in e_refs]
    o_ref[...] = epilogue(dots, extras).astype(o_ref.dtype)


def fused_mm(m, n, dots, extras, epilogue, out_dtype, tm, tn, name):
    in_specs, args, scratch = [], [], []
    for a, acb, k, _, _, _ in dots:
        in_specs.append(pl.BlockSpec((tm, k), lambda j, i, acb=acb: (i, acb)))
        args.append(a)
    for _, _, k, w, lead, off in dots:
        in_specs.append(pl.BlockSpec((None,) * len(lead) + (k, tn),
                                     lambda j, i, lead=tuple(lead), off=off: lead + (0, off + j)))
        args.append(w)
        scratch.append(pltpu.VMEM((k, tn), BF16))
    for e, off in extras:
        in_specs.append(pl.BlockSpec((tm, tn), lambda j, i, off=off: (i, off + j)))
        args.append(e)
    return pl.pallas_call(
        functools.partial(_fused_mm_kernel, n_dots=len(dots), n_extras=len(extras), epilogue=epilogue),
        grid=(n // tn, m // tm), in_specs=in_specs,
        out_specs=pl.BlockSpec((tm, tn), lambda j, i: (i, j)),
        out_shape=jax.ShapeDtypeStruct((m, n), out_dtype),
        scratch_shapes=scratch,
        compiler_params=_params("arbitrary", "arbitrary"), name=name)(*args)


def _rope(x, c, s):
    lane = lax.broadcasted_iota(jnp.int32, x.shape, x.ndim - 1)
    quarter = HEAD_DIM // 4
    partner = jnp.where((lane % (2 * quarter)) < quarter,
                        pltpu.roll(x, HEAD_DIM - quarter, x.ndim - 1),
                        pltpu.roll(x, quarter, x.ndim - 1))
    return x * c + partner * s


def _head_norm(x, g):
    return x * lax.rsqrt(jnp.mean(x * x, axis=-1, keepdims=True) + EPS) * g


def _attn_kernel(q_ref, k_ref, v_ref, cq_ref, sq_ref, ck_ref, sk_ref, qg_ref, kg_ref, o_ref, ks_ref, vs_ref, *,
                 rep):
    @pl.when(pl.program_id(2) == 0)
    def _():
        k = _head_norm(k_ref[...].astype(F32), kg_ref[...])
        ks_ref[...] = _rope(k, ck_ref[...], sk_ref[...]).astype(BF16)
        vs_ref[:, 0:HEAD_DIM] = v_ref[...]
        vs_ref[:, HEAD_DIM:2 * HEAD_DIM] = jnp.ones((v_ref.shape[0], HEAD_DIM), BF16)

    scale = HEAD_DIM ** -0.5 * math.log2(math.e)
    cq = cq_ref[...]
    sq = sq_ref[...]
    for hh in range(rep):
        sl = slice(hh * HEAD_DIM, (hh + 1) * HEAD_DIM)
        q = _head_norm(q_ref[:, sl].astype(F32), qg_ref[...])
        q = (_rope(q, cq, sq) * scale).astype(BF16)
        s = lax.dot_general(q, ks_ref[...], (((1,), (1,)), ((), ())), preferred_element_type=F32)
        m = jnp.max(s, axis=-1, keepdims=True)
        p = jnp.exp2(s - m).astype(BF16)
        o = jnp.dot(p, vs_ref[...], preferred_element_type=F32)
        o_ref[:, sl] = (o[:, 0:HEAD_DIM] / o[:, HEAD_DIM:HEAD_DIM + 1]).astype(o_ref.dtype)


def attention(proj, q_gain, k_gain, layer, rope_c, rope_s, batch, seq, tq=512):
    rep = N_Q_HEADS // N_KV_HEADS
    qw = rep * HEAD_DIM
    nq = seq // tq
    gspec = pl.BlockSpec((None, 1, HEAD_DIM), lambda b, g, i: (layer, 0, 0))
    return pl.pallas_call(
        functools.partial(_attn_kernel, rep=rep),
        grid=(batch, N_KV_HEADS, nq),
        in_specs=[
            pl.BlockSpec((tq, qw), lambda b, g, i: (b * nq + i, Q_OFF // qw + g)),
            pl.BlockSpec((seq, HEAD_DIM), lambda b, g, i: (b, K_OFF // HEAD_DIM + g)),
            pl.BlockSpec((seq, HEAD_DIM), lambda b, g, i: (b, V_OFF // HEAD_DIM + g)),
            pl.BlockSpec((tq, HEAD_DIM), lambda b, g, i: (i, 0)),
            pl.BlockSpec((tq, HEAD_DIM), lambda b, g, i: (i, 0)),
            pl.BlockSpec((seq, HEAD_DIM), lambda b, g, i: (0, 0)),
            pl.BlockSpec((seq, HEAD_DIM), lambda b, g, i: (0, 0)),
            gspec, gspec,
        ],
        out_specs=pl.BlockSpec((tq, qw), lambda b, g, i: (b * nq + i, g)),
        out_shape=jax.ShapeDtypeStruct((batch * seq, ATTN_W), BF16),
        scratch_shapes=[pltpu.VMEM((seq, HEAD_DIM), BF16), pltpu.VMEM((seq, 2 * HEAD_DIM), BF16)],
        compiler_params=_params("arbitrary", "arbitrary", "arbitrary"), name="attention",
    )(proj, proj, proj, rope_c, rope_s, rope_c, rope_s, q_gain, k_gain)


def rope_tables(seq):
    rows = seq // GRID_W
    t = jnp.arange(seq)
    pos = jnp.stack([t // GRID_W - rows // 2, t % GRID_W - GRID_W // 2], axis=-1).astype(F32)
    n_freq = HEAD_DIM // 4
    inv_freq = ROPE_THETA ** (-jnp.arange(n_freq, dtype=F32) / n_freq)
    ang = pos[:, :, None] * inv_freq
    cos, sin = jnp.cos(ang), jnp.sin(ang)
    c = jnp.concatenate([cos[:, 0], cos[:, 0], cos[:, 1], cos[:, 1]], axis=-1)
    s = jnp.concatenate([-sin[:, 0], sin[:, 0], -sin[:, 1], sin[:, 1]], axis=-1)
    return c, s


SSM_SEGS = 8
SSM_BLK = 16
SSM_PG = 2
SSM_QUAD = 4
SSM_PW = SSM_PG * SSM_STATE
SSM_ROW = SSM_BLK * SSM_PG * SSM_GROUP


def _ssm_kernel(u_ref, pw_ref, bt_ref, ct_ref, h_ref, at_ref, o_ref, z_ref, s_ref, e_ref, t_ref, bw_ref, vt_ref, *,
                n_seq, n_blocks):
    ns = n_seq * SSM_SEGS
    pw = SSM_PW
    xw = SSM_PG * SSM_GROUP
    qw = SSM_QUAD * xw
    pair_in_quad = pl.program_id(1)
    xoff = pl.multiple_of(pair_in_quad * xw, xw)

    def part(k):
        return slice(k * pw, (k + 1) * pw)

    bw_ref[...] = jnp.zeros_like(bw_ref)
    t_ref[...] = jnp.zeros_like(t_ref)

    lane = lax.broadcasted_iota(jnp.int32, (SSM_GROUP, pw), 1)
    owns = [lane // SSM_STATE == gp for gp in range(SSM_PG)]

    def pair_lanes(ref):
        return jnp.concatenate([ref[gp] for gp in range(SSM_PG)], axis=-1)

    apow, bbar, cmat, aseg = pair_lanes(pw_ref), pair_lanes(bt_ref), pair_lanes(ct_ref), pair_lanes(at_ref)

    def build(m, d, k, neg_im):
        ar, ai = apow[2 * d, k:k + 1, :], apow[2 * d + 1, k:k + 1, :]
        mr, mi = m[2 * d], m[2 * d + 1]
        xr, xi = ar * mr - ai * mi, ar * mi + ai * mr
        if neg_im:
            xi = -xi
        return [(jnp.where(own, xr, 0.0).astype(BF16), jnp.where(own, xi, 0.0).astype(BF16)) for own in owns]

    for step in range(SSM_BLK):
        for d, k_bw, k_vt in ((0, SSM_BLK - 1 - step, step + 1), (1, step, SSM_BLK - step)):
            for gp, (xr, xi) in enumerate(build(bbar, d, k_bw, False)):
                rws = pl.ds((step % 2) * qw + xoff + gp * SSM_GROUP, SSM_GROUP)
                bw_ref[step // 2, rws, part(2 * d)] = xr
                bw_ref[step // 2, rws, part(2 * d + 1)] = xi
            for gp, (xr, xi) in enumerate(build(cmat, d, k_vt, True)):
                rws = slice(step * xw + gp * SSM_GROUP, step * xw + (gp + 1) * SSM_GROUP)
                vt_ref[rws, part(2 * d)] = xr
                vt_ref[rws, part(2 * d + 1)] = xi

    def u2(s2):
        return jnp.concatenate([u_ref[2 * s2], u_ref[2 * s2 + 1]], axis=-1)

    z = jnp.dot(u2(0), bw_ref[0], preferred_element_type=F32)
    for s2 in range(1, SSM_BLK // 2):
        z = z + jnp.dot(u2(s2), bw_ref[s2], preferred_element_type=F32)
    z_ref[...] = z
    afr, afi, abr, abi = [jnp.broadcast_to(apow[k, SSM_BLK:SSM_BLK + 1, :], (ns, pw)) for k in range(4)]

    def rows(blk):
        return slice(blk * ns, (blk + 1) * ns)

    def scan(state, keep):
        fr, fi, br, bi = state
        for i in range(n_blocks):
            mf, mb = i, n_blocks - 1 - i
            if keep:
                s_ref[rows(mf), part(0)] = fr.astype(BF16)
                s_ref[rows(mf), part(1)] = fi.astype(BF16)
                s_ref[rows(mb), part(2)] = br.astype(BF16)
                s_ref[rows(mb), part(3)] = bi.astype(BF16)
            zfr, zfi = z_ref[rows(mf), part(0)], z_ref[rows(mf), part(1)]
            zbr, zbi = z_ref[rows(mb), part(2)], z_ref[rows(mb), part(3)]
            fr, fi = afr * fr - afi * fi + zfr, afr * fi + afi * fr + zfi
            br, bi = abr * br - abi * bi + zbr, abr * bi + abi * br + zbi
        return fr, fi, br, bi

    zero = jnp.zeros((ns, pw), F32)
    ends = scan((zero, zero, zero, zero), keep=False)
    for k in range(4):
        e_ref[:, part(k)] = ends[k]
    row0 = jnp.zeros((1, pw), F32)
    for b in range(n_seq):
        for kr, ki, order in ((0, 1, range(SSM_SEGS)), (2, 3, range(SSM_SEGS - 1, -1, -1))):
            tr, ti = aseg[kr], aseg[ki]
            cr, ci = row0, row0
            for q in order:
                row = slice(b * SSM_SEGS + q, b * SSM_SEGS + q + 1)
                er, ei = e_ref[row, part(kr)], e_ref[row, part(ki)]
                e_ref[row, part(kr)] = cr
                e_ref[row, part(ki)] = ci
                cr, ci = tr * cr - ti * ci + er, tr * ci + ti * cr + ei
    scan(tuple(e_ref[:, part(k)] for k in range(4)), keep=True)

    h = h_ref[...]
    for rp in range(SSM_BLK):
        off = xw * (SSM_BLK - 1 - rp)
        t_ref[rp // 2, pl.ds((rp % 2) * qw + xoff, xw), :] = h[:, off:off + SSM_ROW].astype(BF16)

    y = lax.dot_general(s_ref[...], vt_ref[...], (((1,), (1,)), ((), ())), preferred_element_type=F32)
    for s2 in range(SSM_BLK // 2):
        y = y + jnp.dot(u2(s2), t_ref[s2], preferred_element_type=F32)
    y = _gelu(y).astype(BF16)

    rr = lax.broadcasted_iota(jnp.int32, (SSM_ROW, SSM_BLK * qw), 0)
    cc = lax.broadcasted_iota(jnp.int32, (SSM_ROW, SSM_BLK * qw), 1)
    place = jnp.where(cc == (rr // xw) * qw + pair_in_quad * xw + rr % xw, 1.0, 0.0).astype(BF16)
    y = jnp.dot(y, place, preferred_element_type=F32)

    @pl.when(pair_in_quad == 0)
    def _():
        o_ref[...] = jnp.zeros_like(o_ref)

    for step in range(SSM_BLK):
        o_ref[step] += y[:, step * qw:(step + 1) * qw].astype(o_ref.dtype)


def _ssm_tables(lam_re, lam_im, log_dt, b_re, b_im, c_re, c_im, d_skip, seg_len):
    g, c, r, pg = SSM_GROUPS, SSM_GROUP, SSM_BLK, SSM_PG
    npair = g // pg
    lre, lim = lam_re.astype(F32), lam_im.astype(F32)
    dt = jnp.exp(log_dt.astype(F32))[..., None]

    def cexp(xr, xi):
        e = jnp.exp(xr)
        return e * jnp.cos(xi), e * jnp.sin(xi)

    def cmul(ar, ai, br, bi):
        return ar * br - ai * bi, ar * bi + ai * br

    steps = jnp.arange(r + 1, dtype=F32)[None, :, None, None]
    pr, pi = cexp((lre * dt)[:, None] * steps, (lim * dt)[:, None] * steps)
    den = lre * lre + lim * lim
    qr = ((pr[:, 1] - 1.0) * lre + pi[:, 1] * lim) / den
    qi = (pi[:, 1] * lre - (pr[:, 1] - 1.0) * lim) / den
    bbr, bbi = cmul(qr[..., None], qi[..., None], b_re.astype(F32), b_im.astype(F32))
    ccr, cci = c_re.astype(F32), c_im.astype(F32)

    def per_group(xr, xi):
        y = jnp.stack([xr, xi], axis=1)
        return jnp.transpose(y, (3, 0, 1, 2, 4)).reshape(g, 4, xr.shape[1], xr.shape[3])

    pw4 = per_group(pr, pi)
    bt4 = per_group(jnp.transpose(bbr, (0, 3, 1, 2)), jnp.transpose(bbi, (0, 3, 1, 2)))
    ct4 = per_group(jnp.transpose(ccr, (0, 2, 1, 3)), jnp.transpose(cci, (0, 2, 1, 3)))

    xr, xi = cmul(ccr[:, :, None], cci[:, :, None], jnp.transpose(pr[:, :r], (0, 2, 1, 3))[:, :, :, None],
                  jnp.transpose(pi[:, :r], (0, 2, 1, 3))[:, :, :, None])
    klag = jnp.einsum('zgkcq,zgqd->zgkdc', jnp.concatenate([xr, -xi], axis=-1),
                      jnp.concatenate([bbr, bbi], axis=2))
    kf, kb = klag[0], klag[1]
    lag0 = kf[:, 0] + kb[:, 0] + jnp.eye(c, dtype=F32) * d_skip.astype(F32).reshape(g, 1, c)
    lagged = jnp.concatenate([kb[:, :0:-1], lag0[:, None], kf[:, 1:]], axis=1)
    lagged = lagged.reshape(npair, pg, 2 * r - 1, c, c)
    h = jnp.einsum('aindc,ij->aidnjc', lagged, jnp.eye(pg, dtype=F32)).reshape(npair, pg * c, 2 * r - 1, pg * c)
    h = jnp.pad(h, ((0, 0), (0, 0), (0, 1), (0, 0))).reshape(npair, pg * c, 2 * r * pg * c)

    sr, si = cexp(lre * dt * seg_len, lim * dt * seg_len)
    a_seg = per_group(sr[:, None], si[:, None])
    return pw4, bt4, ct4, h, a_seg


def ssm_branch(proj, layer, tabs, w_glu, batch, seq):
    pw4, bt4, ct4, h, a_seg = tabs
    seg_len = seq // SSM_SEGS
    n_blocks = seg_len // SSM_BLK
    ns = batch * SSM_SEGS
    m = batch * seq
    nquad = SSM_GROUPS // (SSM_PG * SSM_QUAD)
    qw = SSM_QUAD * SSM_PG * SSM_GROUP
    rows = n_blocks * ns
    u = proj[:, U_OFF:U_OFF + SSM_W].reshape(batch, SSM_SEGS, n_blocks, SSM_BLK, nquad, qw)
    u = jnp.transpose(u, (4, 3, 2, 0, 1, 5)).reshape(nquad, SSM_BLK, rows, qw)

    quad = pl.BlockSpec((None, SSM_BLK, rows, qw), lambda q, j: (q, 0, 0, 0))

    def per_group(a):
        return pl.BlockSpec((SSM_PG,) + a.shape[1:], lambda q, j: (q * SSM_QUAD + j, 0, 0, 0))

    y = pl.pallas_call(
        functools.partial(_ssm_kernel, n_seq=batch, n_blocks=n_blocks),
        grid=(nquad, SSM_QUAD),
        in_specs=[quad, per_group(pw4), per_group(bt4), per_group(ct4),
                  pl.BlockSpec((None,) + h.shape[1:], lambda q, j: (q * SSM_QUAD + j, 0, 0)), per_group(a_seg)],
        out_specs=quad,
        out_shape=jax.ShapeDtypeStruct((nquad, SSM_BLK, rows, qw), BF16),
        scratch_shapes=[pltpu.VMEM((rows, 4 * SSM_PW), F32), pltpu.VMEM((rows, 4 * SSM_PW), BF16),
                        pltpu.VMEM((ns, 4 * SSM_PW), F32), pltpu.VMEM((SSM_BLK // 2, 2 * qw, SSM_ROW), BF16),
                        pltpu.VMEM((SSM_BLK // 2, 2 * qw, 4 * SSM_PW), BF16),
                        pltpu.VMEM((SSM_ROW, 4 * SSM_PW), BF16)],
        compiler_params=_params("arbitrary", "arbitrary"), name="ssm")(u, pw4, bt4, ct4, h, a_seg)
    y = y.reshape(nquad, SSM_BLK, n_blocks, batch, SSM_SEGS, qw)
    y = jnp.transpose(y, (3, 4, 2, 1, 0, 5)).reshape(m, SSM_W)
    return fused_mm(m, SSM_W, [(y, 0, SSM_W, w_glu, (layer,), 0)], [(y, 0)],
                    lambda dts, ex: ex[0].astype(F32) * _sigmoid(dts[0]), BF16, tm=1024, tn=512, name="ssm_glu")


GMLP_NC = 4


def _gmlp_kernel(zu0_ref, zu1_ref, zv0_ref, zv1_ref, g_ref, ws_ref, bb_ref, o_ref):
    half = MLP_W // 2
    zv = jnp.concatenate([zv0_ref[...], zv1_ref[...]], axis=-1).astype(F32)
    v = _gelu(zv)
    v = (v * lax.rsqrt(jnp.mean(v * v, axis=-1, keepdims=True) + EPS) * g_ref[...]).astype(BF16)
    for g in range(MLP_GROUPS):
        cs = slice(g * MLP_GROUP_W, (g + 1) * MLP_GROUP_W)
        vg = jnp.concatenate([v[n * MLP_CHUNK:(n + 1) * MLP_CHUNK, cs] for n in range(GMLP_NC)], axis=-1)
        s = jnp.dot(ws_ref[g].astype(BF16), vg, preferred_element_type=F32)
        zu_ref = zu0_ref if g * MLP_GROUP_W < half else zu1_ref
        us = slice((g * MLP_GROUP_W) % half, (g * MLP_GROUP_W) % half + MLP_GROUP_W)
        for n in range(GMLP_NC):
            rs = slice(n * MLP_CHUNK, (n + 1) * MLP_CHUNK)
            sn = s[:, n * MLP_GROUP_W:(n + 1) * MLP_GROUP_W] + bb_ref[g]
            o_ref[rs, cs] = (_gelu(zu_ref[rs, us].astype(F32)) * sn).astype(o_ref.dtype)


def gmlp_branch(proj, layer, v_gain, w_s, b_bcast, m):
    rows = GMLP_NC * MLP_CHUNK
    half = MLP_W // 2

    def zspec(off):
        return pl.BlockSpec((rows, half), lambda i, off=off: (i, off // half))

    return pl.pallas_call(
        _gmlp_kernel, grid=(m // rows,),
        in_specs=[zspec(ZU_OFF), zspec(ZU_OFF + half), zspec(ZV_OFF), zspec(ZV_OFF + half),
                  pl.BlockSpec((None, 1, MLP_W), lambda i: (layer, 0, 0)),
                  pl.BlockSpec((None, MLP_GROUPS, MLP_CHUNK, MLP_CHUNK), lambda i: (layer, 0, 0, 0)),
                  pl.BlockSpec((None, MLP_GROUPS, MLP_CHUNK, MLP_GROUP_W), lambda i: (layer, 0, 0, 0))],
        out_specs=pl.BlockSpec((rows, MLP_W), lambda i: (i, 0)),
        out_shape=jax.ShapeDtypeStruct((m, MLP_W), BF16),
        compiler_params=_params("parallel"), name="gmlp",
    )(proj, proj, proj, proj, v_gain, w_s, b_bcast)


DENSE_TM = 1024
DENSE_TF = 512
MOE_TM = 1280
MOE_TF = 512
MOE_FC = 256
FFN_SUB = 256


def _ffn_kernel(ue_ref, un_ref, ub_ref, x_ref, w1_ref, w3_ref, w2_ref, o_ref, *, live_counts, fc):
    del ue_ref, ub_ref
    u = pl.program_id(0)
    j = pl.program_id(1)
    nsub = un_ref[u]

    @pl.when(j == 0)
    def _():
        o_ref[...] = jnp.zeros_like(o_ref)

    tf = w2_ref.shape[0]
    for k in live_counts:
        @pl.when(nsub == k)
        def _(k=k):
            rows = k * FFN_SUB
            x = x_ref[0:rows, :]
            for f0 in range(0, tf, fc):
                fs = slice(f0, f0 + fc)
                h1 = jnp.dot(x, w1_ref[:, fs].astype(BF16), preferred_element_type=F32)
                h3 = jnp.dot(x, w3_ref[:, fs].astype(BF16), preferred_element_type=F32)
                act = (_silu(h1) * h3).astype(BF16)
                o_ref[0:rows, :] += jnp.dot(act, w2_ref[fs, :].astype(BF16), preferred_element_type=F32)


def swiglu_ffn(x, unit_expert, unit_nsub, unit_block, w1, w3, w2, lead, live_counts, tm, tf, fc):
    rows, d = x.shape
    f = w1.shape[-1]
    n_units = rows // tm
    nf = f // tf
    nl = len(lead)

    def wmap_up(u, j, ue, un, ub):
        return tuple(lead) + (ue[u], 0, jnp.where(un[u] > 0, j, nf - 1))

    def wmap_down(u, j, ue, un, ub):
        return tuple(lead) + (ue[u], jnp.where(un[u] > 0, j, nf - 1), 0)

    def rmap(u, j, ue, un, ub):
        return (ub[u], 0)

    grid_spec = pltpu.PrefetchScalarGridSpec(
        num_scalar_prefetch=3, grid=(n_units, nf),
        in_specs=[pl.BlockSpec((tm, d), rmap),
                  pl.BlockSpec((None,) * (nl + 1) + (d, tf), wmap_up),
                  pl.BlockSpec((None,) * (nl + 1) + (d, tf), wmap_up),
                  pl.BlockSpec((None,) * (nl + 1) + (tf, d), wmap_down)],
        out_specs=pl.BlockSpec((tm, d), lambda u, j, ue, un, ub: (u, 0)))
    return pl.pallas_call(
        functools.partial(_ffn_kernel, live_counts=tuple(live_counts), fc=fc), grid_spec=grid_spec,
        out_shape=jax.ShapeDtypeStruct((rows, d), F32),
        compiler_params=_params("arbitrary", "arbitrary"), name="swiglu_ffn",
    )(unit_expert, unit_nsub, unit_block, x, w1, w3, w2)


def _router_kernel(x_ref, g_ref, wr_ref, h_ref, r_ref):
    x = x_ref[...]
    h = x * lax.rsqrt(jnp.mean(x * x, axis=-1, keepdims=True) + EPS) * g_ref[...]
    h_ref[...] = h
    logits = jnp.dot(h, wr_ref[...], preferred_element_type=F32, precision=lax.Precision.HIGHEST)
    lane = lax.broadcasted_iota(jnp.int32, logits.shape, 1)
    neg = jnp.float32(-jnp.inf)
    logits = jnp.where(lane < N_EXPERTS, logits, neg)
    m1 = jnp.max(logits, axis=-1, keepdims=True)
    i1 = jnp.min(jnp.where(logits == m1, lane, LANES), axis=-1, keepdims=True)
    rest = jnp.where(lane == i1, neg, logits)
    m2 = jnp.max(rest, axis=-1, keepdims=True)
    i2 = jnp.min(jnp.where(rest == m2, lane, LANES), axis=-1, keepdims=True)
    e = jnp.exp(m2 - m1)
    g1 = 1.0 / (1.0 + e)
    g2 = e / (1.0 + e)
    r_ref[...] = jnp.where(lane == 0, i1.astype(F32),
                           jnp.where(lane == 1, i2.astype(F32),
                                     jnp.where(lane == 2, g1, jnp.where(lane == 3, g2, 0.0))))


def norm_router(x, gains, layer, w_router_pad, tm=256):
    m, d = x.shape
    row = pl.BlockSpec((tm, d), lambda i: (i, 0))
    return pl.pallas_call(
        _router_kernel, grid=(m // tm,),
        in_specs=[row, pl.BlockSpec((None, 1, d), lambda i: (layer, 0, 0)),
                  pl.BlockSpec((d, LANES), lambda i: (0, 0))],
        out_specs=[row, pl.BlockSpec((tm, LANES), lambda i: (i, 0))],
        out_shape=[jax.ShapeDtypeStruct((m, d), F32), jax.ShapeDtypeStruct((m, LANES), F32)],
        compiler_params=_params("parallel"), name="norm_router")(x, gains, w_router_pad)


DMA_GROUP = 8


def _gather_kernel(src_ref, nv_ref, h_ref, o_ref, buf_ref, sem):
    i = pl.program_id(0)
    tb = buf_ref.shape[0]
    base = i * tb
    nv = nv_ref[i]
    ngroups = nv // DMA_GROUP

    @pl.when(i == 0)
    def _():
        buf_ref[...] = jnp.zeros_like(buf_ref)

    def row_copy(t):
        return pltpu.make_async_copy(h_ref.at[pl.ds(src_ref[base + t], 1)], buf_ref.at[pl.ds(t, 1)], sem)

    def group_copy():
        return pltpu.make_async_copy(h_ref.at[pl.ds(0, DMA_GROUP)], buf_ref.at[pl.ds(0, DMA_GROUP)], sem)

    def issue_group(c, carry):
        for r in range(DMA_GROUP):
            row_copy(c * DMA_GROUP + r).start()
        return carry

    def issue_row(t, carry):
        row_copy(t).start()
        return carry

    def wait_group(c, carry):
        group_copy().wait()
        return carry

    def wait_row(t, carry):
        row_copy(t).wait()
        return carry

    lax.fori_loop(0, ngroups, issue_group, 0)
    lax.fori_loop(ngroups * DMA_GROUP, nv, issue_row, 0)
    lax.fori_loop(0, ngroups, wait_group, 0)
    lax.fori_loop(ngroups * DMA_GROUP, nv, wait_row, 0)
    row = lax.broadcasted_iota(jnp.int32, o_ref.shape, 0)
    o_ref[...] = jnp.where(row < nv, buf_ref[...], 0.0).astype(o_ref.dtype)


def moe_gather(h, src, n_valid, cap, tb):
    _, d = h.shape
    grid_spec = pltpu.PrefetchScalarGridSpec(
        num_scalar_prefetch=2, grid=(cap // tb,),
        in_specs=[pl.BlockSpec(memory_space=pl.ANY)],
        out_specs=pl.BlockSpec((tb, d), lambda i, s, n: (i, 0)),
        scratch_shapes=[pltpu.VMEM((tb, d), h.dtype), pltpu.SemaphoreType.DMA(())])
    return pl.pallas_call(
        _gather_kernel, grid_spec=grid_spec,
        out_shape=jax.ShapeDtypeStruct((cap, d), BF16),
        compiler_params=_params("arbitrary"), name="moe_gather")(src, n_valid, h)


COMBINE_TB = 256


def _combine_kernel(dest_ref, x_ref, r_ref, g_ref, y_ref, xo_ref, h_ref, buf_ref, sem):
    base = pl.program_id(0) * COMBINE_TB

    def row_copy(t, k):
        return pltpu.make_async_copy(y_ref.at[pl.ds(dest_ref[TOP_K * (base + t) + k], 1)],
                                     buf_ref.at[k, pl.ds(t, 1)], sem)

    def issue_group(c, carry):
        for r in range(DMA_GROUP // TOP_K):
            for k in range(TOP_K):
                row_copy(c * (DMA_GROUP // TOP_K) + r, k).start()
        return carry

    lax.fori_loop(0, COMBINE_TB * TOP_K // DMA_GROUP, issue_group, 0)
    for k in range(TOP_K):
        pltpu.make_async_copy(y_ref.at[pl.ds(0, COMBINE_TB)], buf_ref.at[k], sem).wait()
    r = r_ref[...]
    g1 = r[:, 2:3]
    g2 = r[:, 3:4]
    x = x_ref[...] + (buf_ref[0] * g1 + buf_ref[1] * g2)
    xo_ref[...] = x
    y = x * lax.rsqrt(jnp.mean(x * x, axis=-1, keepdims=True) + EPS)
    h_ref[...] = (y * g_ref[...]).astype(h_ref.dtype)


def moe_combine(x, route, dest, ybuf, gains, layer):
    m, d = x.shape
    row = pl.BlockSpec((COMBINE_TB, d), lambda i, dr: (i, 0))
    grid_spec = pltpu.PrefetchScalarGridSpec(
        num_scalar_prefetch=1, grid=(m // COMBINE_TB,),
        in_specs=[row, pl.BlockSpec((COMBINE_TB, LANES), lambda i, dr: (i, 0)),
                  pl.BlockSpec((None, 1, d), lambda i, dr: (layer, 0, 0)),
                  pl.BlockSpec(memory_space=pl.ANY)],
        out_specs=[row, row],
        scratch_shapes=[pltpu.VMEM((TOP_K, COMBINE_TB, d), F32), pltpu.SemaphoreType.DMA(())])
    return pl.pallas_call(
        _combine_kernel, grid_spec=grid_spec,
        out_shape=[jax.ShapeDtypeStruct((m, d), F32), jax.ShapeDtypeStruct((m, d), BF16)],
        compiler_params=_params("arbitrary"), name="moe_combine")(dest, x, route, gains, ybuf)


def moe_layer(x1, ffn_norm, ple_norm, layer, w_router, e_w1, e_w3, e_w2, j):
    m, d = x1.shape
    wr_pad = jnp.zeros((d, LANES), F32).at[:, :N_EXPERTS].set(w_router)
    h, route = norm_router(x1, ffn_norm, layer, wr_pad)
    e_flat = route[:, :TOP_K].astype(jnp.int32).reshape(-1)
    onehot = (e_flat[:, None] == jnp.arange(N_EXPERTS)[None, :]).astype(jnp.int32)
    csum = jnp.cumsum(onehot, axis=0)
    rank = jnp.sum((csum - onehot) * onehot, axis=1)
    counts = csum[-1]
    tm = MOE_TM
    n_units_e = (counts + tm - 1) // tm
    unit_end = jnp.cumsum(n_units_e)
    unit_start = unit_end - n_units_e
    dest = (unit_start[e_flat] * tm + rank).astype(jnp.int32)
    n_units = (m * TOP_K) // tm + N_EXPERTS
    uidx = jnp.arange(n_units)
    ue = jnp.minimum(jnp.searchsorted(unit_end, uidx, side='right'), N_EXPERTS - 1).astype(jnp.int32)
    live_rows = jnp.clip(counts[ue] - (uidx - unit_start[ue]) * tm, 0, tm)
    live_rows = jnp.where(uidx < unit_end[-1], live_rows, 0)
    un = ((live_rows + FFN_SUB - 1) // FFN_SUB).astype(jnp.int32)
    last_live = jnp.maximum(unit_end[-1] - 1, 0)
    ue = jnp.where(uidx < unit_end[-1], ue, ue[last_live]).astype(jnp.int32)
    ub = jnp.where(uidx < unit_end[-1], uidx, last_live).astype(jnp.int32)

    cap = n_units * tm
    src = jnp.zeros((cap,), jnp.int32).at[dest].set(jnp.arange(m * TOP_K, dtype=jnp.int32) // TOP_K)

    xbuf = moe_gather(h, src, live_rows.astype(jnp.int32), cap, tm)
    ybuf = swiglu_ffn(xbuf, ue, un, ub, e_w1, e_w3, e_w2, (j,), range(1, tm // FFN_SUB + 1), tm, MOE_TF,
                      MOE_FC)
    return moe_combine(x1, route, dest, ybuf, ple_norm, layer)


def kernel(x, p, mix_norm, w_in, q_norm, k_norm, ssm_lambda_re, ssm_lambda_im, ssm_log_dt, ssm_b_re, ssm_b_im,
           ssm_c_re, ssm_c_im, ssm_d, ssm_glu_w, gmlp_v_norm, gmlp_ws, gmlp_b, w_branch, w_out, ffn_norm,
           dense_w1, dense_w3, dense_w2, router_w, expert_w1, expert_w3, expert_w2, ple_norm, ple_gate_w,
           ple_proj_w):
    batch, seq, d = x.shape
    depth = w_in.shape[0]
    n_in = w_in.shape[-1]
    m = batch * seq
    xs = x.reshape(m, d)
    rope_c, rope_s = rope_tables(seq)

    def g3(a):
        return a.reshape(a.shape[0], 1, a.shape[1])

    mix_norm, q_norm, k_norm, gmlp_v_norm, ffn_norm, ple_norm = map(
        g3, (mix_norm, q_norm, k_norm, gmlp_v_norm, ffn_norm, ple_norm))
    b_bcast = jnp.broadcast_to(gmlp_b[..., None], gmlp_b.shape + (MLP_GROUP_W,))
    p2 = p.reshape(depth, m, p.shape[-1])
    dense_units = m // DENSE_TM
    dense_un = jnp.full((dense_units,), DENSE_TM // FFN_SUB, jnp.int32)

    for i in range(depth):
        _, h = add_norm(xs, None, mix_norm, i)
        proj = fused_mm(m, n_in, [(h, 0, d, w_in, (i,), 0)], [], lambda dts, ex: dts[0], BF16,
                        tm=min(2048, m), tn=512, name="in_proj")
        attn = attention(proj, q_norm, k_norm, i, rope_c, rope_s, batch, seq)
        tabs = _ssm_tables(ssm_lambda_re[i], ssm_lambda_im[i], ssm_log_dt[i], ssm_b_re[i], ssm_b_im[i],
                           ssm_c_re[i], ssm_c_im[i], ssm_d[i], seq // SSM_SEGS)
        ssm = ssm_branch(proj, i, tabs, ssm_glu_w, batch, seq)
        mlp = gmlp_branch(proj, i, gmlp_v_norm, gmlp_ws, b_bcast, m)

        tn = 512
        merged = fused_mm(
            m, d,
            [(br, 0, br.shape[1], w_branch, (i, n), 0) for n, br in enumerate((attn, ssm, mlp))],
            [(proj, (GATE_OFF + n * d) // tn) for n in range(N_BRANCH)],
            lambda dts, ex: sum(_sigmoid(e.astype(F32)) * dt for e, dt in zip(ex, dts)),
            BF16, tm=1024, tn=tn, name="branch_merge")
        x1 = fused_mm(m, d, [(merged, 0, d, w_out, (i,), 0)], [(xs, 0)],
                      lambda dts, ex: ex[0] + dts[0], F32, tm=min(2048, m), tn=512, name="out_proj")

        j = i // 2
        if i % 2 == 0:
            _, h2 = add_norm(x1, None, ffn_norm, i)
            dense_ue = jnp.full((dense_units,), j, jnp.int32)
            y = swiglu_ffn(h2, dense_ue, dense_un, jnp.arange(dense_units, dtype=jnp.int32),
                           dense_w1, dense_w3, dense_w2, (), (DENSE_TM // FFN_SUB,), DENSE_TM, DENSE_TF,
                           DENSE_TF)
            x2, hn = add_norm(x1, y, ple_norm, i)
        else:
            x2, hn = moe_layer(x1, ffn_norm, ple_norm, i, router_w[j], expert_w1, expert_w3, expert_w2, j)

        xs = fused_mm(
            m, d,
            [(hn, 0, d, ple_gate_w, (i,), 0), (p2[i], 0, p.shape[-1], ple_proj_w, (i,), 0)],
            [(x2, 0)],
            lambda dts, ex: ex[0] + _sigmoid(dts[0]) * dts[1], F32, tm=min(2048, m), tn=512, name="ple")
    return xs.reshape(batch, seq, d)
```

```python
import functools
import math

import jax
import jax.numpy as jnp
import numpy as np
from jax import lax
from jax.experimental import pallas as pl
from jax.experimental.pallas import tpu as pltpu

GRID_W = 64
ROPE_THETA = 10000.0
HEAD_DIM = 128
N_Q_HEADS = 8
N_KV_HEADS = 2
ATTN_W = N_Q_HEADS * HEAD_DIM
KV_W = N_KV_HEADS * HEAD_DIM
SSM_GROUP = 16
SSM_GROUPS = 64
SSM_W = SSM_GROUP * SSM_GROUPS
SSM_STATE = 64
MLP_CHUNK = 128
MLP_GROUPS = 8
MLP_GROUP_W = 128
MLP_W = MLP_GROUPS * MLP_GROUP_W
N_BRANCH = 3
N_EXPERTS = 8
TOP_K = 2
EPS = 1e-6

Q_OFF = 0
K_OFF = ATTN_W
V_OFF = K_OFF + KV_W
U_OFF = V_OFF + KV_W
ZU_OFF = U_OFF + SSM_W
ZV_OFF = ZU_OFF + MLP_W
GATE_OFF = ZV_OFF + MLP_W

V7X_VMEM_LIMIT_BYTES = 62 * 1024 * 1024
LANES = 128

BF16 = jnp.bfloat16
F32 = jnp.float32


def _params(*sem):
    return pltpu.CompilerParams(dimension_semantics=sem, vmem_limit_bytes=V7X_VMEM_LIMIT_BYTES)


def _gelu(x):
    c = math.sqrt(2.0 / math.pi)
    return 0.5 * x * (1.0 + jnp.tanh(c * (x + 0.044715 * (x * x * x))))


def _sigmoid(x):
    return 1.0 / (1.0 + jnp.exp(-x))


def _silu(x):
    return x * _sigmoid(x)


def _add_norm_kernel(*refs, has_delta):
    if has_delta:
        x_ref, d_ref, g_ref, xo_ref, h_ref = refs
        x = x_ref[...] + d_ref[...]
        xo_ref[...] = x
    else:
        x_ref, g_ref, h_ref = refs
        x = x_ref[...]
    y = x * lax.rsqrt(jnp.mean(x * x, axis=-1, keepdims=True) + EPS)
    h_ref[...] = (y * g_ref[...]).astype(h_ref.dtype)


def add_norm(x, delta, gains, layer, tm=512):
    m, d = x.shape
    row = pl.BlockSpec((tm, d), lambda i: (i, 0))
    gspec = pl.BlockSpec((None, 1, d), lambda i: (layer, 0, 0))
    if delta is None:
        h = pl.pallas_call(
            functools.partial(_add_norm_kernel, has_delta=False),
            grid=(m // tm,), in_specs=[row, gspec], out_specs=row,
            out_shape=jax.ShapeDtypeStruct((m, d), BF16),
            compiler_params=_params("parallel"), name="norm")(x, gains)
        return x, h
    xo, h = pl.pallas_call(
        functools.partial(_add_norm_kernel, has_delta=True),
        grid=(m // tm,), in_specs=[row, row, gspec], out_specs=[row, row],
        out_shape=[jax.ShapeDtypeStruct((m, d), F32), jax.ShapeDtypeStruct((m, d), BF16)],
        compiler_params=_params("parallel"), name="add_norm")(x, delta, gains)
    return xo, h


def _fused_mm_kernel(*refs, n_dots, n_extras, epilogue):
    a_refs = refs[:n_dots]
    w_refs = refs[n_dots:2 * n_dots]
    e_refs = refs[2 * n_dots:2 * n_dots + n_extras]
    o_ref = refs[2 * n_dots + n_extras]
    wb_refs = refs[2 * n_dots + n_extras + 1:]

    @pl.when(pl.program_id(1) == 0)
    def _():
        for w_ref, wb_ref in zip(w_refs, wb_refs):
            wb_ref[...] = w_ref[...].astype(BF16)

    dots = []
    for a_ref, wb_ref in zip(a_refs, wb_refs):
        a = a_ref[...]
        if a.dtype != BF16:
            a = a.astype(BF16)
        dots.append(jnp.dot(a, wb_ref[...], preferred_element_type=F32))
    extras = [e_ref[...] for e_ref in e_refs]
    o_ref[...] = epilogue(dots, extras).astype(o_ref.dtype)


def fused_mm(m, n, dots, extras, epilogue, out_dtype, tm, tn, name):
    in_specs, args, scratch = [], [], []
    for a, acb, k, _, _, _ in dots:
        a, a_lead = a if isinstance(a, tuple) else (a, ())
        in_specs.append(pl.BlockSpec((None,) * len(a_lead) + (tm, k),
                                     lambda j, i, acb=acb, a_lead=tuple(a_lead): a_lead + (i, acb)))
        args.append(a)
    for _, _, k, w, lead, off in dots:
        in_specs.append(pl.BlockSpec((None,) * len(lead) + (k, tn),
                                     lambda j, i, lead=tuple(lead), off=off: lead + (0, off + j)))
        args.append(w)
        scratch.append(pltpu.VMEM((k, tn), BF16))
    for e, off in extras:
        in_specs.append(pl.BlockSpec((tm, tn), lambda j, i, off=off: (i, off + j)))
        args.append(e)
    return pl.pallas_call(
        functools.partial(_fused_mm_kernel, n_dots=len(dots), n_extras=len(extras), epilogue=epilogue),
        grid=(n // tn, m // tm), in_specs=in_specs,
        out_specs=pl.BlockSpec((tm, tn), lambda j, i: (i, j)),
        out_shape=jax.ShapeDtypeStruct((m, n), out_dtype),
        scratch_shapes=scratch,
        compiler_params=_params("arbitrary", "arbitrary"), name=name)(*args)


def _rope(x, c, s):
    lane = lax.broadcasted_iota(jnp.int32, x.shape, x.ndim - 1)
    quarter = HEAD_DIM // 4
    partner = jnp.where((lane % (2 * quarter)) < quarter,
                        pltpu.roll(x, HEAD_DIM - quarter, x.ndim - 1),
                        pltpu.roll(x, quarter, x.ndim - 1))
    return x * c + partner * s


def _head_norm(x, g):
    return x * lax.rsqrt(jnp.mean(x * x, axis=-1, keepdims=True) + EPS) * g


def _attn_kernel(q_ref, k_ref, v_ref, cq_ref, sq_ref, ck_ref, sk_ref, qg_ref, kg_ref, o_ref, ks_ref, vs_ref, *,
                 rep):
    @pl.when(pl.program_id(2) == 0)
    def _():
        k = _head_norm(k_ref[...].astype(F32), kg_ref[...])
        ks_ref[...] = _rope(k, ck_ref[...], sk_ref[...]).astype(BF16)
        vs_ref[:, 0:HEAD_DIM] = v_ref[...]
        vs_ref[:, HEAD_DIM:2 * HEAD_DIM] = jnp.ones((v_ref.shape[0], HEAD_DIM), BF16)

    scale = HEAD_DIM ** -0.5 * math.log2(math.e)
    cq = cq_ref[...]
    sq = sq_ref[...]
    for hh in range(rep):
        sl = slice(hh * HEAD_DIM, (hh + 1) * HEAD_DIM)
        q = _head_norm(q_ref[:, sl].astype(F32), qg_ref[...])
        q = (_rope(q, cq, sq) * scale).astype(BF16)
        s = lax.dot_general(q, ks_ref[...], (((1,), (1,)), ((), ())), preferred_element_type=F32)
        m = jnp.max(s, axis=-1, keepdims=True)
        p = jnp.exp2(s - m).astype(BF16)
        o = jnp.dot(p, vs_ref[...], preferred_element_type=F32)
        o_ref[:, sl] = (o[:, 0:HEAD_DIM] / o[:, HEAD_DIM:HEAD_DIM + 1]).astype(o_ref.dtype)


def attention(proj, q_gain, k_gain, layer, rope_c, rope_s, batch, seq, tq=512):
    rep = N_Q_HEADS // N_KV_HEADS
    qw = rep * HEAD_DIM
    nq = seq // tq
    gspec = pl.BlockSpec((None, 1, HEAD_DIM), lambda b, g, i: (layer, 0, 0))
    return pl.pallas_call(
        functools.partial(_attn_kernel, rep=rep),
        grid=(batch, N_KV_HEADS, nq),
        in_specs=[
            pl.BlockSpec((tq, qw), lambda b, g, i: (b * nq + i, Q_OFF // qw + g)),
            pl.BlockSpec((seq, HEAD_DIM), lambda b, g, i: (b, K_OFF // HEAD_DIM + g)),
            pl.BlockSpec((seq, HEAD_DIM), lambda b, g, i: (b, V_OFF // HEAD_DIM + g)),
            pl.BlockSpec((tq, HEAD_DIM), lambda b, g, i: (i, 0)),
            pl.BlockSpec((tq, HEAD_DIM), lambda b, g, i: (i, 0)),
            pl.BlockSpec((seq, HEAD_DIM), lambda b, g, i: (0, 0)),
            pl.BlockSpec((seq, HEAD_DIM), lambda b, g, i: (0, 0)),
            gspec, gspec,
        ],
        out_specs=pl.BlockSpec((tq, qw), lambda b, g, i: (b * nq + i, g)),
        out_shape=jax.ShapeDtypeStruct((batch * seq, ATTN_W), BF16),
        scratch_shapes=[pltpu.VMEM((seq, HEAD_DIM), BF16), pltpu.VMEM((seq, 2 * HEAD_DIM), BF16)],
        compiler_params=_params("arbitrary", "arbitrary", "arbitrary"), name="attention",
    )(proj, proj, proj, rope_c, rope_s, rope_c, rope_s, q_gain, k_gain)


def rope_tables(seq):
    rows = seq // GRID_W
    t = np.arange(seq)
    pos = np.stack([t // GRID_W - rows // 2, t % GRID_W - GRID_W // 2], axis=-1).astype(np.float32)
    n_freq = HEAD_DIM // 4
    inv_freq = np.float32(ROPE_THETA) ** (-np.arange(n_freq, dtype=np.float32) / np.float32(n_freq))
    ang = pos[:, :, None] * inv_freq.astype(np.float32)
    cos, sin = np.cos(ang).astype(np.float32), np.sin(ang).astype(np.float32)
    c = np.concatenate([cos[:, 0], cos[:, 0], cos[:, 1], cos[:, 1]], axis=-1)
    s = np.concatenate([-sin[:, 0], sin[:, 0], -sin[:, 1], sin[:, 1]], axis=-1)
    return jnp.asarray(c), jnp.asarray(s)


SSM_SEGS = 8
SSM_BLK = 16
SSM_PG = 2
SSM_QUAD = 4
SSM_PW = SSM_PG * SSM_STATE
SSM_ROW = SSM_BLK * SSM_PG * SSM_GROUP


def _ssm_kernel(u_ref, pw_ref, bt_ref, ct_ref, h_ref, at_ref, o_ref, z_ref, s_ref, e_ref, t_ref, bw_ref, vt_ref, *,
                n_seq, n_blocks):
    ns = n_seq * SSM_SEGS
    pw = SSM_PW
    xw = SSM_PG * SSM_GROUP
    qw = SSM_QUAD * xw
    pair_in_quad = pl.program_id(1)
    xoff = pl.multiple_of(pair_in_quad * xw, xw)

    def part(k):
        return slice(k * pw, (k + 1) * pw)

    bw_ref[...] = jnp.zeros_like(bw_ref)
    t_ref[...] = jnp.zeros_like(t_ref)

    lane = lax.broadcasted_iota(jnp.int32, (SSM_GROUP, pw), 1)
    owns = [lane // SSM_STATE == gp for gp in range(SSM_PG)]

    def pair_lanes(ref):
        return jnp.concatenate([ref[gp] for gp in range(SSM_PG)], axis=-1)

    apow, bbar, cmat, aseg = pair_lanes(pw_ref), pair_lanes(bt_ref), pair_lanes(ct_ref), pair_lanes(at_ref)

    def build(m, d, k, neg_im):
        ar, ai = apow[2 * d, k:k + 1, :], apow[2 * d + 1, k:k + 1, :]
        mr, mi = m[2 * d], m[2 * d + 1]
        xr, xi = ar * mr - ai * mi, ar * mi + ai * mr
        if neg_im:
            xi = -xi
        return [(jnp.where(own, xr, 0.0).astype(BF16), jnp.where(own, xi, 0.0).astype(BF16)) for own in owns]

    for step in range(SSM_BLK):
        for d, k_bw, k_vt in ((0, SSM_BLK - 1 - step, step + 1), (1, step, SSM_BLK - step)):
            for gp, (xr, xi) in enumerate(build(bbar, d, k_bw, False)):
                rws = pl.ds((step % 2) * qw + xoff + gp * SSM_GROUP, SSM_GROUP)
                bw_ref[step // 2, rws, part(2 * d)] = xr
                bw_ref[step // 2, rws, part(2 * d + 1)] = xi
            for gp, (xr, xi) in enumerate(build(cmat, d, k_vt, True)):
                rws = slice(step * xw + gp * SSM_GROUP, step * xw + (gp + 1) * SSM_GROUP)
                vt_ref[rws, part(2 * d)] = xr
                vt_ref[rws, part(2 * d + 1)] = xi

    def u2(s2):
        return jnp.concatenate([u_ref[2 * s2], u_ref[2 * s2 + 1]], axis=-1)

    z = jnp.dot(u2(0), bw_ref[0], preferred_element_type=F32)
    for s2 in range(1, SSM_BLK // 2):
        z = z + jnp.dot(u2(s2), bw_ref[s2], preferred_element_type=F32)
    z_ref[...] = z
    afr, afi, abr, abi = [jnp.broadcast_to(apow[k, SSM_BLK:SSM_BLK + 1, :], (ns, pw)) for k in range(4)]

    def rows(blk):
        return slice(blk * ns, (blk + 1) * ns)

    def scan(state, keep):
        fr, fi, br, bi = state
        for i in range(n_blocks):
            mf, mb = i, n_blocks - 1 - i
            if keep:
                s_ref[rows(mf), part(0)] = fr.astype(BF16)
                s_ref[rows(mf), part(1)] = fi.astype(BF16)
                s_ref[rows(mb), part(2)] = br.astype(BF16)
                s_ref[rows(mb), part(3)] = bi.astype(BF16)
            zfr, zfi = z_ref[rows(mf), part(0)], z_ref[rows(mf), part(1)]
            zbr, zbi = z_ref[rows(mb), part(2)], z_ref[rows(mb), part(3)]
            fr, fi = afr * fr - afi * fi + zfr, afr * fi + afi * fr + zfi
            br, bi = abr * br - abi * bi + zbr, abr * bi + abi * br + zbi
        return fr, fi, br, bi

    zero = jnp.zeros((ns, pw), F32)
    ends = scan((zero, zero, zero, zero), keep=False)
    for k in range(4):
        e_ref[:, part(k)] = ends[k]
    row0 = jnp.zeros((1, pw), F32)
    for b in range(n_seq):
        for kr, ki, order in ((0, 1, range(SSM_SEGS)), (2, 3, range(SSM_SEGS - 1, -1, -1))):
            tr, ti = aseg[kr], aseg[ki]
            cr, ci = row0, row0
            for q in order:
                row = slice(b * SSM_SEGS + q, b * SSM_SEGS + q + 1)
                er, ei = e_ref[row, part(kr)], e_ref[row, part(ki)]
                e_ref[row, part(kr)] = cr
                e_ref[row, part(ki)] = ci
                cr, ci = tr * cr - ti * ci + er, tr * ci + ti * cr + ei
    scan(tuple(e_ref[:, part(k)] for k in range(4)), keep=True)

    h = h_ref[...]
    for rp in range(SSM_BLK):
        off = xw * (SSM_BLK - 1 - rp)
        t_ref[rp // 2, pl.ds((rp % 2) * qw + xoff, xw), :] = h[:, off:off + SSM_ROW].astype(BF16)

    y = lax.dot_general(s_ref[...], vt_ref[...], (((1,), (1,)), ((), ())), preferred_element_type=F32)
    for s2 in range(SSM_BLK // 2):
        y = y + jnp.dot(u2(s2), t_ref[s2], preferred_element_type=F32)
    y = _gelu(y).astype(BF16)

    rr = lax.broadcasted_iota(jnp.int32, (SSM_ROW, SSM_BLK * qw), 0)
    cc = lax.broadcasted_iota(jnp.int32, (SSM_ROW, SSM_BLK * qw), 1)
    place = jnp.where(cc == (rr // xw) * qw + pair_in_quad * xw + rr % xw, 1.0, 0.0).astype(BF16)
    y = jnp.dot(y, place, preferred_element_type=F32)

    @pl.when(pair_in_quad == 0)
    def _():
        o_ref[...] = jnp.zeros_like(o_ref)

    for step in range(SSM_BLK):
        o_ref[step] += y[:, step * qw:(step + 1) * qw].astype(o_ref.dtype)


def _ssm_tables(lam_re, lam_im, log_dt, b_re, b_im, c_re, c_im, d_skip, seg_len):
    g, c, r, pg = SSM_GROUPS, SSM_GROUP, SSM_BLK, SSM_PG
    npair = g // pg
    lre, lim = lam_re.astype(F32), lam_im.astype(F32)
    dt = jnp.exp(log_dt.astype(F32))[..., None]

    def cexp(xr, xi):
        e = jnp.exp(xr)
        return e * jnp.cos(xi), e * jnp.sin(xi)

    def cmul(ar, ai, br, bi):
        return ar * br - ai * bi, ar * bi + ai * br

    steps = jnp.arange(r + 1, dtype=F32)[None, :, None, None]
    pr, pi = cexp((lre * dt)[:, None] * steps, (lim * dt)[:, None] * steps)
    den = lre * lre + lim * lim
    qr = ((pr[:, 1] - 1.0) * lre + pi[:, 1] * lim) / den
    qi = (pi[:, 1] * lre - (pr[:, 1] - 1.0) * lim) / den
    bbr, bbi = cmul(qr[..., None], qi[..., None], b_re.astype(F32), b_im.astype(F32))
    ccr, cci = c_re.astype(F32), c_im.astype(F32)

    def per_group(xr, xi):
        y = jnp.stack([xr, xi], axis=1)
        return jnp.transpose(y, (3, 0, 1, 2, 4)).reshape(g, 4, xr.shape[1], xr.shape[3])

    pw4 = per_group(pr, pi)
    bt4 = per_group(jnp.transpose(bbr, (0, 3, 1, 2)), jnp.transpose(bbi, (0, 3, 1, 2)))
    ct4 = per_group(jnp.transpose(ccr, (0, 2, 1, 3)), jnp.transpose(cci, (0, 2, 1, 3)))

    xr, xi = cmul(ccr[:, :, None], cci[:, :, None], jnp.transpose(pr[:, :r], (0, 2, 1, 3))[:, :, :, None],
                  jnp.transpose(pi[:, :r], (0, 2, 1, 3))[:, :, :, None])
    klag = jnp.einsum('zgkcq,zgqd->zgkdc', jnp.concatenate([xr, -xi], axis=-1),
                      jnp.concatenate([bbr, bbi], axis=2))
    kf, kb = klag[0], klag[1]
    lag0 = kf[:, 0] + kb[:, 0] + jnp.eye(c, dtype=F32) * d_skip.astype(F32).reshape(g, 1, c)
    lagged = jnp.concatenate([kb[:, :0:-1], lag0[:, None], kf[:, 1:]], axis=1)
    lagged = lagged.reshape(npair, pg, 2 * r - 1, c, c)
    h = jnp.einsum('aindc,ij->aidnjc', lagged, jnp.eye(pg, dtype=F32)).reshape(npair, pg * c, 2 * r - 1, pg * c)
    h = jnp.pad(h, ((0, 0), (0, 0), (0, 1), (0, 0))).reshape(npair, pg * c, 2 * r * pg * c)

    sr, si = cexp(lre * dt * seg_len, lim * dt * seg_len)
    a_seg = per_group(sr[:, None], si[:, None])
    return pw4, bt4, ct4, h, a_seg


def ssm_branch(proj, layer, tabs, w_glu, batch, seq):
    pw4, bt4, ct4, h, a_seg = tabs
    seg_len = seq // SSM_SEGS
    n_blocks = seg_len // SSM_BLK
    ns = batch * SSM_SEGS
    m = batch * seq
    nquad = SSM_GROUPS // (SSM_PG * SSM_QUAD)
    qw = SSM_QUAD * SSM_PG * SSM_GROUP
    rows = n_blocks * ns
    u = proj[:, U_OFF:U_OFF + SSM_W].reshape(batch, SSM_SEGS, n_blocks, SSM_BLK, nquad, qw)
    u = jnp.transpose(u, (4, 3, 2, 0, 1, 5)).reshape(nquad, SSM_BLK, rows, qw)

    quad = pl.BlockSpec((None, SSM_BLK, rows, qw), lambda q, j: (q, 0, 0, 0))

    def per_group(a):
        return pl.BlockSpec((SSM_PG,) + a.shape[1:], lambda q, j: (q * SSM_QUAD + j, 0, 0, 0))

    y = pl.pallas_call(
        functools.partial(_ssm_kernel, n_seq=batch, n_blocks=n_blocks),
        grid=(nquad, SSM_QUAD),
        in_specs=[quad, per_group(pw4), per_group(bt4), per_group(ct4),
                  pl.BlockSpec((None,) + h.shape[1:], lambda q, j: (q * SSM_QUAD + j, 0, 0)), per_group(a_seg)],
        out_specs=quad,
        out_shape=jax.ShapeDtypeStruct((nquad, SSM_BLK, rows, qw), BF16),
        scratch_shapes=[pltpu.VMEM((rows, 4 * SSM_PW), F32), pltpu.VMEM((rows, 4 * SSM_PW), BF16),
                        pltpu.VMEM((ns, 4 * SSM_PW), F32), pltpu.VMEM((SSM_BLK // 2, 2 * qw, SSM_ROW), BF16),
                        pltpu.VMEM((SSM_BLK // 2, 2 * qw, 4 * SSM_PW), BF16),
                        pltpu.VMEM((SSM_ROW, 4 * SSM_PW), BF16)],
        compiler_params=_params("arbitrary", "arbitrary"), name="ssm")(u, pw4, bt4, ct4, h, a_seg)
    y = y.reshape(nquad, SSM_BLK, n_blocks, batch, SSM_SEGS, qw)
    y = jnp.transpose(y, (3, 4, 2, 1, 0, 5)).reshape(m, SSM_W)
    return fused_mm(m, SSM_W, [(y, 0, SSM_W, w_glu, (layer,), 0)], [(y, 0)],
                    lambda dts, ex: ex[0].astype(F32) * _sigmoid(dts[0]), BF16, tm=1024, tn=512, name="ssm_glu")


GMLP_NC = 4


def _gmlp_kernel(zu0_ref, zu1_ref, zv0_ref, zv1_ref, g_ref, ws_ref, bb_ref, o_ref):
    half = MLP_W // 2
    zv = jnp.concatenate([zv0_ref[...], zv1_ref[...]], axis=-1).astype(F32)
    v = _gelu(zv)
    v = (v * lax.rsqrt(jnp.mean(v * v, axis=-1, keepdims=True) + EPS) * g_ref[...]).astype(BF16)
    for g in range(MLP_GROUPS):
        cs = slice(g * MLP_GROUP_W, (g + 1) * MLP_GROUP_W)
        vg = jnp.concatenate([v[n * MLP_CHUNK:(n + 1) * MLP_CHUNK, cs] for n in range(GMLP_NC)], axis=-1)
        s = jnp.dot(ws_ref[g].astype(BF16), vg, preferred_element_type=F32)
        zu_ref = zu0_ref if g * MLP_GROUP_W < half else zu1_ref
        us = slice((g * MLP_GROUP_W) % half, (g * MLP_GROUP_W) % half + MLP_GROUP_W)
        for n in range(GMLP_NC):
            rs = slice(n * MLP_CHUNK, (n + 1) * MLP_CHUNK)
            sn = s[:, n * MLP_GROUP_W:(n + 1) * MLP_GROUP_W] + bb_ref[g]
            o_ref[rs, cs] = (_gelu(zu_ref[rs, us].astype(F32)) * sn).astype(o_ref.dtype)


def gmlp_branch(proj, layer, v_gain, w_s, b_bcast, m):
    rows = GMLP_NC * MLP_CHUNK
    half = MLP_W // 2

    def zspec(off):
        return pl.BlockSpec((rows, half), lambda i, off=off: (i, off // half))

    return pl.pallas_call(
        _gmlp_kernel, grid=(m // rows,),
        in_specs=[zspec(ZU_OFF), zspec(ZU_OFF + half), zspec(ZV_OFF), zspec(ZV_OFF + half),
                  pl.BlockSpec((None, 1, MLP_W), lambda i: (layer, 0, 0)),
                  pl.BlockSpec((None, MLP_GROUPS, MLP_CHUNK, MLP_CHUNK), lambda i: (layer, 0, 0, 0)),
                  pl.BlockSpec((None, MLP_GROUPS, MLP_CHUNK, MLP_GROUP_W), lambda i: (layer, 0, 0, 0))],
        out_specs=pl.BlockSpec((rows, MLP_W), lambda i: (i, 0)),
        out_shape=jax.ShapeDtypeStruct((m, MLP_W), BF16),
        compiler_params=_params("parallel"), name="gmlp",
    )(proj, proj, proj, proj, v_gain, w_s, b_bcast)


DENSE_TM = 1024
DENSE_TF = 512
MOE_TM = 1280
MOE_TF = 512
MOE_FC = 256
FFN_SUB = 256


def _ffn_kernel(ue_ref, un_ref, ub_ref, x_ref, w1_ref, w3_ref, w2_ref, o_ref, *, live_counts, fc):
    del ue_ref, ub_ref
    u = pl.program_id(0)
    j = pl.program_id(1)
    nsub = un_ref[u]

    @pl.when(j == 0)
    def _():
        o_ref[...] = jnp.zeros_like(o_ref)

    tf = w2_ref.shape[0]
    for k in live_counts:
        @pl.when(nsub == k)
        def _(k=k):
            rows = k * FFN_SUB
            x = x_ref[0:rows, :]
            for f0 in range(0, tf, fc):
                fs = slice(f0, f0 + fc)
                h1 = jnp.dot(x, w1_ref[:, fs].astype(BF16), preferred_element_type=F32)
                h3 = jnp.dot(x, w3_ref[:, fs].astype(BF16), preferred_element_type=F32)
                act = (_silu(h1) * h3).astype(BF16)
                o_ref[0:rows, :] += jnp.dot(act, w2_ref[fs, :].astype(BF16), preferred_element_type=F32)


def swiglu_ffn(x, unit_expert, unit_nsub, unit_block, w1, w3, w2, lead, live_counts, tm, tf, fc):
    rows, d = x.shape
    f = w1.shape[-1]
    n_units = rows // tm
    nf = f // tf
    nl = len(lead)

    def wmap_up(u, j, ue, un, ub):
        return tuple(lead) + (ue[u], 0, jnp.where(un[u] > 0, j, nf - 1))

    def wmap_down(u, j, ue, un, ub):
        return tuple(lead) + (ue[u], jnp.where(un[u] > 0, j, nf - 1), 0)

    def rmap(u, j, ue, un, ub):
        return (ub[u], 0)

    grid_spec = pltpu.PrefetchScalarGridSpec(
        num_scalar_prefetch=3, grid=(n_units, nf),
        in_specs=[pl.BlockSpec((tm, d), rmap),
                  pl.BlockSpec((None,) * (nl + 1) + (d, tf), wmap_up),
                  pl.BlockSpec((None,) * (nl + 1) + (d, tf), wmap_up),
                  pl.BlockSpec((None,) * (nl + 1) + (tf, d), wmap_down)],
        out_specs=pl.BlockSpec((tm, d), lambda u, j, ue, un, ub: (u, 0)))
    return pl.pallas_call(
        functools.partial(_ffn_kernel, live_counts=tuple(live_counts), fc=fc), grid_spec=grid_spec,
        out_shape=jax.ShapeDtypeStruct((rows, d), F32),
        compiler_params=_params("arbitrary", "arbitrary"), name="swiglu_ffn",
    )(unit_expert, unit_nsub, unit_block, x, w1, w3, w2)


def _router_kernel(x_ref, g_ref, wr_ref, h_ref, r_ref):
    x = x_ref[...]
    h = x * lax.rsqrt(jnp.mean(x * x, axis=-1, keepdims=True) + EPS) * g_ref[...]
    h_ref[...] = h
    logits = jnp.dot(h, wr_ref[...], preferred_element_type=F32, precision=lax.Precision.HIGHEST)
    lane = lax.broadcasted_iota(jnp.int32, logits.shape, 1)
    neg = jnp.float32(-jnp.inf)
    logits = jnp.where(lane < N_EXPERTS, logits, neg)
    m1 = jnp.max(logits, axis=-1, keepdims=True)
    i1 = jnp.min(jnp.where(logits == m1, lane, LANES), axis=-1, keepdims=True)
    rest = jnp.where(lane == i1, neg, logits)
    m2 = jnp.max(rest, axis=-1, keepdims=True)
    i2 = jnp.min(jnp.where(rest == m2, lane, LANES), axis=-1, keepdims=True)
    e = jnp.exp(m2 - m1)
    g1 = 1.0 / (1.0 + e)
    g2 = e / (1.0 + e)
    r_ref[...] = jnp.where(lane == 0, i1.astype(F32),
                           jnp.where(lane == 1, i2.astype(F32),
                                     jnp.where(lane == 2, g1, jnp.where(lane == 3, g2, 0.0))))


def norm_router(x, gains, layer, w_router_pad, tm=256):
    m, d = x.shape
    row = pl.BlockSpec((tm, d), lambda i: (i, 0))
    return pl.pallas_call(
        _router_kernel, grid=(m // tm,),
        in_specs=[row, pl.BlockSpec((None, 1, d), lambda i: (layer, 0, 0)),
                  pl.BlockSpec((d, LANES), lambda i: (0, 0))],
        out_specs=[row, pl.BlockSpec((tm, LANES), lambda i: (i, 0))],
        out_shape=[jax.ShapeDtypeStruct((m, d), F32), jax.ShapeDtypeStruct((m, LANES), F32)],
        compiler_params=_params("parallel"), name="norm_router")(x, gains, w_router_pad)


DMA_GROUP = 8


def _gather_kernel(src_ref, nv_ref, h_ref, o_ref, buf_ref, sem):
    i = pl.program_id(0)
    tb = buf_ref.shape[0]
    base = i * tb
    nv = nv_ref[i]
    ngroups = nv // DMA_GROUP

    @pl.when(i == 0)
    def _():
        buf_ref[...] = jnp.zeros_like(buf_ref)

    def row_copy(t):
        return pltpu.make_async_copy(h_ref.at[pl.ds(src_ref[base + t], 1)], buf_ref.at[pl.ds(t, 1)], sem)

    def group_copy():
        return pltpu.make_async_copy(h_ref.at[pl.ds(0, DMA_GROUP)], buf_ref.at[pl.ds(0, DMA_GROUP)], sem)

    def issue_group(c, carry):
        for r in range(DMA_GROUP):
            row_copy(c * DMA_GROUP + r).start()
        return carry

    def issue_row(t, carry):
        row_copy(t).start()
        return carry

    def wait_group(c, carry):
        group_copy().wait()
        return carry

    def wait_row(t, carry):
        row_copy(t).wait()
        return carry

    lax.fori_loop(0, ngroups, issue_group, 0)
    lax.fori_loop(ngroups * DMA_GROUP, nv, issue_row, 0)
    lax.fori_loop(0, ngroups, wait_group, 0)
    lax.fori_loop(ngroups * DMA_GROUP, nv, wait_row, 0)
    row = lax.broadcasted_iota(jnp.int32, o_ref.shape, 0)
    o_ref[...] = jnp.where(row < nv, buf_ref[...], 0.0).astype(o_ref.dtype)


def moe_gather(h, src, n_valid, cap, tb):
    _, d = h.shape
    grid_spec = pltpu.PrefetchScalarGridSpec(
        num_scalar_prefetch=2, grid=(cap // tb,),
        in_specs=[pl.BlockSpec(memory_space=pl.ANY)],
        out_specs=pl.BlockSpec((tb, d), lambda i, s, n: (i, 0)),
        scratch_shapes=[pltpu.VMEM((tb, d), h.dtype), pltpu.SemaphoreType.DMA(())])
    return pl.pallas_call(
        _gather_kernel, grid_spec=grid_spec,
        out_shape=jax.ShapeDtypeStruct((cap, d), BF16),
        compiler_params=_params("arbitrary"), name="moe_gather")(src, n_valid, h)


COMBINE_TB = 256


def _combine_kernel(dest_ref, x_ref, r_ref, g_ref, y_ref, xo_ref, h_ref, buf_ref, sem):
    base = pl.program_id(0) * COMBINE_TB

    def row_copy(t, k):
        return pltpu.make_async_copy(y_ref.at[pl.ds(dest_ref[TOP_K * (base + t) + k], 1)],
                                     buf_ref.at[k, pl.ds(t, 1)], sem)

    def issue_group(c, carry):
        for r in range(DMA_GROUP // TOP_K):
            for k in range(TOP_K):
                row_copy(c * (DMA_GROUP // TOP_K) + r, k).start()
        return carry

    lax.fori_loop(0, COMBINE_TB * TOP_K // DMA_GROUP, issue_group, 0)
    for k in range(TOP_K):
        pltpu.make_async_copy(y_ref.at[pl.ds(0, COMBINE_TB)], buf_ref.at[k], sem).wait()
    r = r_ref[...]
    g1 = r[:, 2:3]
    g2 = r[:, 3:4]
    x = x_ref[...] + (buf_ref[0] * g1 + buf_ref[1] * g2)
    xo_ref[...] = x
    y = x * lax.rsqrt(jnp.mean(x * x, axis=-1, keepdims=True) + EPS)
    h_ref[...] = (y * g_ref[...]).astype(h_ref.dtype)


def moe_combine(x, route, dest, ybuf, gains, layer):
    m, d = x.shape
    row = pl.BlockSpec((COMBINE_TB, d), lambda i, dr: (i, 0))
    grid_spec = pltpu.PrefetchScalarGridSpec(
        num_scalar_prefetch=1, grid=(m // COMBINE_TB,),
        in_specs=[row, pl.BlockSpec((COMBINE_TB, LANES), lambda i, dr: (i, 0)),
                  pl.BlockSpec((None, 1, d), lambda i, dr: (layer, 0, 0)),
                  pl.BlockSpec(memory_space=pl.ANY)],
        out_specs=[row, row],
        scratch_shapes=[pltpu.VMEM((TOP_K, COMBINE_TB, d), F32), pltpu.SemaphoreType.DMA(())])
    return pl.pallas_call(
        _combine_kernel, grid_spec=grid_spec,
        out_shape=[jax.ShapeDtypeStruct((m, d), F32), jax.ShapeDtypeStruct((m, d), BF16)],
        compiler_params=_params("arbitrary"), name="moe_combine")(dest, x, route, gains, ybuf)


def moe_layer(x1, ffn_norm, ple_norm, layer, w_router, e_w1, e_w3, e_w2, j):
    m, d = x1.shape
    wr_pad = jnp.zeros((d, LANES), F32).at[:, :N_EXPERTS].set(w_router)
    h, route = norm_router(x1, ffn_norm, layer, wr_pad)
    e_flat = route[:, :TOP_K].astype(jnp.int32).reshape(-1)
    onehot = (e_flat[:, None] == jnp.arange(N_EXPERTS)[None, :]).astype(jnp.int32)
    csum = jnp.cumsum(onehot, axis=0)
    rank = jnp.sum((csum - onehot) * onehot, axis=1)
    counts = csum[-1]
    tm = MOE_TM
    n_units_e = (counts + tm - 1) // tm
    unit_end = jnp.cumsum(n_units_e)
    unit_start = unit_end - n_units_e
    dest = (unit_start[e_flat] * tm + rank).astype(jnp.int32)
    n_units = (m * TOP_K) // tm + N_EXPERTS
    uidx = jnp.arange(n_units)
    ue = jnp.minimum(jnp.searchsorted(unit_end, uidx, side='right'), N_EXPERTS - 1).astype(jnp.int32)
    live_rows = jnp.clip(counts[ue] - (uidx - unit_start[ue]) * tm, 0, tm)
    live_rows = jnp.where(uidx < unit_end[-1], live_rows, 0)
    un = ((live_rows + FFN_SUB - 1) // FFN_SUB).astype(jnp.int32)
    last_live = jnp.maximum(unit_end[-1] - 1, 0)
    ue = jnp.where(uidx < unit_end[-1], ue, ue[last_live]).astype(jnp.int32)
    ub = jnp.where(uidx < unit_end[-1], uidx, last_live).astype(jnp.int32)

    cap = n_units * tm
    src = jnp.zeros((cap,), jnp.int32).at[dest].set(jnp.arange(m * TOP_K, dtype=jnp.int32) // TOP_K)

    xbuf = moe_gather(h, src, live_rows.astype(jnp.int32), cap, tm)
    ybuf = swiglu_ffn(xbuf, ue, un, ub, e_w1, e_w3, e_w2, (j,), range(1, tm // FFN_SUB + 1), tm, MOE_TF,
                      MOE_FC)
    return moe_combine(x1, route, dest, ybuf, ple_norm, layer)


def kernel(x, p, mix_norm, w_in, q_norm, k_norm, ssm_lambda_re, ssm_lambda_im, ssm_log_dt, ssm_b_re, ssm_b_im,
           ssm_c_re, ssm_c_im, ssm_d, ssm_glu_w, gmlp_v_norm, gmlp_ws, gmlp_b, w_branch, w_out, ffn_norm,
           dense_w1, dense_w3, dense_w2, router_w, expert_w1, expert_w3, expert_w2, ple_norm, ple_gate_w,
           ple_proj_w):
    batch, seq, d = x.shape
    depth = w_in.shape[0]
    n_in = w_in.shape[-1]
    m = batch * seq
    xs = x.reshape(m, d)
    rope_c, rope_s = rope_tables(seq)

    def g3(a):
        return a.reshape(a.shape[0], 1, a.shape[1])

    mix_norm, q_norm, k_norm, gmlp_v_norm, ffn_norm, ple_norm = map(
        g3, (mix_norm, q_norm, k_norm, gmlp_v_norm, ffn_norm, ple_norm))
    b_bcast = jnp.broadcast_to(gmlp_b[..., None], gmlp_b.shape + (MLP_GROUP_W,))
    p2 = p.reshape(depth, m, p.shape[-1])
    dense_units = m // DENSE_TM
    dense_un = jnp.full((dense_units,), DENSE_TM // FFN_SUB, jnp.int32)

    for i in range(depth):
        _, h = add_norm(xs, None, mix_norm, i)
        proj = fused_mm(m, n_in, [(h, 0, d, w_in, (i,), 0)], [], lambda dts, ex: dts[0], BF16,
                        tm=min(2048, m), tn=768, name="in_proj")
        attn = attention(proj, q_norm, k_norm, i, rope_c, rope_s, batch, seq)
        tabs = _ssm_tables(ssm_lambda_re[i], ssm_lambda_im[i], ssm_log_dt[i], ssm_b_re[i], ssm_b_im[i],
                           ssm_c_re[i], ssm_c_im[i], ssm_d[i], seq // SSM_SEGS)
        ssm = ssm_branch(proj, i, tabs, ssm_glu_w, batch, seq)
        mlp = gmlp_branch(proj, i, gmlp_v_norm, gmlp_ws, b_bcast, m)

        tn = 512
        merged = fused_mm(
            m, d,
            [(br, 0, br.shape[1], w_branch, (i, n), 0) for n, br in enumerate((attn, ssm, mlp))],
            [(proj, (GATE_OFF + n * d) // tn) for n in range(N_BRANCH)],
            lambda dts, ex: sum(_sigmoid(e.astype(F32)) * dt for e, dt in zip(ex, dts)),
            BF16, tm=1024, tn=tn, name="branch_merge")
        x1 = fused_mm(m, d, [(merged, 0, d, w_out, (i,), 0)], [(xs, 0)],
                      lambda dts, ex: ex[0] + dts[0], F32, tm=min(2048, m), tn=512, name="out_proj")

        j = i // 2
        if i % 2 == 0:
            _, h2 = add_norm(x1, None, ffn_norm, i)
            dense_ue = jnp.full((dense_units,), j, jnp.int32)
            y = swiglu_ffn(h2, dense_ue, dense_un, jnp.arange(dense_units, dtype=jnp.int32),
                           dense_w1, dense_w3, dense_w2, (), (DENSE_TM // FFN_SUB,), DENSE_TM, DENSE_TF,
                           DENSE_TF)
            x2, hn = add_norm(x1, y, ple_norm, i)
        else:
            x2, hn = moe_layer(x1, ffn_norm, ple_norm, i, router_w[j], expert_w1, expert_w3, expert_w2, j)

        xs = fused_mm(
            m, d,
            [(hn, 0, d, ple_gate_w, (i,), 0), ((p2, (i,)), 0, p.shape[-1], ple_proj_w, (i,), 0)],
            [(x2, 0)],
            lambda dts, ex: ex[0] + _sigmoid(dts[0]) * dts[1], F32, tm=min(2048, m), tn=512, name="ple")
    return xs.reshape(batch, seq, d)
```

```python
import functools
import math

import jax
import jax.numpy as jnp
import numpy as np
from jax import lax
from jax.experimental import pallas as pl
from jax.experimental.pallas import tpu as pltpu

GRID_W = 64
ROPE_THETA = 10000.0
HEAD_DIM = 128
N_Q_HEADS = 8
N_KV_HEADS = 2
ATTN_W = N_Q_HEADS * HEAD_DIM
KV_W = N_KV_HEADS * HEAD_DIM
SSM_GROUP = 16
SSM_GROUPS = 64
SSM_W = SSM_GROUP * SSM_GROUPS
SSM_STATE = 64
MLP_CHUNK = 128
MLP_GROUPS = 8
MLP_GROUP_W = 128
MLP_W = MLP_GROUPS * MLP_GROUP_W
N_BRANCH = 3
N_EXPERTS = 8
TOP_K = 2
EPS = 1e-6

Q_OFF = 0
K_OFF = ATTN_W
V_OFF = K_OFF + KV_W
U_OFF = V_OFF + KV_W
ZU_OFF = U_OFF + SSM_W
ZV_OFF = ZU_OFF + MLP_W
GATE_OFF = ZV_OFF + MLP_W

V7X_VMEM_LIMIT_BYTES = 62 * 1024 * 1024
LANES = 128

BF16 = jnp.bfloat16
F32 = jnp.float32


def _params(*sem):
    return pltpu.CompilerParams(dimension_semantics=sem, vmem_limit_bytes=V7X_VMEM_LIMIT_BYTES)


def _gelu(x):
    c = math.sqrt(2.0 / math.pi)
    return 0.5 * x * (1.0 + jnp.tanh(c * (x + 0.044715 * (x * x * x))))


def _sigmoid(x):
    return 1.0 / (1.0 + jnp.exp(-x))


def _silu(x):
    return x * _sigmoid(x)


def _add_norm_kernel(*refs, has_delta):
    if has_delta:
        x_ref, d_ref, g_ref, xo_ref, h_ref = refs
        x = x_ref[...] + d_ref[...]
        xo_ref[...] = x
    else:
        x_ref, g_ref, h_ref = refs
        x = x_ref[...]
    y = x * lax.rsqrt(jnp.mean(x * x, axis=-1, keepdims=True) + EPS)
    h_ref[...] = (y * g_ref[...]).astype(h_ref.dtype)


def add_norm(x, delta, gains, layer, tm=512):
    m, d = x.shape
    row = pl.BlockSpec((tm, d), lambda i: (i, 0))
    gspec = pl.BlockSpec((None, 1, d), lambda i: (layer, 0, 0))
    if delta is None:
        h = pl.pallas_call(
            functools.partial(_add_norm_kernel, has_delta=False),
            grid=(m // tm,), in_specs=[row, gspec], out_specs=row,
            out_shape=jax.ShapeDtypeStruct((m, d), BF16),
            compiler_params=_params("parallel"), name="norm")(x, gains)
        return x, h
    xo, h = pl.pallas_call(
        functools.partial(_add_norm_kernel, has_delta=True),
        grid=(m // tm,), in_specs=[row, row, gspec], out_specs=[row, row],
        out_shape=[jax.ShapeDtypeStruct((m, d), F32), jax.ShapeDtypeStruct((m, d), BF16)],
        compiler_params=_params("parallel"), name="add_norm")(x, delta, gains)
    return xo, h


def _fused_mm_kernel(*refs, n_dots, n_extras, epilogue):
    a_refs = refs[:n_dots]
    w_refs = refs[n_dots:2 * n_dots]
    e_refs = refs[2 * n_dots:2 * n_dots + n_extras]
    o_ref = refs[2 * n_dots + n_extras]
    wb_refs = refs[2 * n_dots + n_extras + 1:]

    @pl.when(pl.program_id(1) == 0)
    def _():
        for w_ref, wb_ref in zip(w_refs, wb_refs):
            wb_ref[...] = w_ref[...].astype(BF16)

    dots = []
    for a_ref, wb_ref in zip(a_refs, wb_refs):
        a = a_ref[...]
        if a.dtype != BF16:
            a = a.astype(BF16)
        dots.append(jnp.dot(a, wb_ref[...], preferred_element_type=F32))
    extras = [e_ref[...] for e_ref in e_refs]
    o_ref[...] = epilogue(dots, extras).astype(o_ref.dtype)


def fused_mm(m, n, dots, extras, epilogue, out_dtype, tm, tn, name):
    in_specs, args, scratch = [], [], []
    for a, acb, k, _, _, _ in dots:
        a, a_lead = a if isinstance(a, tuple) else (a, ())
        in_specs.append(pl.BlockSpec((None,) * len(a_lead) + (tm, k),
                                     lambda j, i, acb=acb, a_lead=tuple(a_lead): a_lead + (i, acb)))
        args.append(a)
    for _, _, k, w, lead, off in dots:
        in_specs.append(pl.BlockSpec((None,) * len(lead) + (k, tn),
                                     lambda j, i, lead=tuple(lead), off=off: lead + (0, off + j)))
        args.append(w)
        scratch.append(pltpu.VMEM((k, tn), BF16))
    for e, off in extras:
        in_specs.append(pl.BlockSpec((tm, tn), lambda j, i, off=off: (i, off + j)))
        args.append(e)
    return pl.pallas_call(
        functools.partial(_fused_mm_kernel, n_dots=len(dots), n_extras=len(extras), epilogue=epilogue),
        grid=(n // tn, m // tm), in_specs=in_specs,
        out_specs=pl.BlockSpec((tm, tn), lambda j, i: (i, j)),
        out_shape=jax.ShapeDtypeStruct((m, n), out_dtype),
        scratch_shapes=scratch,
        compiler_params=_params("arbitrary", "arbitrary"), name=name)(*args)


def _rope(x, c, s):
    lane = lax.broadcasted_iota(jnp.int32, x.shape, x.ndim - 1)
    quarter = HEAD_DIM // 4
    partner = jnp.where((lane % (2 * quarter)) < quarter,
                        pltpu.roll(x, HEAD_DIM - quarter, x.ndim - 1),
                        pltpu.roll(x, quarter, x.ndim - 1))
    return x * c + partner * s


def _head_norm(x, g):
    return x * lax.rsqrt(jnp.mean(x * x, axis=-1, keepdims=True) + EPS) * g


def _attn_kernel(q_ref, k_ref, v_ref, cq_ref, sq_ref, ck_ref, sk_ref, qg_ref, kg_ref, o_ref, ks_ref, vs_ref, *,
                 rep):
    @pl.when(pl.program_id(2) == 0)
    def _():
        k = _head_norm(k_ref[...].astype(F32), kg_ref[...])
        ks_ref[...] = _rope(k, ck_ref[...], sk_ref[...]).astype(BF16)
        vs_ref[:, 0:HEAD_DIM] = v_ref[...]
        vs_ref[:, HEAD_DIM:2 * HEAD_DIM] = jnp.ones((v_ref.shape[0], HEAD_DIM), BF16)

    scale = HEAD_DIM ** -0.5 * math.log2(math.e)
    cq = cq_ref[...]
    sq = sq_ref[...]
    for hh in range(rep):
        sl = slice(hh * HEAD_DIM, (hh + 1) * HEAD_DIM)
        q = _head_norm(q_ref[:, sl].astype(F32), qg_ref[...])
        q = (_rope(q, cq, sq) * scale).astype(BF16)
        s = lax.dot_general(q, ks_ref[...], (((1,), (1,)), ((), ())), preferred_element_type=F32)
        m = jnp.max(s, axis=-1, keepdims=True)
        p = jnp.exp2(s - m).astype(BF16)
        o = jnp.dot(p, vs_ref[...], preferred_element_type=F32)
        o_ref[:, sl] = (o[:, 0:HEAD_DIM] / o[:, HEAD_DIM:HEAD_DIM + 1]).astype(o_ref.dtype)


def attention(proj, q_gain, k_gain, layer, rope_c, rope_s, batch, seq, tq=512):
    rep = N_Q_HEADS // N_KV_HEADS
    qw = rep * HEAD_DIM
    nq = seq // tq
    gspec = pl.BlockSpec((None, 1, HEAD_DIM), lambda b, g, i: (layer, 0, 0))
    return pl.pallas_call(
        functools.partial(_attn_kernel, rep=rep),
        grid=(batch, N_KV_HEADS, nq),
        in_specs=[
            pl.BlockSpec((tq, qw), lambda b, g, i: (b * nq + i, Q_OFF // qw + g)),
            pl.BlockSpec((seq, HEAD_DIM), lambda b, g, i: (b, K_OFF // HEAD_DIM + g)),
            pl.BlockSpec((seq, HEAD_DIM), lambda b, g, i: (b, V_OFF // HEAD_DIM + g)),
            pl.BlockSpec((tq, HEAD_DIM), lambda b, g, i: (i, 0)),
            pl.BlockSpec((tq, HEAD_DIM), lambda b, g, i: (i, 0)),
            pl.BlockSpec((seq, HEAD_DIM), lambda b, g, i: (0, 0)),
            pl.BlockSpec((seq, HEAD_DIM), lambda b, g, i: (0, 0)),
            gspec, gspec,
        ],
        out_specs=pl.BlockSpec((tq, qw), lambda b, g, i: (b * nq + i, g)),
        out_shape=jax.ShapeDtypeStruct((batch * seq, ATTN_W), BF16),
        scratch_shapes=[pltpu.VMEM((seq, HEAD_DIM), BF16), pltpu.VMEM((seq, 2 * HEAD_DIM), BF16)],
        compiler_params=_params("arbitrary", "arbitrary", "arbitrary"), name="attention",
    )(proj, proj, proj, rope_c, rope_s, rope_c, rope_s, q_gain, k_gain)


def rope_tables(seq):
    rows = seq // GRID_W
    t = np.arange(seq)
    pos = np.stack([t // GRID_W - rows // 2, t % GRID_W - GRID_W // 2], axis=-1).astype(np.float32)
    n_freq = HEAD_DIM // 4
    inv_freq = np.float32(ROPE_THETA) ** (-np.arange(n_freq, dtype=np.float32) / np.float32(n_freq))
    ang = pos[:, :, None] * inv_freq.astype(np.float32)
    cos, sin = np.cos(ang).astype(np.float32), np.sin(ang).astype(np.float32)
    c = np.concatenate([cos[:, 0], cos[:, 0], cos[:, 1], cos[:, 1]], axis=-1)
    s = np.concatenate([-sin[:, 0], sin[:, 0], -sin[:, 1], sin[:, 1]], axis=-1)
    return jnp.asarray(c), jnp.asarray(s)


SSM_SEGS = 8
SSM_BLK = 16
SSM_PG = 2
SSM_QUAD = 4
SSM_PPS = 4
SSM_PW = SSM_PG * SSM_STATE
SSM_ROW = SSM_BLK * SSM_PG * SSM_GROUP


def _ssm_kernel(u_ref, pw_ref, bt_ref, ct_ref, h_ref, at_ref, o_ref, acc_ref, *scratch, n_seq, n_blocks):
    @pl.when(pl.program_id(1) == 0)
    def _():
        acc_ref[...] = jnp.zeros_like(acc_ref)

    per_pair = len(scratch) // SSM_PPS
    for pp in range(SSM_PPS):
        groups = slice(pp * SSM_PG, (pp + 1) * SSM_PG)
        _ssm_pair(u_ref, pw_ref.at[groups], bt_ref.at[groups], ct_ref.at[groups], h_ref.at[pp], at_ref.at[groups],
                  acc_ref, *scratch[pp * per_pair:(pp + 1) * per_pair],
                  pair_in_quad=pl.program_id(1) * SSM_PPS + pp, n_seq=n_seq, n_blocks=n_blocks)

    @pl.when(pl.program_id(1) == SSM_QUAD // SSM_PPS - 1)
    def _():
        o_ref[...] = acc_ref[...].astype(o_ref.dtype)


def _ssm_pair(u_ref, pw_ref, bt_ref, ct_ref, h_ref, at_ref, acc_ref, z_ref, s_ref, e_ref, t_ref, bw_ref, vt_ref, *,
              pair_in_quad, n_seq, n_blocks):
    ns = n_seq * SSM_SEGS
    pw = SSM_PW
    xw = SSM_PG * SSM_GROUP
    qw = SSM_QUAD * xw
    xoff = pl.multiple_of(pair_in_quad * xw, xw)

    def part(k):
        return slice(k * pw, (k + 1) * pw)

    bw_ref[...] = jnp.zeros_like(bw_ref)
    t_ref[...] = jnp.zeros_like(t_ref)

    lane = lax.broadcasted_iota(jnp.int32, (SSM_GROUP, pw), 1)
    owns = [lane // SSM_STATE == gp for gp in range(SSM_PG)]

    def pair_lanes(ref):
        return jnp.concatenate([ref[gp] for gp in range(SSM_PG)], axis=-1)

    apow, bbar, cmat, aseg = pair_lanes(pw_ref), pair_lanes(bt_ref), pair_lanes(ct_ref), pair_lanes(at_ref)

    def build(m, d, k, neg_im):
        ar, ai = apow[2 * d, k:k + 1, :], apow[2 * d + 1, k:k + 1, :]
        mr, mi = m[2 * d], m[2 * d + 1]
        xr, xi = ar * mr - ai * mi, ar * mi + ai * mr
        if neg_im:
            xi = -xi
        return [(jnp.where(own, xr, 0.0).astype(BF16), jnp.where(own, xi, 0.0).astype(BF16)) for own in owns]

    for step in range(SSM_BLK):
        for d, k_bw, k_vt in ((0, SSM_BLK - 1 - step, step + 1), (1, step, SSM_BLK - step)):
            for gp, (xr, xi) in enumerate(build(bbar, d, k_bw, False)):
                rws = pl.ds((step % 2) * qw + xoff + gp * SSM_GROUP, SSM_GROUP)
                bw_ref[step // 2, rws, part(2 * d)] = xr
                bw_ref[step // 2, rws, part(2 * d + 1)] = xi
            for gp, (xr, xi) in enumerate(build(cmat, d, k_vt, True)):
                rws = slice(step * xw + gp * SSM_GROUP, step * xw + (gp + 1) * SSM_GROUP)
                vt_ref[rws, part(2 * d)] = xr
                vt_ref[rws, part(2 * d + 1)] = xi

    def u2(s2):
        return jnp.concatenate([u_ref[2 * s2], u_ref[2 * s2 + 1]], axis=-1)

    z = jnp.dot(u2(0), bw_ref[0], preferred_element_type=F32)
    for s2 in range(1, SSM_BLK // 2):
        z = z + jnp.dot(u2(s2), bw_ref[s2], preferred_element_type=F32)
    z_ref[...] = z
    afr, afi, abr, abi = [jnp.broadcast_to(apow[k, SSM_BLK:SSM_BLK + 1, :], (ns, pw)) for k in range(4)]

    def rows(blk):
        return slice(blk * ns, (blk + 1) * ns)

    def scan(state, keep):
        fr, fi, br, bi = state
        for i in range(n_blocks):
            mf, mb = i, n_blocks - 1 - i
            if keep:
                s_ref[rows(mf), part(0)] = fr.astype(BF16)
                s_ref[rows(mf), part(1)] = fi.astype(BF16)
                s_ref[rows(mb), part(2)] = br.astype(BF16)
                s_ref[rows(mb), part(3)] = bi.astype(BF16)
            zfr, zfi = z_ref[rows(mf), part(0)], z_ref[rows(mf), part(1)]
            zbr, zbi = z_ref[rows(mb), part(2)], z_ref[rows(mb), part(3)]
            fr, fi = afr * fr - afi * fi + zfr, afr * fi + afi * fr + zfi
            br, bi = abr * br - abi * bi + zbr, abr * bi + abi * br + zbi
        return fr, fi, br, bi

    zero = jnp.zeros((ns, pw), F32)
    ends = scan((zero, zero, zero, zero), keep=False)
    for k in range(4):
        e_ref[:, part(k)] = ends[k]
    row0 = jnp.zeros((1, pw), F32)
    for b in range(n_seq):
        for kr, ki, order in ((0, 1, range(SSM_SEGS)), (2, 3, range(SSM_SEGS - 1, -1, -1))):
            tr, ti = aseg[kr], aseg[ki]
            cr, ci = row0, row0
            for q in order:
                row = slice(b * SSM_SEGS + q, b * SSM_SEGS + q + 1)
                er, ei = e_ref[row, part(kr)], e_ref[row, part(ki)]
                e_ref[row, part(kr)] = cr
                e_ref[row, part(ki)] = ci
                cr, ci = tr * cr - ti * ci + er, tr * ci + ti * cr + ei
    scan(tuple(e_ref[:, part(k)] for k in range(4)), keep=True)

    h = h_ref[...]
    for rp in range(SSM_BLK):
        off = xw * (SSM_BLK - 1 - rp)
        t_ref[rp // 2, pl.ds((rp % 2) * qw + xoff, xw), :] = h[:, off:off + SSM_ROW].astype(BF16)

    y = lax.dot_general(s_ref[...], vt_ref[...], (((1,), (1,)), ((), ())), preferred_element_type=F32)
    for s2 in range(SSM_BLK // 2):
        y = y + jnp.dot(u2(s2), t_ref[s2], preferred_element_type=F32)
    y = _gelu(y)

    lane = lax.broadcasted_iota(jnp.int32, (y.shape[0], qw), 1)
    mine = lane // xw == pair_in_quad
    for step in range(SSM_BLK):
        tile = y[:, (step // SSM_QUAD) * qw:(step // SSM_QUAD + 1) * qw]
        shift = ((pair_in_quad + SSM_QUAD - step % SSM_QUAD) % SSM_QUAD) * xw
        acc_ref[step] = jnp.where(mine, pltpu.roll(tile, shift, 1), acc_ref[step])


def _ssm_tables(lam_re, lam_im, log_dt, b_re, b_im, c_re, c_im, d_skip, seg_len):
    g, c, r, pg = SSM_GROUPS, SSM_GROUP, SSM_BLK, SSM_PG
    npair = g // pg
    lre, lim = lam_re.astype(F32), lam_im.astype(F32)
    dt = jnp.exp(log_dt.astype(F32))[..., None]

    def cexp(xr, xi):
        e = jnp.exp(xr)
        return e * jnp.cos(xi), e * jnp.sin(xi)

    def cmul(ar, ai, br, bi):
        return ar * br - ai * bi, ar * bi + ai * br

    steps = jnp.arange(r + 1, dtype=F32)[None, :, None, None]
    pr, pi = cexp((lre * dt)[:, None] * steps, (lim * dt)[:, None] * steps)
    den = lre * lre + lim * lim
    qr = ((pr[:, 1] - 1.0) * lre + pi[:, 1] * lim) / den
    qi = (pi[:, 1] * lre - (pr[:, 1] - 1.0) * lim) / den
    bbr, bbi = cmul(qr[..., None], qi[..., None], b_re.astype(F32), b_im.astype(F32))
    ccr, cci = c_re.astype(F32), c_im.astype(F32)

    def per_group(xr, xi):
        y = jnp.stack([xr, xi], axis=1)
        return jnp.transpose(y, (3, 0, 1, 2, 4)).reshape(g, 4, xr.shape[1], xr.shape[3])

    pw4 = per_group(pr, pi)
    bt4 = per_group(jnp.transpose(bbr, (0, 3, 1, 2)), jnp.transpose(bbi, (0, 3, 1, 2)))
    ct4 = per_group(jnp.transpose(ccr, (0, 2, 1, 3)), jnp.transpose(cci, (0, 2, 1, 3)))

    xr, xi = cmul(ccr[:, :, None], cci[:, :, None], jnp.transpose(pr[:, :r], (0, 2, 1, 3))[:, :, :, None],
                  jnp.transpose(pi[:, :r], (0, 2, 1, 3))[:, :, :, None])
    klag = jnp.einsum('zgkcq,zgqd->zgkdc', jnp.concatenate([xr, -xi], axis=-1),
                      jnp.concatenate([bbr, bbi], axis=2))
    kf, kb = klag[0], klag[1]
    lag0 = kf[:, 0] + kb[:, 0] + jnp.eye(c, dtype=F32) * d_skip.astype(F32).reshape(g, 1, c)
    lagged = jnp.concatenate([kb[:, :0:-1], lag0[:, None], kf[:, 1:]], axis=1)
    lagged = lagged.reshape(npair, pg, 2 * r - 1, c, c)
    h = jnp.einsum('aindc,ij->aidnjc', lagged, jnp.eye(pg, dtype=F32)).reshape(npair, pg * c, 2 * r - 1, pg * c)
    h = jnp.pad(h, ((0, 0), (0, 0), (0, 1), (0, 0))).reshape(npair, pg * c, 2 * r * pg * c)

    sr, si = cexp(lre * dt * seg_len, lim * dt * seg_len)
    a_seg = per_group(sr[:, None], si[:, None])
    return pw4, bt4, ct4, h, a_seg


def ssm_branch(proj, layer, tabs, w_glu, batch, seq):
    pw4, bt4, ct4, h, a_seg = tabs
    seg_len = seq // SSM_SEGS
    n_blocks = seg_len // SSM_BLK
    ns = batch * SSM_SEGS
    m = batch * seq
    nquad = SSM_GROUPS // (SSM_PG * SSM_QUAD)
    qw = SSM_QUAD * SSM_PG * SSM_GROUP
    rows = n_blocks * ns
    u = proj[:, U_OFF:U_OFF + SSM_W].reshape(batch, SSM_SEGS, n_blocks, SSM_BLK, nquad, qw)
    u = jnp.transpose(u, (4, 3, 2, 0, 1, 5)).reshape(nquad, SSM_BLK, rows, qw)

    quad = pl.BlockSpec((None, SSM_BLK, rows, qw), lambda q, j: (q, 0, 0, 0))

    steps_per_quad = SSM_QUAD // SSM_PPS

    def per_group(a):
        return pl.BlockSpec((SSM_PPS * SSM_PG,) + a.shape[1:], lambda q, j: (q * steps_per_quad + j, 0, 0, 0))

    pair_scratch = [pltpu.VMEM((rows, 4 * SSM_PW), F32), pltpu.VMEM((rows, 4 * SSM_PW), BF16),
                    pltpu.VMEM((ns, 4 * SSM_PW), F32), pltpu.VMEM((SSM_BLK // 2, 2 * qw, SSM_ROW), BF16),
                    pltpu.VMEM((SSM_BLK // 2, 2 * qw, 4 * SSM_PW), BF16), pltpu.VMEM((SSM_ROW, 4 * SSM_PW), BF16)]
    y = pl.pallas_call(
        functools.partial(_ssm_kernel, n_seq=batch, n_blocks=n_blocks),
        grid=(nquad, steps_per_quad),
        in_specs=[quad, per_group(pw4), per_group(bt4), per_group(ct4),
                  pl.BlockSpec((SSM_PPS,) + h.shape[1:], lambda q, j: (q * steps_per_quad + j, 0, 0)),
                  per_group(a_seg)],
        out_specs=quad,
        out_shape=jax.ShapeDtypeStruct((nquad, SSM_BLK, rows, qw), BF16),
        scratch_shapes=[pltpu.VMEM((SSM_BLK, rows, qw), F32)] + pair_scratch * SSM_PPS,
        compiler_params=_params("arbitrary", "arbitrary"), name="ssm")(u, pw4, bt4, ct4, h, a_seg)
    y = y.reshape(nquad, SSM_BLK, n_blocks, batch, SSM_SEGS, qw)
    y = jnp.transpose(y, (3, 4, 2, 1, 0, 5)).reshape(m, SSM_W)
    return fused_mm(m, SSM_W, [(y, 0, SSM_W, w_glu, (layer,), 0)], [(y, 0)],
                    lambda dts, ex: ex[0].astype(F32) * _sigmoid(dts[0]), BF16, tm=1024, tn=512, name="ssm_glu")


GMLP_NC = 4


def _gmlp_kernel(zu0_ref, zu1_ref, zv0_ref, zv1_ref, g_ref, ws_ref, bb_ref, o_ref):
    half = MLP_W // 2
    zv = jnp.concatenate([zv0_ref[...], zv1_ref[...]], axis=-1).astype(F32)
    v = _gelu(zv)
    v = (v * lax.rsqrt(jnp.mean(v * v, axis=-1, keepdims=True) + EPS) * g_ref[...]).astype(BF16)
    for g in range(MLP_GROUPS):
        cs = slice(g * MLP_GROUP_W, (g + 1) * MLP_GROUP_W)
        vg = jnp.concatenate([v[n * MLP_CHUNK:(n + 1) * MLP_CHUNK, cs] for n in range(GMLP_NC)], axis=-1)
        s = jnp.dot(ws_ref[g].astype(BF16), vg, preferred_element_type=F32)
        zu_ref = zu0_ref if g * MLP_GROUP_W < half else zu1_ref
        us = slice((g * MLP_GROUP_W) % half, (g * MLP_GROUP_W) % half + MLP_GROUP_W)
        for n in range(GMLP_NC):
            rs = slice(n * MLP_CHUNK, (n + 1) * MLP_CHUNK)
            sn = s[:, n * MLP_GROUP_W:(n + 1) * MLP_GROUP_W] + bb_ref[g]
            o_ref[rs, cs] = (_gelu(zu_ref[rs, us].astype(F32)) * sn).astype(o_ref.dtype)


def gmlp_branch(proj, layer, v_gain, w_s, b_bcast, m):
    rows = GMLP_NC * MLP_CHUNK
    half = MLP_W // 2

    def zspec(off):
        return pl.BlockSpec((rows, half), lambda i, off=off: (i, off // half))

    return pl.pallas_call(
        _gmlp_kernel, grid=(m // rows,),
        in_specs=[zspec(ZU_OFF), zspec(ZU_OFF + half), zspec(ZV_OFF), zspec(ZV_OFF + half),
                  pl.BlockSpec((None, 1, MLP_W), lambda i: (layer, 0, 0)),
                  pl.BlockSpec((None, MLP_GROUPS, MLP_CHUNK, MLP_CHUNK), lambda i: (layer, 0, 0, 0)),
                  pl.BlockSpec((None, MLP_GROUPS, MLP_CHUNK, MLP_GROUP_W), lambda i: (layer, 0, 0, 0))],
        out_specs=pl.BlockSpec((rows, MLP_W), lambda i: (i, 0)),
        out_shape=jax.ShapeDtypeStruct((m, MLP_W), BF16),
        compiler_params=_params("parallel"), name="gmlp",
    )(proj, proj, proj, proj, v_gain, w_s, b_bcast)


DENSE_TM = 1024
DENSE_TF = 512
MOE_TM = 1280
MOE_TF = 512
MOE_FC = 256
FFN_SUB = 256


def _ffn_kernel(ue_ref, un_ref, ub_ref, x_ref, w1_ref, w3_ref, w2_ref, o_ref, *, live_counts, fc):
    del ue_ref, ub_ref
    u = pl.program_id(0)
    j = pl.program_id(1)
    nsub = un_ref[u]

    @pl.when(j == 0)
    def _():
        o_ref[...] = jnp.zeros_like(o_ref)

    tf = w2_ref.shape[0]
    for k in live_counts:
        @pl.when(nsub == k)
        def _(k=k):
            rows = k * FFN_SUB
            x = x_ref[0:rows, :]
            for f0 in range(0, tf, fc):
                fs = slice(f0, f0 + fc)
                h1 = jnp.dot(x, w1_ref[:, fs].astype(BF16), preferred_element_type=F32)
                h3 = jnp.dot(x, w3_ref[:, fs].astype(BF16), preferred_element_type=F32)
                act = (_silu(h1) * h3).astype(BF16)
                o_ref[0:rows, :] += jnp.dot(act, w2_ref[fs, :].astype(BF16), preferred_element_type=F32)


def swiglu_ffn(x, unit_expert, unit_nsub, unit_block, w1, w3, w2, lead, live_counts, tm, tf, fc):
    rows, d = x.shape
    f = w1.shape[-1]
    n_units = rows // tm
    nf = f // tf
    nl = len(lead)

    def wmap_up(u, j, ue, un, ub):
        return tuple(lead) + (ue[u], 0, jnp.where(un[u] > 0, j, nf - 1))

    def wmap_down(u, j, ue, un, ub):
        return tuple(lead) + (ue[u], jnp.where(un[u] > 0, j, nf - 1), 0)

    def rmap(u, j, ue, un, ub):
        return (ub[u], 0)

    grid_spec = pltpu.PrefetchScalarGridSpec(
        num_scalar_prefetch=3, grid=(n_units, nf),
        in_specs=[pl.BlockSpec((tm, d), rmap),
                  pl.BlockSpec((None,) * (nl + 1) + (d, tf), wmap_up),
                  pl.BlockSpec((None,) * (nl + 1) + (d, tf), wmap_up),
                  pl.BlockSpec((None,) * (nl + 1) + (tf, d), wmap_down)],
        out_specs=pl.BlockSpec((tm, d), lambda u, j, ue, un, ub: (u, 0)))
    return pl.pallas_call(
        functools.partial(_ffn_kernel, live_counts=tuple(live_counts), fc=fc), grid_spec=grid_spec,
        out_shape=jax.ShapeDtypeStruct((rows, d), F32),
        compiler_params=_params("arbitrary", "arbitrary"), name="swiglu_ffn",
    )(unit_expert, unit_nsub, unit_block, x, w1, w3, w2)


def _router_kernel(x_ref, g_ref, wr_ref, h_ref, r_ref):
    x = x_ref[...]
    h = x * lax.rsqrt(jnp.mean(x * x, axis=-1, keepdims=True) + EPS) * g_ref[...]
    h_ref[...] = h
    logits = jnp.dot(h, wr_ref[...], preferred_element_type=F32, precision=lax.Precision.HIGHEST)
    lane = lax.broadcasted_iota(jnp.int32, logits.shape, 1)
    neg = jnp.float32(-jnp.inf)
    logits = jnp.where(lane < N_EXPERTS, logits, neg)
    m1 = jnp.max(logits, axis=-1, keepdims=True)
    i1 = jnp.min(jnp.where(logits == m1, lane, LANES), axis=-1, keepdims=True)
    rest = jnp.where(lane == i1, neg, logits)
    m2 = jnp.max(rest, axis=-1, keepdims=True)
    i2 = jnp.min(jnp.where(rest == m2, lane, LANES), axis=-1, keepdims=True)
    e = jnp.exp(m2 - m1)
    g1 = 1.0 / (1.0 + e)
    g2 = e / (1.0 + e)
    r_ref[...] = jnp.where(lane == 0, i1.astype(F32),
                           jnp.where(lane == 1, i2.astype(F32),
                                     jnp.where(lane == 2, g1, jnp.where(lane == 3, g2, 0.0))))


def norm_router(x, gains, layer, w_router_pad, tm=256):
    m, d = x.shape
    row = pl.BlockSpec((tm, d), lambda i: (i, 0))
    return pl.pallas_call(
        _router_kernel, grid=(m // tm,),
        in_specs=[row, pl.BlockSpec((None, 1, d), lambda i: (layer, 0, 0)),
                  pl.BlockSpec((d, LANES), lambda i: (0, 0))],
        out_specs=[row, pl.BlockSpec((tm, LANES), lambda i: (i, 0))],
        out_shape=[jax.ShapeDtypeStruct((m, d), F32), jax.ShapeDtypeStruct((m, LANES), F32)],
        compiler_params=_params("parallel"), name="norm_router")(x, gains, w_router_pad)


DMA_GROUP = 8


def _gather_kernel(src_ref, nv_ref, h_ref, o_ref, buf_ref, sem):
    i = pl.program_id(0)
    tb = buf_ref.shape[0]
    base = i * tb
    nv = nv_ref[i]
    ngroups = nv // DMA_GROUP

    @pl.when(i == 0)
    def _():
        buf_ref[...] = jnp.zeros_like(buf_ref)

    def row_copy(t):
        return pltpu.make_async_copy(h_ref.at[pl.ds(src_ref[base + t], 1)], buf_ref.at[pl.ds(t, 1)], sem)

    def group_copy():
        return pltpu.make_async_copy(h_ref.at[pl.ds(0, DMA_GROUP)], buf_ref.at[pl.ds(0, DMA_GROUP)], sem)

    def issue_group(c, carry):
        for r in range(DMA_GROUP):
            row_copy(c * DMA_GROUP + r).start()
        return carry

    def issue_row(t, carry):
        row_copy(t).start()
        return carry

    def wait_group(c, carry):
        group_copy().wait()
        return carry

    def wait_row(t, carry):
        row_copy(t).wait()
        return carry

    lax.fori_loop(0, ngroups, issue_group, 0)
    lax.fori_loop(ngroups * DMA_GROUP, nv, issue_row, 0)
    lax.fori_loop(0, ngroups, wait_group, 0)
    lax.fori_loop(ngroups * DMA_GROUP, nv, wait_row, 0)
    row = lax.broadcasted_iota(jnp.int32, o_ref.shape, 0)
    o_ref[...] = jnp.where(row < nv, buf_ref[...], 0.0).astype(o_ref.dtype)


def moe_gather(h, src, n_valid, cap, tb):
    _, d = h.shape
    grid_spec = pltpu.PrefetchScalarGridSpec(
        num_scalar_prefetch=2, grid=(cap // tb,),
        in_specs=[pl.BlockSpec(memory_space=pl.ANY)],
        out_specs=pl.BlockSpec((tb, d), lambda i, s, n: (i, 0)),
        scratch_shapes=[pltpu.VMEM((tb, d), h.dtype), pltpu.SemaphoreType.DMA(())])
    return pl.pallas_call(
        _gather_kernel, grid_spec=grid_spec,
        out_shape=jax.ShapeDtypeStruct((cap, d), BF16),
        compiler_params=_params("arbitrary"), name="moe_gather")(src, n_valid, h)


COMBINE_TB = 256


def _combine_kernel(dest_ref, x_ref, r_ref, g_ref, y_ref, xo_ref, h_ref, buf_ref, sem):
    base = pl.program_id(0) * COMBINE_TB

    def row_copy(t, k):
        return pltpu.make_async_copy(y_ref.at[pl.ds(dest_ref[TOP_K * (base + t) + k], 1)],
                                     buf_ref.at[k, pl.ds(t, 1)], sem)

    def issue_group(c, carry):
        for r in range(DMA_GROUP // TOP_K):
            for k in range(TOP_K):
                row_copy(c * (DMA_GROUP // TOP_K) + r, k).start()
        return carry

    lax.fori_loop(0, COMBINE_TB * TOP_K // DMA_GROUP, issue_group, 0)
    for k in range(TOP_K):
        pltpu.make_async_copy(y_ref.at[pl.ds(0, COMBINE_TB)], buf_ref.at[k], sem).wait()
    r = r_ref[...]
    g1 = r[:, 2:3]
    g2 = r[:, 3:4]
    x = x_ref[...] + (buf_ref[0] * g1 + buf_ref[1] * g2)
    xo_ref[...] = x
    y = x * lax.rsqrt(jnp.mean(x * x, axis=-1, keepdims=True) + EPS)
    h_ref[...] = (y * g_ref[...]).astype(h_ref.dtype)


def moe_combine(x, route, dest, ybuf, gains, layer):
    m, d = x.shape
    row = pl.BlockSpec((COMBINE_TB, d), lambda i, dr: (i, 0))
    grid_spec = pltpu.PrefetchScalarGridSpec(
        num_scalar_prefetch=1, grid=(m // COMBINE_TB,),
        in_specs=[row, pl.BlockSpec((COMBINE_TB, LANES), lambda i, dr: (i, 0)),
                  pl.BlockSpec((None, 1, d), lambda i, dr: (layer, 0, 0)),
                  pl.BlockSpec(memory_space=pl.ANY)],
        out_specs=[row, row],
        scratch_shapes=[pltpu.VMEM((TOP_K, COMBINE_TB, d), F32), pltpu.SemaphoreType.DMA(())])
    return pl.pallas_call(
        _combine_kernel, grid_spec=grid_spec,
        out_shape=[jax.ShapeDtypeStruct((m, d), F32), jax.ShapeDtypeStruct((m, d), BF16)],
        compiler_params=_params("arbitrary"), name="moe_combine")(dest, x, route, gains, ybuf)


def moe_layer(x1, ffn_norm, ple_norm, layer, w_router, e_w1, e_w3, e_w2, j):
    m, d = x1.shape
    wr_pad = jnp.zeros((d, LANES), F32).at[:, :N_EXPERTS].set(w_router)
    h, route = norm_router(x1, ffn_norm, layer, wr_pad)
    e_flat = route[:, :TOP_K].astype(jnp.int32).reshape(-1)
    onehot = (e_flat[:, None] == jnp.arange(N_EXPERTS)[None, :]).astype(jnp.int32)
    csum = jnp.cumsum(onehot, axis=0)
    rank = jnp.sum((csum - onehot) * onehot, axis=1)
    counts = csum[-1]
    tm = MOE_TM
    n_units_e = (counts + tm - 1) // tm
    unit_end = jnp.cumsum(n_units_e)
    unit_start = unit_end - n_units_e
    dest = (unit_start[e_flat] * tm + rank).astype(jnp.int32)
    n_units = (m * TOP_K) // tm + N_EXPERTS
    uidx = jnp.arange(n_units)
    ue = jnp.minimum(jnp.searchsorted(unit_end, uidx, side='right'), N_EXPERTS - 1).astype(jnp.int32)
    live_rows = jnp.clip(counts[ue] - (uidx - unit_start[ue]) * tm, 0, tm)
    live_rows = jnp.where(uidx < unit_end[-1], live_rows, 0)
    un = ((live_rows + FFN_SUB - 1) // FFN_SUB).astype(jnp.int32)
    last_live = jnp.maximum(unit_end[-1] - 1, 0)
    ue = jnp.where(uidx < unit_end[-1], ue, ue[last_live]).astype(jnp.int32)
    ub = jnp.where(uidx < unit_end[-1], uidx, last_live).astype(jnp.int32)

    cap = n_units * tm
    src = jnp.zeros((cap,), jnp.int32).at[dest].set(jnp.arange(m * TOP_K, dtype=jnp.int32) // TOP_K)

    xbuf = moe_gather(h, src, live_rows.astype(jnp.int32), cap, tm)
    ybuf = swiglu_ffn(xbuf, ue, un, ub, e_w1, e_w3, e_w2, (j,), range(1, tm // FFN_SUB + 1), tm, MOE_TF,
                      MOE_FC)
    return moe_combine(x1, route, dest, ybuf, ple_norm, layer)


def kernel(x, p, mix_norm, w_in, q_norm, k_norm, ssm_lambda_re, ssm_lambda_im, ssm_log_dt, ssm_b_re, ssm_b_im,
           ssm_c_re, ssm_c_im, ssm_d, ssm_glu_w, gmlp_v_norm, gmlp_ws, gmlp_b, w_branch, w_out, ffn_norm,
           dense_w1, dense_w3, dense_w2, router_w, expert_w1, expert_w3, expert_w2, ple_norm, ple_gate_w,
           ple_proj_w):
    batch, seq, d = x.shape
    depth = w_in.shape[0]
    n_in = w_in.shape[-1]
    m = batch * seq
    xs = x.reshape(m, d)
    rope_c, rope_s = rope_tables(seq)

    def g3(a):
        return a.reshape(a.shape[0], 1, a.shape[1])

    mix_norm, q_norm, k_norm, gmlp_v_norm, ffn_norm, ple_norm = map(
        g3, (mix_norm, q_norm, k_norm, gmlp_v_norm, ffn_norm, ple_norm))
    b_bcast = jnp.broadcast_to(gmlp_b[..., None], gmlp_b.shape + (MLP_GROUP_W,))
    p2 = p.reshape(depth, m, p.shape[-1])
    dense_units = m // DENSE_TM
    dense_un = jnp.full((dense_units,), DENSE_TM // FFN_SUB, jnp.int32)

    for i in range(depth):
        _, h = add_norm(xs, None, mix_norm, i)
        proj = fused_mm(m, n_in, [(h, 0, d, w_in, (i,), 0)], [], lambda dts, ex: dts[0], BF16,
                        tm=min(2048, m), tn=768, name="in_proj")
        attn = attention(proj, q_norm, k_norm, i, rope_c, rope_s, batch, seq)
        tabs = _ssm_tables(ssm_lambda_re[i], ssm_lambda_im[i], ssm_log_dt[i], ssm_b_re[i], ssm_b_im[i],
                           ssm_c_re[i], ssm_c_im[i], ssm_d[i], seq // SSM_SEGS)
        ssm = ssm_branch(proj, i, tabs, ssm_glu_w, batch, seq)
        mlp = gmlp_branch(proj, i, gmlp_v_norm, gmlp_ws, b_bcast, m)

        tn = 512
        merged = fused_mm(
            m, d,
            [(br, 0, br.shape[1], w_branch, (i, n), 0) for n, br in enumerate((attn, ssm, mlp))],
            [(proj, (GATE_OFF + n * d) // tn) for n in range(N_BRANCH)],
            lambda dts, ex: sum(_sigmoid(e.astype(F32)) * dt for e, dt in zip(ex, dts)),
            BF16, tm=1024, tn=tn, name="branch_merge")
        x1 = fused_mm(m, d, [(merged, 0, d, w_out, (i,), 0)], [(xs, 0)],
                      lambda dts, ex: ex[0] + dts[0], F32, tm=min(2048, m), tn=512, name="out_proj")

        j = i // 2
        if i % 2 == 0:
            _, h2 = add_norm(x1, None, ffn_norm, i)
            dense_ue = jnp.full((dense_units,), j, jnp.int32)
            y = swiglu_ffn(h2, dense_ue, dense_un, jnp.arange(dense_units, dtype=jnp.int32),
                           dense_w1, dense_w3, dense_w2, (), (DENSE_TM // FFN_SUB,), DENSE_TM, DENSE_TF,
                           DENSE_TF)
            x2, hn = add_norm(x1, y, ple_norm, i)
        else:
            x2, hn = moe_layer(x1, ffn_norm, ple_norm, i, router_w[j], expert_w1, expert_w3, expert_w2, j)

        xs = fused_mm(
            m, d,
            [(hn, 0, d, ple_gate_w, (i,), 0), ((p2, (i,)), 0, p.shape[-1], ple_proj_w, (i,), 0)],
            [(x2, 0)],
            lambda dts, ex: ex[0] + _sigmoid(dts[0]) * dts[1], F32, tm=min(2048, m), tn=512, name="ple")
    return xs.reshape(batch, seq, d)
```

```python
import functools
import math

import jax
import jax.numpy as jnp
import numpy as np
from jax import lax
from jax.experimental import pallas as pl
from jax.experimental.pallas import tpu as pltpu

GRID_W = 64
ROPE_THETA = 10000.0
HEAD_DIM = 128
N_Q_HEADS = 8
N_KV_HEADS = 2
ATTN_W = N_Q_HEADS * HEAD_DIM
KV_W = N_KV_HEADS * HEAD_DIM
SSM_GROUP = 16
SSM_GROUPS = 64
SSM_W = SSM_GROUP * SSM_GROUPS
SSM_STATE = 64
MLP_CHUNK = 128
MLP_GROUPS = 8
MLP_GROUP_W = 128
MLP_W = MLP_GROUPS * MLP_GROUP_W
N_BRANCH = 3
N_EXPERTS = 8
TOP_K = 2
EPS = 1e-6

Q_OFF = 0
K_OFF = ATTN_W
V_OFF = K_OFF + KV_W
U_OFF = V_OFF + KV_W
ZU_OFF = U_OFF + SSM_W
ZV_OFF = ZU_OFF + MLP_W
GATE_OFF = ZV_OFF + MLP_W

V7X_VMEM_LIMIT_BYTES = 62 * 1024 * 1024
LANES = 128

BF16 = jnp.bfloat16
F32 = jnp.float32


def _params(*sem):
    return pltpu.CompilerParams(dimension_semantics=sem, vmem_limit_bytes=V7X_VMEM_LIMIT_BYTES)


def _gelu(x):
    c = math.sqrt(2.0 / math.pi)
    return 0.5 * x * (1.0 + jnp.tanh(c * (x + 0.044715 * (x * x * x))))


def _sigmoid(x):
    return 1.0 / (1.0 + jnp.exp(-x))


def _silu(x):
    return x * _sigmoid(x)


def _add_norm_kernel(*refs, has_delta):
    if has_delta:
        x_ref, d_ref, g_ref, xo_ref, h_ref = refs
        x = x_ref[...] + d_ref[...]
        xo_ref[...] = x
    else:
        x_ref, g_ref, h_ref = refs
        x = x_ref[...]
    y = x * lax.rsqrt(jnp.mean(x * x, axis=-1, keepdims=True) + EPS)
    h_ref[...] = (y * g_ref[...]).astype(h_ref.dtype)


def add_norm(x, delta, gains, layer, tm=512):
    m, d = x.shape
    row = pl.BlockSpec((tm, d), lambda i: (i, 0))
    gspec = pl.BlockSpec((None, 1, d), lambda i: (layer, 0, 0))
    if delta is None:
        h = pl.pallas_call(
            functools.partial(_add_norm_kernel, has_delta=False),
            grid=(m // tm,), in_specs=[row, gspec], out_specs=row,
            out_shape=jax.ShapeDtypeStruct((m, d), BF16),
            compiler_params=_params("parallel"), name="norm")(x, gains)
        return x, h
    xo, h = pl.pallas_call(
        functools.partial(_add_norm_kernel, has_delta=True),
        grid=(m // tm,), in_specs=[row, row, gspec], out_specs=[row, row],
        out_shape=[jax.ShapeDtypeStruct((m, d), F32), jax.ShapeDtypeStruct((m, d), BF16)],
        compiler_params=_params("parallel"), name="add_norm")(x, delta, gains)
    return xo, h


def _fused_mm_kernel(*refs, n_dots, n_extras, epilogue):
    a_refs = refs[:n_dots]
    w_refs = refs[n_dots:2 * n_dots]
    e_refs = refs[2 * n_dots:2 * n_dots + n_extras]
    o_ref = refs[2 * n_dots + n_extras]
    wb_refs = refs[2 * n_dots + n_extras + 1:]

    @pl.when(pl.program_id(1) == 0)
    def _():
        for w_ref, wb_ref in zip(w_refs, wb_refs):
            wb_ref[...] = w_ref[...].astype(BF16)

    dots = []
    for a_ref, wb_ref in zip(a_refs, wb_refs):
        a = a_ref[...]
        if a.dtype != BF16:
            a = a.astype(BF16)
        dots.append(jnp.dot(a, wb_ref[...], preferred_element_type=F32))
    extras = [e_ref[...] for e_ref in e_refs]
    o_ref[...] = epilogue(dots, extras).astype(o_ref.dtype)


def fused_mm(m, n, dots, extras, epilogue, out_dtype, tm, tn, name):
    in_specs, args, scratch = [], [], []
    for a, acb, k, _, _, _ in dots:
        a, a_lead = a if isinstance(a, tuple) else (a, ())
        in_specs.append(pl.BlockSpec((None,) * len(a_lead) + (tm, k),
                                     lambda j, i, acb=acb, a_lead=tuple(a_lead): a_lead + (i, acb)))
        args.append(a)
    for _, _, k, w, lead, off in dots:
        in_specs.append(pl.BlockSpec((None,) * len(lead) + (k, tn),
                                     lambda j, i, lead=tuple(lead), off=off: lead + (0, off + j)))
        args.append(w)
        scratch.append(pltpu.VMEM((k, tn), BF16))
    for e, off in extras:
        in_specs.append(pl.BlockSpec((tm, tn), lambda j, i, off=off: (i, off + j)))
        args.append(e)
    return pl.pallas_call(
        functools.partial(_fused_mm_kernel, n_dots=len(dots), n_extras=len(extras), epilogue=epilogue),
        grid=(n // tn, m // tm), in_specs=in_specs,
        out_specs=pl.BlockSpec((tm, tn), lambda j, i: (i, j)),
        out_shape=jax.ShapeDtypeStruct((m, n), out_dtype),
        scratch_shapes=scratch,
        compiler_params=_params("arbitrary", "arbitrary"), name=name)(*args)


def _rope(x, c, s):
    lane = lax.broadcasted_iota(jnp.int32, x.shape, x.ndim - 1)
    quarter = HEAD_DIM // 4
    partner = jnp.where((lane % (2 * quarter)) < quarter,
                        pltpu.roll(x, HEAD_DIM - quarter, x.ndim - 1),
                        pltpu.roll(x, quarter, x.ndim - 1))
    return x * c + partner * s


def _head_norm(x, g):
    return x * lax.rsqrt(jnp.mean(x * x, axis=-1, keepdims=True) + EPS) * g


def _attn_kernel(q_ref, k_ref, v_ref, cq_ref, sq_ref, ck_ref, sk_ref, qg_ref, kg_ref, o_ref, ks_ref, vs_ref, *,
                 rep):
    @pl.when(pl.program_id(2) == 0)
    def _():
        k = _head_norm(k_ref[...].astype(F32), kg_ref[...])
        ks_ref[...] = _rope(k, ck_ref[...], sk_ref[...]).astype(BF16)
        vs_ref[:, 0:HEAD_DIM] = v_ref[...]
        vs_ref[:, HEAD_DIM:2 * HEAD_DIM] = jnp.ones((v_ref.shape[0], HEAD_DIM), BF16)

    scale = HEAD_DIM ** -0.5 * math.log2(math.e)
    cq = cq_ref[...]
    sq = sq_ref[...]
    for hh in range(rep):
        sl = slice(hh * HEAD_DIM, (hh + 1) * HEAD_DIM)
        q = _head_norm(q_ref[:, sl].astype(F32), qg_ref[...])
        q = (_rope(q, cq, sq) * scale).astype(BF16)
        s = lax.dot_general(q, ks_ref[...], (((1,), (1,)), ((), ())), preferred_element_type=F32)
        m = jnp.max(s, axis=-1, keepdims=True)
        p = jnp.exp2(s - m).astype(BF16)
        o = jnp.dot(p, vs_ref[...], preferred_element_type=F32)
        o_ref[:, sl] = (o[:, 0:HEAD_DIM] / o[:, HEAD_DIM:HEAD_DIM + 1]).astype(o_ref.dtype)


def attention(proj, q_gain, k_gain, layer, rope_c, rope_s, batch, seq, tq=512):
    rep = N_Q_HEADS // N_KV_HEADS
    qw = rep * HEAD_DIM
    nq = seq // tq
    gspec = pl.BlockSpec((None, 1, HEAD_DIM), lambda b, g, i: (layer, 0, 0))
    return pl.pallas_call(
        functools.partial(_attn_kernel, rep=rep),
        grid=(batch, N_KV_HEADS, nq),
        in_specs=[
            pl.BlockSpec((tq, qw), lambda b, g, i: (b * nq + i, Q_OFF // qw + g)),
            pl.BlockSpec((seq, HEAD_DIM), lambda b, g, i: (b, K_OFF // HEAD_DIM + g)),
            pl.BlockSpec((seq, HEAD_DIM), lambda b, g, i: (b, V_OFF // HEAD_DIM + g)),
            pl.BlockSpec((tq, HEAD_DIM), lambda b, g, i: (i, 0)),
            pl.BlockSpec((tq, HEAD_DIM), lambda b, g, i: (i, 0)),
            pl.BlockSpec((seq, HEAD_DIM), lambda b, g, i: (0, 0)),
            pl.BlockSpec((seq, HEAD_DIM), lambda b, g, i: (0, 0)),
            gspec, gspec,
        ],
        out_specs=pl.BlockSpec((tq, qw), lambda b, g, i: (b * nq + i, g)),
        out_shape=jax.ShapeDtypeStruct((batch * seq, ATTN_W), BF16),
        scratch_shapes=[pltpu.VMEM((seq, HEAD_DIM), BF16), pltpu.VMEM((seq, 2 * HEAD_DIM), BF16)],
        compiler_params=_params("arbitrary", "arbitrary", "arbitrary"), name="attention",
    )(proj, proj, proj, rope_c, rope_s, rope_c, rope_s, q_gain, k_gain)


def rope_tables(seq):
    rows = seq // GRID_W
    t = np.arange(seq)
    pos = np.stack([t // GRID_W - rows // 2, t % GRID_W - GRID_W // 2], axis=-1).astype(np.float32)
    n_freq = HEAD_DIM // 4
    inv_freq = np.float32(ROPE_THETA) ** (-np.arange(n_freq, dtype=np.float32) / np.float32(n_freq))
    ang = pos[:, :, None] * inv_freq.astype(np.float32)
    cos, sin = np.cos(ang).astype(np.float32), np.sin(ang).astype(np.float32)
    c = np.concatenate([cos[:, 0], cos[:, 0], cos[:, 1], cos[:, 1]], axis=-1)
    s = np.concatenate([-sin[:, 0], sin[:, 0], -sin[:, 1], sin[:, 1]], axis=-1)
    return jnp.asarray(c), jnp.asarray(s)


SSM_SEGS = 8
SSM_BLK = 16
SSM_PG = 2
SSM_QUAD = 4
SSM_PPS = 4
SSM_PW = SSM_PG * SSM_STATE
SSM_ROW = SSM_BLK * SSM_PG * SSM_GROUP


def _ssm_kernel(u_ref, pw_ref, bt_ref, ct_ref, h_ref, at_ref, o_ref, acc_ref, *scratch, n_seq, n_blocks):
    @pl.when(pl.program_id(1) == 0)
    def _():
        acc_ref[...] = jnp.zeros_like(acc_ref)

    per_pair = len(scratch) // SSM_PPS
    for pp in range(SSM_PPS):
        groups = slice(pp * SSM_PG, (pp + 1) * SSM_PG)
        _ssm_pair(u_ref, pw_ref.at[groups], bt_ref.at[groups], ct_ref.at[groups], h_ref.at[pp], at_ref.at[groups],
                  acc_ref, *scratch[pp * per_pair:(pp + 1) * per_pair],
                  pair_in_quad=pl.program_id(1) * SSM_PPS + pp, n_seq=n_seq, n_blocks=n_blocks)

    @pl.when(pl.program_id(1) == SSM_QUAD // SSM_PPS - 1)
    def _():
        o_ref[...] = acc_ref[...].astype(o_ref.dtype)


def _ssm_pair(u_ref, pw_ref, bt_ref, ct_ref, h_ref, at_ref, acc_ref, z_ref, s_ref, e_ref, t_ref, bw_ref, vt_ref, *,
              pair_in_quad, n_seq, n_blocks):
    ns = n_seq * SSM_SEGS
    pw = SSM_PW
    xw = SSM_PG * SSM_GROUP
    qw = SSM_QUAD * xw
    xoff = pl.multiple_of(pair_in_quad * xw, xw)

    def part(k):
        return slice(k * pw, (k + 1) * pw)

    bw_ref[...] = jnp.zeros_like(bw_ref)
    t_ref[...] = jnp.zeros_like(t_ref)

    lane = lax.broadcasted_iota(jnp.int32, (SSM_GROUP, pw), 1)
    owns = [lane // SSM_STATE == gp for gp in range(SSM_PG)]

    def pair_lanes(ref):
        return jnp.concatenate([ref[gp] for gp in range(SSM_PG)], axis=-1)

    apow, bbar, cmat, aseg = pair_lanes(pw_ref), pair_lanes(bt_ref), pair_lanes(ct_ref), pair_lanes(at_ref)

    def build(m, d, k, neg_im):
        ar, ai = apow[2 * d, k:k + 1, :], apow[2 * d + 1, k:k + 1, :]
        mr, mi = m[2 * d], m[2 * d + 1]
        xr, xi = ar * mr - ai * mi, ar * mi + ai * mr
        if neg_im:
            xi = -xi
        return [(jnp.where(own, xr, 0.0).astype(BF16), jnp.where(own, xi, 0.0).astype(BF16)) for own in owns]

    for step in range(SSM_BLK):
        for d, k_bw, k_vt in ((0, SSM_BLK - 1 - step, step + 1), (1, step, SSM_BLK - step)):
            for gp, (xr, xi) in enumerate(build(bbar, d, k_bw, False)):
                rws = pl.ds((step % 2) * qw + xoff + gp * SSM_GROUP, SSM_GROUP)
                bw_ref[step // 2, rws, part(2 * d)] = xr
                bw_ref[step // 2, rws, part(2 * d + 1)] = xi
            for gp, (xr, xi) in enumerate(build(cmat, d, k_vt, True)):
                rws = slice(step * xw + gp * SSM_GROUP, step * xw + (gp + 1) * SSM_GROUP)
                vt_ref[rws, part(2 * d)] = xr
                vt_ref[rws, part(2 * d + 1)] = xi

    def u2(s2):
        return jnp.concatenate([u_ref[2 * s2], u_ref[2 * s2 + 1]], axis=-1)

    z = jnp.dot(u2(0), bw_ref[0], preferred_element_type=F32)
    for s2 in range(1, SSM_BLK // 2):
        z = z + jnp.dot(u2(s2), bw_ref[s2], preferred_element_type=F32)
    z_ref[...] = z
    afr, afi, abr, abi = [jnp.broadcast_to(apow[k, SSM_BLK:SSM_BLK + 1, :], (ns, pw)) for k in range(4)]

    def rows(blk):
        return slice(blk * ns, (blk + 1) * ns)

    def scan(state, keep):
        fr, fi, br, bi = state
        for i in range(n_blocks):
            mf, mb = i, n_blocks - 1 - i
            if keep:
                s_ref[rows(mf), part(0)] = fr.astype(BF16)
                s_ref[rows(mf), part(1)] = fi.astype(BF16)
                s_ref[rows(mb), part(2)] = br.astype(BF16)
                s_ref[rows(mb), part(3)] = bi.astype(BF16)
            zfr, zfi = z_ref[rows(mf), part(0)], z_ref[rows(mf), part(1)]
            zbr, zbi = z_ref[rows(mb), part(2)], z_ref[rows(mb), part(3)]
            fr, fi = afr * fr - afi * fi + zfr, afr * fi + afi * fr + zfi
            br, bi = abr * br - abi * bi + zbr, abr * bi + abi * br + zbi
        return fr, fi, br, bi

    zero = jnp.zeros((ns, pw), F32)
    ends = scan((zero, zero, zero, zero), keep=False)
    for k in range(4):
        e_ref[:, part(k)] = ends[k]
    row0 = jnp.zeros((1, pw), F32)
    for b in range(n_seq):
        for kr, ki, order in ((0, 1, range(SSM_SEGS)), (2, 3, range(SSM_SEGS - 1, -1, -1))):
            tr, ti = aseg[kr], aseg[ki]
            cr, ci = row0, row0
            for q in order:
                row = slice(b * SSM_SEGS + q, b * SSM_SEGS + q + 1)
                er, ei = e_ref[row, part(kr)], e_ref[row, part(ki)]
                e_ref[row, part(kr)] = cr
                e_ref[row, part(ki)] = ci
                cr, ci = tr * cr - ti * ci + er, tr * ci + ti * cr + ei
    scan(tuple(e_ref[:, part(k)] for k in range(4)), keep=True)

    h = h_ref[...]
    for rp in range(SSM_BLK):
        off = xw * (SSM_BLK - 1 - rp)
        t_ref[rp // 2, pl.ds((rp % 2) * qw + xoff, xw), :] = h[:, off:off + SSM_ROW].astype(BF16)

    y = lax.dot_general(s_ref[...], vt_ref[...], (((1,), (1,)), ((), ())), preferred_element_type=F32)
    for s2 in range(SSM_BLK // 2):
        y = y + jnp.dot(u2(s2), t_ref[s2], preferred_element_type=F32)
    y = _gelu(y)

    lane = lax.broadcasted_iota(jnp.int32, (y.shape[0], qw), 1)
    mine = lane // xw == pair_in_quad
    for step in range(SSM_BLK):
        tile = y[:, (step // SSM_QUAD) * qw:(step // SSM_QUAD + 1) * qw]
        shift = ((pair_in_quad + SSM_QUAD - step % SSM_QUAD) % SSM_QUAD) * xw
        acc_ref[step] = jnp.where(mine, pltpu.roll(tile, shift, 1), acc_ref[step])


def _ssm_tables(lam_re, lam_im, log_dt, b_re, b_im, c_re, c_im, d_skip, seg_len):
    g, c, r, pg = SSM_GROUPS, SSM_GROUP, SSM_BLK, SSM_PG
    npair = g // pg
    lre, lim = lam_re.astype(F32), lam_im.astype(F32)
    dt = jnp.exp(log_dt.astype(F32))[..., None]

    def cexp(xr, xi):
        e = jnp.exp(xr)
        return e * jnp.cos(xi), e * jnp.sin(xi)

    def cmul(ar, ai, br, bi):
        return ar * br - ai * bi, ar * bi + ai * br

    steps = jnp.arange(r + 1, dtype=F32)[None, :, None, None]
    pr, pi = cexp((lre * dt)[:, None] * steps, (lim * dt)[:, None] * steps)
    den = lre * lre + lim * lim
    qr = ((pr[:, 1] - 1.0) * lre + pi[:, 1] * lim) / den
    qi = (pi[:, 1] * lre - (pr[:, 1] - 1.0) * lim) / den
    bbr, bbi = cmul(qr[..., None], qi[..., None], b_re.astype(F32), b_im.astype(F32))
    ccr, cci = c_re.astype(F32), c_im.astype(F32)

    def per_group(xr, xi):
        y = jnp.stack([xr, xi], axis=1)
        return jnp.transpose(y, (3, 0, 1, 2, 4)).reshape(g, 4, xr.shape[1], xr.shape[3])

    pw4 = per_group(pr, pi)
    bt4 = per_group(jnp.transpose(bbr, (0, 3, 1, 2)), jnp.transpose(bbi, (0, 3, 1, 2)))
    ct4 = per_group(jnp.transpose(ccr, (0, 2, 1, 3)), jnp.transpose(cci, (0, 2, 1, 3)))

    xr, xi = cmul(ccr[:, :, None], cci[:, :, None], jnp.transpose(pr[:, :r], (0, 2, 1, 3))[:, :, :, None],
                  jnp.transpose(pi[:, :r], (0, 2, 1, 3))[:, :, :, None])
    klag = jnp.einsum('zgkcq,zgqd->zgkdc', jnp.concatenate([xr, -xi], axis=-1),
                      jnp.concatenate([bbr, bbi], axis=2))
    kf, kb = klag[0], klag[1]
    lag0 = kf[:, 0] + kb[:, 0] + jnp.eye(c, dtype=F32) * d_skip.astype(F32).reshape(g, 1, c)
    lagged = jnp.concatenate([kb[:, :0:-1], lag0[:, None], kf[:, 1:]], axis=1)
    lagged = lagged.reshape(npair, pg, 2 * r - 1, c, c)
    h = jnp.einsum('aindc,ij->aidnjc', lagged, jnp.eye(pg, dtype=F32)).reshape(npair, pg * c, 2 * r - 1, pg * c)
    h = jnp.pad(h, ((0, 0), (0, 0), (0, 1), (0, 0))).reshape(npair, pg * c, 2 * r * pg * c)

    sr, si = cexp(lre * dt * seg_len, lim * dt * seg_len)
    a_seg = per_group(sr[:, None], si[:, None])
    return pw4, bt4, ct4, h, a_seg


def ssm_branch(proj, layer, tabs, w_glu, batch, seq):
    pw4, bt4, ct4, h, a_seg = tabs
    seg_len = seq // SSM_SEGS
    n_blocks = seg_len // SSM_BLK
    ns = batch * SSM_SEGS
    m = batch * seq
    nquad = SSM_GROUPS // (SSM_PG * SSM_QUAD)
    qw = SSM_QUAD * SSM_PG * SSM_GROUP
    rows = n_blocks * ns
    u = proj[:, U_OFF:U_OFF + SSM_W].reshape(batch, SSM_SEGS, n_blocks, SSM_BLK, nquad, qw)
    u = jnp.transpose(u, (4, 3, 2, 0, 1, 5)).reshape(nquad, SSM_BLK, rows, qw)

    quad = pl.BlockSpec((None, SSM_BLK, rows, qw), lambda q, j: (q, 0, 0, 0))

    steps_per_quad = SSM_QUAD // SSM_PPS

    def per_group(a):
        return pl.BlockSpec((SSM_PPS * SSM_PG,) + a.shape[1:], lambda q, j: (q * steps_per_quad + j, 0, 0, 0))

    pair_scratch = [pltpu.VMEM((rows, 4 * SSM_PW), F32), pltpu.VMEM((rows, 4 * SSM_PW), BF16),
                    pltpu.VMEM((ns, 4 * SSM_PW), F32), pltpu.VMEM((SSM_BLK // 2, 2 * qw, SSM_ROW), BF16),
                    pltpu.VMEM((SSM_BLK // 2, 2 * qw, 4 * SSM_PW), BF16), pltpu.VMEM((SSM_ROW, 4 * SSM_PW), BF16)]
    y = pl.pallas_call(
        functools.partial(_ssm_kernel, n_seq=batch, n_blocks=n_blocks),
        grid=(nquad, steps_per_quad),
        in_specs=[quad, per_group(pw4), per_group(bt4), per_group(ct4),
                  pl.BlockSpec((SSM_PPS,) + h.shape[1:], lambda q, j: (q * steps_per_quad + j, 0, 0)),
                  per_group(a_seg)],
        out_specs=quad,
        out_shape=jax.ShapeDtypeStruct((nquad, SSM_BLK, rows, qw), BF16),
        scratch_shapes=[pltpu.VMEM((SSM_BLK, rows, qw), F32)] + pair_scratch * SSM_PPS,
        compiler_params=_params("arbitrary", "arbitrary"), name="ssm")(u, pw4, bt4, ct4, h, a_seg)
    y = y.reshape(nquad, SSM_BLK, n_blocks, batch, SSM_SEGS, qw)
    y = jnp.transpose(y, (3, 4, 2, 1, 0, 5)).reshape(m, SSM_W)
    return fused_mm(m, SSM_W, [(y, 0, SSM_W, w_glu, (layer,), 0)], [(y, 0)],
                    lambda dts, ex: ex[0].astype(F32) * _sigmoid(dts[0]), BF16, tm=1024, tn=512, name="ssm_glu")


GMLP_NC = 4


def _gmlp_kernel(zu0_ref, zu1_ref, zv0_ref, zv1_ref, g_ref, ws_ref, bb_ref, o_ref):
    half = MLP_W // 2
    zv = jnp.concatenate([zv0_ref[...], zv1_ref[...]], axis=-1).astype(F32)
    v = _gelu(zv)
    v = (v * lax.rsqrt(jnp.mean(v * v, axis=-1, keepdims=True) + EPS) * g_ref[...]).astype(BF16)
    for g in range(MLP_GROUPS):
        cs = slice(g * MLP_GROUP_W, (g + 1) * MLP_GROUP_W)
        vg = jnp.concatenate([v[n * MLP_CHUNK:(n + 1) * MLP_CHUNK, cs] for n in range(GMLP_NC)], axis=-1)
        s = jnp.dot(ws_ref[g].astype(BF16), vg, preferred_element_type=F32)
        zu_ref = zu0_ref if g * MLP_GROUP_W < half else zu1_ref
        us = slice((g * MLP_GROUP_W) % half, (g * MLP_GROUP_W) % half + MLP_GROUP_W)
        for n in range(GMLP_NC):
            rs = slice(n * MLP_CHUNK, (n + 1) * MLP_CHUNK)
            sn = s[:, n * MLP_GROUP_W:(n + 1) * MLP_GROUP_W] + bb_ref[g]
            o_ref[rs, cs] = (_gelu(zu_ref[rs, us].astype(F32)) * sn).astype(o_ref.dtype)


def gmlp_branch(proj, layer, v_gain, w_s, b_bcast, m):
    rows = GMLP_NC * MLP_CHUNK
    half = MLP_W // 2

    def zspec(off):
        return pl.BlockSpec((rows, half), lambda i, off=off: (i, off // half))

    return pl.pallas_call(
        _gmlp_kernel, grid=(m // rows,),
        in_specs=[zspec(ZU_OFF), zspec(ZU_OFF + half), zspec(ZV_OFF), zspec(ZV_OFF + half),
                  pl.BlockSpec((None, 1, MLP_W), lambda i: (layer, 0, 0)),
                  pl.BlockSpec((None, MLP_GROUPS, MLP_CHUNK, MLP_CHUNK), lambda i: (layer, 0, 0, 0)),
                  pl.BlockSpec((None, MLP_GROUPS, MLP_CHUNK, MLP_GROUP_W), lambda i: (layer, 0, 0, 0))],
        out_specs=pl.BlockSpec((rows, MLP_W), lambda i: (i, 0)),
        out_shape=jax.ShapeDtypeStruct((m, MLP_W), BF16),
        compiler_params=_params("parallel"), name="gmlp",
    )(proj, proj, proj, proj, v_gain, w_s, b_bcast)


DENSE_TM = 1024
DENSE_TF = 512
MOE_TM = 1280
MOE_TF = 512
MOE_FC = 256
FFN_SUB = 256


def _ffn_kernel(ue_ref, un_ref, ub_ref, x_ref, w1_ref, w3_ref, w2_ref, o_ref, *, live_counts, fc):
    del ue_ref, ub_ref
    u = pl.program_id(0)
    j = pl.program_id(1)
    nsub = un_ref[u]

    @pl.when(j == 0)
    def _():
        o_ref[...] = jnp.zeros_like(o_ref)

    tf = w2_ref.shape[0]
    for k in live_counts:
        @pl.when(nsub == k)
        def _(k=k):
            rows = k * FFN_SUB
            x = x_ref[0:rows, :]
            for f0 in range(0, tf, fc):
                fs = slice(f0, f0 + fc)
                h1 = jnp.dot(x, w1_ref[:, fs].astype(BF16), preferred_element_type=F32)
                h3 = jnp.dot(x, w3_ref[:, fs].astype(BF16), preferred_element_type=F32)
                act = (_silu(h1) * h3).astype(BF16)
                o_ref[0:rows, :] += jnp.dot(act, w2_ref[fs, :].astype(BF16), preferred_element_type=F32)


def swiglu_ffn(x, unit_expert, unit_nsub, unit_block, w1, w3, w2, lead, live_counts, tm, tf, fc):
    rows, d = x.shape
    f = w1.shape[-1]
    n_units = rows // tm
    nf = f // tf
    nl = len(lead)

    def wmap_up(u, j, ue, un, ub):
        return tuple(lead) + (ue[u], 0, jnp.where(un[u] > 0, j, nf - 1))

    def wmap_down(u, j, ue, un, ub):
        return tuple(lead) + (ue[u], jnp.where(un[u] > 0, j, nf - 1), 0)

    def rmap(u, j, ue, un, ub):
        return (ub[u], 0)

    grid_spec = pltpu.PrefetchScalarGridSpec(
        num_scalar_prefetch=3, grid=(n_units, nf),
        in_specs=[pl.BlockSpec((tm, d), rmap),
                  pl.BlockSpec((None,) * (nl + 1) + (d, tf), wmap_up),
                  pl.BlockSpec((None,) * (nl + 1) + (d, tf), wmap_up),
                  pl.BlockSpec((None,) * (nl + 1) + (tf, d), wmap_down)],
        out_specs=pl.BlockSpec((tm, d), lambda u, j, ue, un, ub: (u, 0)))
    return pl.pallas_call(
        functools.partial(_ffn_kernel, live_counts=tuple(live_counts), fc=fc), grid_spec=grid_spec,
        out_shape=jax.ShapeDtypeStruct((rows, d), F32),
        compiler_params=_params("arbitrary", "arbitrary"), name="swiglu_ffn",
    )(unit_expert, unit_nsub, unit_block, x, w1, w3, w2)


def _router_kernel(x_ref, g_ref, wr_ref, h_ref, r_ref):
    x = x_ref[...]
    h = x * lax.rsqrt(jnp.mean(x * x, axis=-1, keepdims=True) + EPS) * g_ref[...]
    h_ref[...] = h
    logits = jnp.dot(h, wr_ref[...], preferred_element_type=F32, precision=lax.Precision.HIGHEST)
    lane = lax.broadcasted_iota(jnp.int32, logits.shape, 1)
    neg = jnp.float32(-jnp.inf)
    logits = jnp.where(lane < N_EXPERTS, logits, neg)
    m1 = jnp.max(logits, axis=-1, keepdims=True)
    i1 = jnp.min(jnp.where(logits == m1, lane, LANES), axis=-1, keepdims=True)
    rest = jnp.where(lane == i1, neg, logits)
    m2 = jnp.max(rest, axis=-1, keepdims=True)
    i2 = jnp.min(jnp.where(rest == m2, lane, LANES), axis=-1, keepdims=True)
    e = jnp.exp(m2 - m1)
    g1 = 1.0 / (1.0 + e)
    g2 = e / (1.0 + e)
    r_ref[...] = jnp.where(lane == 0, i1.astype(F32),
                           jnp.where(lane == 1, i2.astype(F32),
                                     jnp.where(lane == 2, g1, jnp.where(lane == 3, g2, 0.0))))


def norm_router(x, gains, layer, w_router_pad, tm=256):
    m, d = x.shape
    row = pl.BlockSpec((tm, d), lambda i: (i, 0))
    return pl.pallas_call(
        _router_kernel, grid=(m // tm,),
        in_specs=[row, pl.BlockSpec((None, 1, d), lambda i: (layer, 0, 0)),
                  pl.BlockSpec((d, LANES), lambda i: (0, 0))],
        out_specs=[row, pl.BlockSpec((tm, LANES), lambda i: (i, 0))],
        out_shape=[jax.ShapeDtypeStruct((m, d), F32), jax.ShapeDtypeStruct((m, LANES), F32)],
        compiler_params=_params("parallel"), name="norm_router")(x, gains, w_router_pad)


DMA_GROUP = 8


def _gather_kernel(src_ref, nv_ref, h_ref, o_ref, buf_ref, sem):
    i = pl.program_id(0)
    tb = buf_ref.shape[0]
    base = i * tb
    nv = nv_ref[i]
    ngroups = nv // DMA_GROUP

    @pl.when(i == 0)
    def _():
        buf_ref[...] = jnp.zeros_like(buf_ref)

    def row_copy(t):
        return pltpu.make_async_copy(h_ref.at[pl.ds(src_ref[base + t], 1)], buf_ref.at[pl.ds(t, 1)], sem)

    def group_copy():
        return pltpu.make_async_copy(h_ref.at[pl.ds(0, DMA_GROUP)], buf_ref.at[pl.ds(0, DMA_GROUP)], sem)

    def issue_group(c, carry):
        for r in range(DMA_GROUP):
            row_copy(c * DMA_GROUP + r).start()
        return carry

    def issue_row(t, carry):
        row_copy(t).start()
        return carry

    def wait_group(c, carry):
        group_copy().wait()
        return carry

    def wait_row(t, carry):
        row_copy(t).wait()
        return carry

    lax.fori_loop(0, ngroups, issue_group, 0)
    lax.fori_loop(ngroups * DMA_GROUP, nv, issue_row, 0)
    lax.fori_loop(0, ngroups, wait_group, 0)
    lax.fori_loop(ngroups * DMA_GROUP, nv, wait_row, 0)
    row = lax.broadcasted_iota(jnp.int32, o_ref.shape, 0)
    o_ref[...] = jnp.where(row < nv, buf_ref[...], 0.0).astype(o_ref.dtype)


def moe_gather(h, src, n_valid, cap, tb):
    _, d = h.shape
    grid_spec = pltpu.PrefetchScalarGridSpec(
        num_scalar_prefetch=2, grid=(cap // tb,),
        in_specs=[pl.BlockSpec(memory_space=pl.ANY)],
        out_specs=pl.BlockSpec((tb, d), lambda i, s, n: (i, 0)),
        scratch_shapes=[pltpu.VMEM((tb, d), h.dtype), pltpu.SemaphoreType.DMA(())])
    return pl.pallas_call(
        _gather_kernel, grid_spec=grid_spec,
        out_shape=jax.ShapeDtypeStruct((cap, d), BF16),
        compiler_params=_params("arbitrary"), name="moe_gather")(src, n_valid, h)


COMBINE_TB = 256


def _combine_kernel(dest_ref, x_ref, r_ref, g_ref, y_ref, xo_ref, h_ref, buf_ref, sem):
    base = pl.program_id(0) * COMBINE_TB

    def row_copy(t, k):
        return pltpu.make_async_copy(y_ref.at[pl.ds(dest_ref[TOP_K * (base + t) + k], 1)],
                                     buf_ref.at[k, pl.ds(t, 1)], sem)

    def issue_group(c, carry):
        for r in range(DMA_GROUP // TOP_K):
            for k in range(TOP_K):
                row_copy(c * (DMA_GROUP // TOP_K) + r, k).start()
        return carry

    lax.fori_loop(0, COMBINE_TB * TOP_K // DMA_GROUP, issue_group, 0)
    for k in range(TOP_K):
        pltpu.make_async_copy(y_ref.at[pl.ds(0, COMBINE_TB)], buf_ref.at[k], sem).wait()
    r = r_ref[...]
    g1 = r[:, 2:3]
    g2 = r[:, 3:4]
    x = x_ref[...] + (buf_ref[0] * g1 + buf_ref[1] * g2)
    xo_ref[...] = x
    y = x * lax.rsqrt(jnp.mean(x * x, axis=-1, keepdims=True) + EPS)
    h_ref[...] = (y * g_ref[...]).astype(h_ref.dtype)


def moe_combine(x, route, dest, ybuf, gains, layer):
    m, d = x.shape
    row = pl.BlockSpec((COMBINE_TB, d), lambda i, dr: (i, 0))
    grid_spec = pltpu.PrefetchScalarGridSpec(
        num_scalar_prefetch=1, grid=(m // COMBINE_TB,),
        in_specs=[row, pl.BlockSpec((COMBINE_TB, LANES), lambda i, dr: (i, 0)),
                  pl.BlockSpec((None, 1, d), lambda i, dr: (layer, 0, 0)),
                  pl.BlockSpec(memory_space=pl.ANY)],
        out_specs=[row, row],
        scratch_shapes=[pltpu.VMEM((TOP_K, COMBINE_TB, d), F32), pltpu.SemaphoreType.DMA(())])
    return pl.pallas_call(
        _combine_kernel, grid_spec=grid_spec,
        out_shape=[jax.ShapeDtypeStruct((m, d), F32), jax.ShapeDtypeStruct((m, d), BF16)],
        compiler_params=_params("arbitrary"), name="moe_combine")(dest, x, route, gains, ybuf)


def moe_layer(x1, ffn_norm, ple_norm, layer, w_router, e_w1, e_w3, e_w2, j):
    m, d = x1.shape
    wr_pad = jnp.zeros((d, LANES), F32).at[:, :N_EXPERTS].set(w_router)
    h, route = norm_router(x1, ffn_norm, layer, wr_pad)
    e_flat = route[:, :TOP_K].astype(jnp.int32).reshape(-1)
    onehot = (e_flat[:, None] == jnp.arange(N_EXPERTS)[None, :]).astype(jnp.int32)
    csum = jnp.cumsum(onehot, axis=0)
    rank = jnp.sum((csum - onehot) * onehot, axis=1)
    counts = csum[-1]
    tm = MOE_TM
    n_units_e = (counts + tm - 1) // tm
    unit_end = jnp.cumsum(n_units_e)
    unit_start = unit_end - n_units_e
    rows_e = (counts + n_units_e * FFN_SUB - 1) // jnp.maximum(n_units_e * FFN_SUB, 1) * FFN_SUB
    rows_e = jnp.clip(rows_e, FFN_SUB, tm)
    re_flat = rows_e[e_flat]
    dest = ((unit_start[e_flat] + rank // re_flat) * tm + rank % re_flat).astype(jnp.int32)
    n_units = (m * TOP_K) // tm + N_EXPERTS
    uidx = jnp.arange(n_units)
    ue = jnp.minimum(jnp.searchsorted(unit_end, uidx, side='right'), N_EXPERTS - 1).astype(jnp.int32)
    live_rows = jnp.clip(counts[ue] - (uidx - unit_start[ue]) * rows_e[ue], 0, rows_e[ue])
    live_rows = jnp.where(uidx < unit_end[-1], live_rows, 0)
    un = ((live_rows + FFN_SUB - 1) // FFN_SUB).astype(jnp.int32)
    last_live = jnp.maximum(unit_end[-1] - 1, 0)
    ue = jnp.where(uidx < unit_end[-1], ue, ue[last_live]).astype(jnp.int32)
    ub = jnp.where(uidx < unit_end[-1], uidx, last_live).astype(jnp.int32)

    cap = n_units * tm
    src = jnp.zeros((cap,), jnp.int32).at[dest].set(jnp.arange(m * TOP_K, dtype=jnp.int32) // TOP_K)

    xbuf = moe_gather(h, src, live_rows.astype(jnp.int32), cap, tm)
    ybuf = swiglu_ffn(xbuf, ue, un, ub, e_w1, e_w3, e_w2, (j,), range(1, tm // FFN_SUB + 1), tm, MOE_TF,
                      MOE_FC)
    return moe_combine(x1, route, dest, ybuf, ple_norm, layer)


def kernel(x, p, mix_norm, w_in, q_norm, k_norm, ssm_lambda_re, ssm_lambda_im, ssm_log_dt, ssm_b_re, ssm_b_im,
           ssm_c_re, ssm_c_im, ssm_d, ssm_glu_w, gmlp_v_norm, gmlp_ws, gmlp_b, w_branch, w_out, ffn_norm,
           dense_w1, dense_w3, dense_w2, router_w, expert_w1, expert_w3, expert_w2, ple_norm, ple_gate_w,
           ple_proj_w):
    batch, seq, d = x.shape
    depth = w_in.shape[0]
    n_in = w_in.shape[-1]
    m = batch * seq
    xs = x.reshape(m, d)
    rope_c, rope_s = rope_tables(seq)

    def g3(a):
        return a.reshape(a.shape[0], 1, a.shape[1])

    mix_norm, q_norm, k_norm, gmlp_v_norm, ffn_norm, ple_norm = map(
        g3, (mix_norm, q_norm, k_norm, gmlp_v_norm, ffn_norm, ple_norm))
    b_bcast = jnp.broadcast_to(gmlp_b[..., None], gmlp_b.shape + (MLP_GROUP_W,))
    p2 = p.reshape(depth, m, p.shape[-1])
    dense_units = m // DENSE_TM
    dense_un = jnp.full((dense_units,), DENSE_TM // FFN_SUB, jnp.int32)

    for i in range(depth):
        _, h = add_norm(xs, None, mix_norm, i)
        proj = fused_mm(m, n_in, [(h, 0, d, w_in, (i,), 0)], [], lambda dts, ex: dts[0], BF16,
                        tm=min(2048, m), tn=768, name="in_proj")
        attn = attention(proj, q_norm, k_norm, i, rope_c, rope_s, batch, seq)
        tabs = _ssm_tables(ssm_lambda_re[i], ssm_lambda_im[i], ssm_log_dt[i], ssm_b_re[i], ssm_b_im[i],
                           ssm_c_re[i], ssm_c_im[i], ssm_d[i], seq // SSM_SEGS)
        ssm = ssm_branch(proj, i, tabs, ssm_glu_w, batch, seq)
        mlp = gmlp_branch(proj, i, gmlp_v_norm, gmlp_ws, b_bcast, m)

        tn = 512
        merged = fused_mm(
            m, d,
            [(br, 0, br.shape[1], w_branch, (i, n), 0) for n, br in enumerate((attn, ssm, mlp))],
            [(proj, (GATE_OFF + n * d) // tn) for n in range(N_BRANCH)],
            lambda dts, ex: sum(_sigmoid(e.astype(F32)) * dt for e, dt in zip(ex, dts)),
            BF16, tm=1024, tn=tn, name="branch_merge")
        x1 = fused_mm(m, d, [(merged, 0, d, w_out, (i,), 0)], [(xs, 0)],
                      lambda dts, ex: ex[0] + dts[0], F32, tm=min(2048, m), tn=512, name="out_proj")

        j = i // 2
        if i % 2 == 0:
            _, h2 = add_norm(x1, None, ffn_norm, i)
            dense_ue = jnp.full((dense_units,), j, jnp.int32)
            y = swiglu_ffn(h2, dense_ue, dense_un, jnp.arange(dense_units, dtype=jnp.int32),
                           dense_w1, dense_w3, dense_w2, (), (DENSE_TM // FFN_SUB,), DENSE_TM, DENSE_TF,
                           DENSE_TF)
            x2, hn = add_norm(x1, y, ple_norm, i)
        else:
            x2, hn = moe_layer(x1, ffn_norm, ple_norm, i, router_w[j], expert_w1, expert_w3, expert_w2, j)

        xs = fused_mm(
            m, d,
            [(hn, 0, d, ple_gate_w, (i,), 0), ((p2, (i,)), 0, p.shape[-1], ple_proj_w, (i,), 0)],
            [(x2, 0)],
            lambda dts, ex: ex[0] + _sigmoid(dts[0]) * dts[1], F32, tm=min(2048, m), tn=512, name="ple")
    return xs.reshape(batch, seq, d)
```
